```python
import math, functools
import jax, jax.numpy as jnp
from jax import lax
import numpy as np

D_MODEL = 1024
BATCH = 16
SEQ = 2048
DEPTH = 1
DEC_BATCH = 128
DEC_SEQ = 8
PAST_LEN = 16384
PAGE_SIZE = 128

SSM_WIDTH = D_MODEL // 2
SSM_GROUP = 16
SSM_GROUPS = SSM_WIDTH // SSM_GROUP
SSM_STATE = 64
DT_MIN = 1e-3
DT_MAX = 1e-1
N_HEADS = 8
QK_NOPE = 64
QK_ROPE = 32
V_DIM = 64
Q_RANK = 256
KV_RANK = 256
MLA_WIDTH = N_HEADS * V_DIM
ROPE_THETA = 10000.0
ATTN_SCALE = (QK_NOPE + QK_ROPE) ** -0.5
Q_BLOCK = 128
IN_COLS = SSM_WIDTH + Q_RANK + KV_RANK + QK_ROPE + 2 * D_MODEL
SPLITS = (SSM_WIDTH, SSM_WIDTH + Q_RANK, SSM_WIDTH + Q_RANK + KV_RANK,
          SSM_WIDTH + Q_RANK + KV_RANK + QK_ROPE,
          SSM_WIDTH + Q_RANK + KV_RANK + QK_ROPE + D_MODEL)
N_GROUPS = 4
EXPERTS_PER_GROUP = 8
N_EXPERTS = N_GROUPS * EXPERTS_PER_GROUP
TOP_K = 2
D_EXPERT = 256
PLE_DIM = 256
LN_EPS = 1e-5
RMS_EPS = 1e-6
ALPHA = (2 * DEPTH) ** 0.25
BETA = (8 * DEPTH) ** -0.25

kernel_name = "hybrid_s5_mla_hmoe_decode_step"


def layer_norm(x, g, b):
    xf = x.astype(jnp.float32)
    mu = jnp.mean(xf, -1, keepdims=True)
    var = jnp.mean(jnp.square(xf - mu), -1, keepdims=True)
    return ((xf - mu) * lax.rsqrt(var + LN_EPS) * g.astype(jnp.float32) + b.astype(jnp.float32)).astype(x.dtype)


def rms_norm(x, g):
    xf = x.astype(jnp.float32)
    ms = jnp.mean(jnp.square(xf), -1, keepdims=True)
    return (xf * lax.rsqrt(ms + RMS_EPS) * g.astype(jnp.float32)).astype(x.dtype)


def apply_rope(x, pos):
    half = QK_ROPE // 2
    inv = jnp.exp(-math.log(ROPE_THETA) * jnp.arange(half, dtype=jnp.float32) / half)
    ang = pos.astype(jnp.float32)[:, None] * inv[None, :]
    cos = jnp.cos(ang)[None, :, None, :]
    sin = jnp.sin(ang)[None, :, None, :]
    x1 = x[..., :half].astype(jnp.float32)
    x2 = x[..., half:].astype(jnp.float32)
    return jnp.concatenate([x1 * cos - x2 * sin, x2 * cos + x1 * sin], -1).astype(x.dtype)


def _complex_affine_combine(e1, e2):
    a1r, a1i, b1r, b1i = e1
    a2r, a2i, b2r, b2i = e2
    return (a2r * a1r - a2i * a1i,
            a2r * a1i + a2i * a1r,
            a2r * b1r - a2i * b1i + b2r,
            a2r * b1i + a2i * b1r + b2i)


def s5_branch(u, s0_re, s0_im, a_re, a_im, log_dt, b_re, b_im, c_re, c_im, d_skip):
    B, S, _ = u.shape
    f32 = jnp.float32
    dt = jnp.exp(log_dt.astype(f32))[:, None]
    ar = a_re.astype(f32)
    ai = a_im.astype(f32)
    mag = jnp.exp(dt * ar)
    ph = dt * ai
    lr = mag * jnp.cos(ph)
    li = mag * jnp.sin(ph)
    den = ar * ar + ai * ai
    fr = ((lr - 1.0) * ar + li * ai) / den
    fi = (li * ar - (lr - 1.0) * ai) / den
    br = b_re.astype(f32)
    bi = b_im.astype(f32)
    bbr = fr[..., None] * br - fi[..., None] * bi
    bbi = fr[..., None] * bi + fi[..., None] * br
    uf = u.astype(f32)
    ug = uf.reshape(B, S, SSM_GROUPS, SSM_GROUP)
    bu_r = jnp.einsum("bsgc,gnc->bsgn", ug, bbr)
    bu_i = jnp.einsum("bsgc,gnc->bsgn", ug, bbi)
    s0r = s0_re.astype(f32)
    s0i = s0_im.astype(f32)
    bu_r = bu_r.at[:, 0].add(lr * s0r - li * s0i)
    bu_i = bu_i.at[:, 0].add(lr * s0i + li * s0r)
    lam_r = jnp.broadcast_to(lr, bu_r.shape)
    lam_i = jnp.broadcast_to(li, bu_i.shape)
    _, _, sr, si = lax.associative_scan(_complex_affine_combine, (lam_r, lam_i, bu_r, bu_i), axis=1)
    y = (jnp.einsum("bsgn,gcn->bsgc", sr, c_re.astype(f32))
         - jnp.einsum("bsgn,gcn->bsgc", si, c_im.astype(f32)))
    y = y.reshape(B, S, SSM_WIDTH) + d_skip.astype(f32) * uf
    return y.astype(u.dtype), sr[:, -1].astype(s0_re.dtype), si[:, -1].astype(s0_im.dtype)


def mla_prompt_attention(q_lat, q_rope, c_kv, k_rope):
    B, S = q_lat.shape[:2]
    nb = S // Q_BLOCK
    ql = q_lat.reshape(B, nb, Q_BLOCK, N_HEADS, KV_RANK).transpose(1, 0, 2, 3, 4)
    qr = q_rope.reshape(B, nb, Q_BLOCK, N_HEADS, QK_ROPE).transpose(1, 0, 2, 3, 4)
    kpos = jnp.arange(S, dtype=jnp.int32)

    def block(args):
        qlb, qrb, i = args
        s = (jnp.einsum("bqhr,bkr->bhqk", qlb, c_kv).astype(jnp.float32)
             + jnp.einsum("bqhd,bkd->bhqk", qrb, k_rope).astype(jnp.float32)) * ATTN_SCALE
        qpos = i * Q_BLOCK + jnp.arange(Q_BLOCK, dtype=jnp.int32)
        s = jnp.where(kpos[None, :] <= qpos[:, None], s, -jnp.inf)
        p = jax.nn.softmax(s, axis=-1).astype(c_kv.dtype)
        return jnp.einsum("bhqk,bkr->bqhr", p, c_kv)

    o = lax.map(block, (ql, qr, jnp.arange(nb, dtype=jnp.int32)))
    return o.transpose(1, 0, 2, 3, 4).reshape(B, S, N_HEADS, KV_RANK)


def mla_sample_attention(q_lat, q_rope, c_new, kr_new, c_past, kr_past):
    T = c_new.shape[1]
    P = c_past.shape[1]
    s_past = (jnp.einsum("bqhr,bkr->bhqk", q_lat, c_past).astype(jnp.float32)
              + jnp.einsum("bqhd,bkd->bhqk", q_rope, kr_past).astype(jnp.float32)) * ATTN_SCALE
    s_new = (jnp.einsum("bqhr,bkr->bhqk", q_lat, c_new).astype(jnp.float32)
             + jnp.einsum("bqhd,bkd->bhqk", q_rope, kr_new).astype(jnp.float32)) * ATTN_SCALE
    causal = jnp.arange(T)[None, :] <= jnp.arange(T)[:, None]
    s_new = jnp.where(causal, s_new, -jnp.inf)
    p = jax.nn.softmax(jnp.concatenate([s_past, s_new], -1), axis=-1).astype(c_new.dtype)
    return (jnp.einsum("bhqk,bkr->bqhr", p[..., :P], c_past)
            + jnp.einsum("bhqk,bkr->bqhr", p[..., P:], c_new))


def hier_moe(x, w_gr, b_gr, w_er, b_er, w_e_gate, w_e_up, w_e_down):
    B, S, D = x.shape
    xt = x.reshape(-1, D)
    g_logits = (xt @ w_gr + b_gr).astype(jnp.float32)
    g_prob = jax.nn.softmax(g_logits, axis=-1)
    g_sel = jnp.argmax(g_logits, axis=-1)
    p_group = jnp.take_along_axis(g_prob, g_sel[:, None], axis=-1)
    e_logits = (xt @ w_er + b_er).astype(jnp.float32).reshape(-1, N_GROUPS, EXPERTS_PER_GROUP)
    e_in_group = jnp.take_along_axis(e_logits, g_sel[:, None, None], axis=1)[:, 0]
    top_v, top_i = lax.top_k(e_in_group, TOP_K)
    w_top = jax.nn.softmax(top_v, axis=-1) * p_group
    expert_ids = g_sel[:, None] * EXPERTS_PER_GROUP + top_i
    gates = jnp.sum(jax.nn.one_hot(expert_ids, N_EXPERTS, dtype=jnp.float32) * w_top[..., None], axis=1)

    def expert_step(acc, ew):
        wg, wu, wd, ge = ew
        hdn = jax.nn.silu(xt @ wg) * (xt @ wu)
        return acc + ge[:, None].astype(x.dtype) * (hdn @ wd), None

    acc, _ = lax.scan(expert_step, jnp.zeros_like(xt), (w_e_gate, w_e_up, w_e_down, gates.T))
    return acc.reshape(B, S, D)


def decoder_layer(x, p_emb, pos, s0_re, s0_im, attend, lw):
    B, S, _ = x.shape
    h = x @ lw["w_in"]
    u, cq, ckv, kr, g_ssm, g_mla = jnp.split(h, SPLITS, axis=-1)
    y_ssm, s_re, s_im = s5_branch(u, s0_re, s0_im, lw["a_re"], lw["a_im"], lw["log_dt"],
                                  lw["b_re"], lw["b_im"], lw["c_re"], lw["c_im"], lw["d_skip"])
    z = jax.nn.gelu(y_ssm)
    ssm_out = z * jax.nn.sigmoid(z @ lw["w_glu"] + lw["b_glu"])
    cq = rms_norm(cq, lw["q_norm_g"])
    q = (cq @ lw["w_uq"]).reshape(B, S, N_HEADS, QK_NOPE + QK_ROPE)
    q_nope = q[..., :QK_NOPE]
    q_rope = apply_rope(q[..., QK_NOPE:], pos)
    ckv = rms_norm(ckv, lw["kv_norm_g"])
    kr = apply_rope(kr[:, :, None, :], pos)[:, :, 0, :]
    q_lat = jnp.einsum("bshd,rhd->bshr", q_nope, lw["w_uk"])
    o_lat = attend(q_lat, q_rope, ckv, kr)
    mla_out = jnp.einsum("bshr,rhd->bshd", o_lat, lw["w_uv"]).reshape(B, S, MLA_WIDTH)
    merged = (jax.nn.sigmoid(g_ssm) * (ssm_out @ lw["w_br_ssm"])
              + jax.nn.sigmoid(g_mla) * (mla_out @ lw["w_br_mla"]))
    x = layer_norm(ALPHA * x + merged @ lw["w_out"], lw["ln1_g"], lw["ln1_b"])
    moe = hier_moe(x, lw["w_gr"], lw["b_gr"], lw["w_er"], lw["b_er"],
                   lw["w_e_gate"], lw["w_e_up"], lw["w_e_down"])
    x = layer_norm(ALPHA * x + moe, lw["ln2_g"], lw["ln2_b"])
    x = x + jax.nn.sigmoid(x @ lw["w_ple_gate"] + lw["b_ple_gate"]) * (p_emb @ lw["w_ple_proj"])
    return x, ckv, kr, s_re, s_im


def setup_inputs(seed: int = 0) -> dict:
    key = jax.random.key(seed)
    keys = list(jax.random.split(key, 48))

    def nrm(shape, scale):
        return scale * jax.random.normal(keys.pop(), shape, jnp.float32)

    L = DEPTH
    n_pages = PAST_LEN // PAGE_SIZE
    n_used = DEC_BATCH * n_pages
    n_pool = n_used + max(1, n_used // 4)
    page_table = jax.random.permutation(keys.pop(), n_pool)[:n_used].reshape(DEC_BATCH, n_pages).astype(jnp.int32)
    n_idx = jnp.arange(SSM_STATE, dtype=jnp.float32)
    cplx = math.sqrt(0.5)
    return {
        "x_prompt": nrm((BATCH, SEQ, D_MODEL), 1.0),
        "x_sample": nrm((DEC_BATCH, DEC_SEQ, D_MODEL), 1.0),
        "p_prompt": nrm((L, BATCH, SEQ, PLE_DIM), 1.0),
        "p_sample": nrm((L, DEC_BATCH, DEC_SEQ, PLE_DIM), 1.0),
        "cache_kv": nrm((L, n_pool, PAGE_SIZE, KV_RANK), 1.0),
        "cache_k_rope": nrm((L, n_pool, PAGE_SIZE, QK_ROPE), 1.0),
        "state_ssm_re": nrm((L, DEC_BATCH, SSM_GROUPS, SSM_STATE), 0.1),
        "state_ssm_im": nrm((L, DEC_BATCH, SSM_GROUPS, SSM_STATE), 0.1),
        "page_table": page_table,
        "w_in": nrm((L, D_MODEL, IN_COLS), D_MODEL ** -0.5),
        "a_re": -0.5 + nrm((L, SSM_GROUPS, SSM_STATE), 1e-3),
        "a_im": math.pi * n_idx + nrm((L, SSM_GROUPS, SSM_STATE), 1e-3),
        "log_dt": jax.random.uniform(keys.pop(), (L, SSM_GROUPS), jnp.float32, math.log(DT_MIN), math.log(DT_MAX)),
        "b_re": nrm((L, SSM_GROUPS, SSM_STATE, SSM_GROUP), cplx * SSM_GROUP ** -0.5),
        "b_im": nrm((L, SSM_GROUPS, SSM_STATE, SSM_GROUP), cplx * SSM_GROUP ** -0.5),
        "c_re": nrm((L, SSM_GROUPS, SSM_GROUP, SSM_STATE), cplx * SSM_STATE ** -0.5),
        "c_im": nrm((L, SSM_GROUPS, SSM_GROUP, SSM_STATE), cplx * SSM_STATE ** -0.5),
        "d_skip": nrm((L, SSM_WIDTH), 1.0),
        "w_glu": nrm((L, SSM_WIDTH, SSM_WIDTH), SSM_WIDTH ** -0.5),
        "b_glu": nrm((L, SSM_WIDTH), 0.01),
        "w_br_ssm": nrm((L, SSM_WIDTH, D_MODEL), SSM_WIDTH ** -0.5),
        "q_norm_g": 1.0 + nrm((L, Q_RANK), 0.01),
        "w_uq": nrm((L, Q_RANK, N_HEADS * (QK_NOPE + QK_ROPE)), Q_RANK ** -0.5),
        "kv_norm_g": 1.0 + nrm((L, KV_RANK), 0.01),
        "w_uk": nrm((L, KV_RANK, N_HEADS, QK_NOPE), KV_RANK ** -0.5),
        "w_uv": nrm((L, KV_RANK, N_HEADS, V_DIM), KV_RANK ** -0.5),
        "w_br_mla": nrm((L, MLA_WIDTH, D_MODEL), MLA_WIDTH ** -0.5),
        "w_out": nrm((L, D_MODEL, D_MODEL), BETA * D_MODEL ** -0.5),
        "ln1_g": 1.0 + nrm((L, D_MODEL), 0.01),
        "ln1_b": nrm((L, D_MODEL), 0.01),
        "w_gr": nrm((L, D_MODEL, N_GROUPS), D_MODEL ** -0.5),
        "b_gr": nrm((L, N_GROUPS), 0.01),
        "w_er": nrm((L, D_MODEL, N_EXPERTS), D_MODEL ** -0.5),
        "b_er": nrm((L, N_EXPERTS), 0.01),
        "w_e_gate": nrm((L, N_EXPERTS, D_MODEL, D_EXPERT), D_MODEL ** -0.5),
        "w_e_up": nrm((L, N_EXPERTS, D_MODEL, D_EXPERT), D_MODEL ** -0.5),
        "w_e_down": nrm((L, N_EXPERTS, D_EXPERT, D_MODEL), BETA * D_EXPERT ** -0.5),
        "ln2_g": 1.0 + nrm((L, D_MODEL), 0.01),
        "ln2_b": nrm((L, D_MODEL), 0.01),
        "w_ple_gate": nrm((L, D_MODEL, D_MODEL), D_MODEL ** -0.5),
        "b_ple_gate": nrm((L, D_MODEL), 0.01),
        "w_ple_proj": nrm((L, PLE_DIM, D_MODEL), PLE_DIM ** -0.5),
    }


def reference(x_prompt, x_sample, p_prompt, p_sample, cache_kv, cache_k_rope, state_ssm_re, state_ssm_im,
              page_table, w_in, a_re, a_im, log_dt, b_re, b_im, c_re, c_im, d_skip, w_glu, b_glu, w_br_ssm,
              q_norm_g, w_uq, kv_norm_g, w_uk, w_uv, w_br_mla, w_out, ln1_g, ln1_b,
              w_gr, b_gr, w_er, b_er, w_e_gate, w_e_up, w_e_down, ln2_g, ln2_b,
              w_ple_gate, b_ple_gate, w_ple_proj):
    pos_prompt = jnp.arange(SEQ, dtype=jnp.int32)
    pos_sample = PAST_LEN + jnp.arange(DEC_SEQ, dtype=jnp.int32)
    xp = x_prompt
    xs = x_sample
    kv_p, kr_p, sre_p, sim_p = [], [], [], []
    kv_s, kr_s, sre_s, sim_s = [], [], [], []
    for i in range(DEPTH):
        lw = dict(w_in=w_in[i], a_re=a_re[i], a_im=a_im[i], log_dt=log_dt[i], b_re=b_re[i], b_im=b_im[i],
                  c_re=c_re[i], c_im=c_im[i], d_skip=d_skip[i], w_glu=w_glu[i], b_glu=b_glu[i],
                  w_br_ssm=w_br_ssm[i], q_norm_g=q_norm_g[i], w_uq=w_uq[i], kv_norm_g=kv_norm_g[i],
                  w_uk=w_uk[i], w_uv=w_uv[i], w_br_mla=w_br_mla[i], w_out=w_out[i],
                  ln1_g=ln1_g[i], ln1_b=ln1_b[i], w_gr=w_gr[i], b_gr=b_gr[i], w_er=w_er[i], b_er=b_er[i],
                  w_e_gate=w_e_gate[i], w_e_up=w_e_up[i], w_e_down=w_e_down[i],
                  ln2_g=ln2_g[i], ln2_b=ln2_b[i], w_ple_gate=w_ple_gate[i], b_ple_gate=b_ple_gate[i],
                  w_ple_proj=w_ple_proj[i])
        zero_state = jnp.zeros((BATCH, SSM_GROUPS, SSM_STATE), state_ssm_re.dtype)
        xp, ckv, kr, sre, sim = decoder_layer(xp, p_prompt[i], pos_prompt, zero_state, zero_state,
                                              mla_prompt_attention, lw)
        kv_p.append(ckv)
        kr_p.append(kr)
        sre_p.append(sre)
        sim_p.append(sim)
        c_past = cache_kv[i][page_table].reshape(DEC_BATCH, -1, KV_RANK)
        kr_past = cache_k_rope[i][page_table].reshape(DEC_BATCH, -1, QK_ROPE)
        attend_s = functools.partial(mla_sample_attention, c_past=c_past, kr_past=kr_past)
        xs, ckv, kr, sre, sim = decoder_layer(xs, p_sample[i], pos_sample, state_ssm_re[i], state_ssm_im[i],
                                              attend_s, lw)
        kv_s.append(ckv)
        kr_s.append(kr)
        sre_s.append(sre)
        sim_s.append(sim)
    return (xp, xs,
            jnp.stack(kv_p), jnp.stack(kr_p), jnp.stack(sre_p), jnp.stack(sim_p),
            jnp.stack(kv_s), jnp.stack(kr_s), jnp.stack(sre_s), jnp.stack(sim_s))
```

```python
import functools
import math

import jax
import jax.numpy as jnp
from jax import lax
from jax.experimental import pallas as pl
from jax.experimental.pallas import tpu as pltpu

F32 = jnp.float32
BF16 = jnp.bfloat16

D_MODEL = 1024
DEPTH = 1
PAST_LEN = 16384
PAGE_SIZE = 128
SSM_WIDTH = 512
SSM_GROUP = 16
SSM_GROUPS = 32
SSM_STATE = 64
STATE_COLS = 2 * SSM_GROUPS * SSM_STATE
N_HEADS = 8
QK_NOPE = 64
QK_ROPE = 32
V_DIM = 64
Q_RANK = 256
KV_RANK = 256
HEAD_PAD = 128
ROPE_THETA = 10000.0
ATTN_SCALE = (QK_NOPE + QK_ROPE) ** -0.5
N_GROUPS = 4
EXPERTS_PER_GROUP = 8
D_EXPERT = 256
PLE_DIM = 256
LN_EPS = 1e-5
RMS_EPS = 1e-6
ALPHA = (2 * DEPTH) ** 0.25

LANES = 128
MXU_DIM = 256
VMEM_LIMIT = 56 * 1024 * 1024

NEG_INF = float("-inf")


def _params(sem, vmem=VMEM_LIMIT):
    return pltpu.CompilerParams(dimension_semantics=sem, vmem_limit_bytes=vmem)


def _const_spec(shape):
    zeros = (0,) * len(shape)
    return pl.BlockSpec(shape, lambda *_: zeros)


def _dot(a, b):
    return jnp.dot(a, b, preferred_element_type=F32)


def _dot_t(a, b):
    return lax.dot_general(a, b, (((1,), (1,)), ((), ())), preferred_element_type=F32)


def _split_bf16(a):
    hi = a.astype(BF16)
    lo = (a - hi.astype(F32)).astype(BF16)
    return hi, lo


def _dot_3pass(x, w_hi, w_lo):
    x_hi, x_lo = _split_bf16(x)
    return _dot(x_hi, w_hi) + (_dot(x_lo, w_hi) + _dot(x_hi, w_lo))


def _layer_norm(x, g, b):
    mu = jnp.mean(x, axis=-1, keepdims=True)
    xc = x - mu
    var = jnp.mean(xc * xc, axis=-1, keepdims=True)
    return xc * lax.rsqrt(var + LN_EPS) * g + b


def _rms_norm(x, g):
    ms = jnp.mean(x * x, axis=-1, keepdims=True)
    return x * lax.rsqrt(ms + RMS_EPS) * g


def _proj_kernel(x_ref, w_ref, qg_ref, kg_ref, cos_ref, sin_ref, wq_ref, wqr_ref, *rest, absorbed):
    if absorbed:
        (wuk_ref, wqc_ref, wqcr_ref, cos8_ref, sin8_ref,
         u_ref, ckv_ref, kr_ref, gs_ref, gm_ref, ql_ref, qr_ref) = rest
    else:
        wk_ref, wv_ref, u_ref, ckv_ref, kr_ref, gs_ref, gm_ref, q_ref, k_ref, v_ref = rest
    xb = x_ref[...].astype(BF16)
    u_ref[...] = _dot(xb, w_ref[:, 0:512])
    cq = _rms_norm(_dot(xb, w_ref[:, 512:768]), qg_ref[...])
    ckv = _rms_norm(_dot(xb, w_ref[:, 768:1024]), kg_ref[...])
    ckv_ref[...] = ckv
    gs_ref[...] = jax.nn.sigmoid(_dot(xb, w_ref[:, 1024:2048])).astype(BF16)
    gm_ref[...] = jax.nn.sigmoid(_dot(xb, w_ref[:, 2048:3072])).astype(BF16)
    cos = cos_ref[...]
    sin = sin_ref[...]
    kr = _dot(xb, w_ref[:, 3072:3200]) * cos + _dot(xb, w_ref[:, 3200:3328]) * sin
    kr_ref[...] = kr

    cqb = cq.astype(BF16)
    qa = _dot(cqb, wq_ref[...])
    qb = _dot(cqb, wqr_ref[...])
    lane = lax.broadcasted_iota(jnp.int32, cos.shape, 1)
    cq_tab = (cos + jnp.where(lane < QK_NOPE, 1.0, 0.0)) * ATTN_SCALE
    sq_tab = sin * ATTN_SCALE
    ckvb = ckv.astype(BF16)
    if absorbed:
        for h in range(N_HEADS):
            sl = slice(h * HEAD_PAD, (h + 1) * HEAD_PAD)
            qh = (qa[:, sl] * cq_tab + qb[:, sl] * sq_tab).astype(BF16)
            ql_ref[:, h * KV_RANK:(h + 1) * KV_RANK] = _dot(qh, wuk_ref[h]).astype(BF16)
        qc = _dot(cqb, wqc_ref[...]) * cos8_ref[...] + _dot(cqb, wqcr_ref[...]) * sin8_ref[...]
        qr_ref[...] = (qc * ATTN_SCALE).astype(BF16)
    else:
        kn = _dot(ckvb, wk_ref[...])
        vv = _dot(ckvb, wv_ref[...])
        for h in range(N_HEADS):
            sl = slice(h * HEAD_PAD, (h + 1) * HEAD_PAD)
            q_ref[h] = (qa[:, sl] * cq_tab + qb[:, sl] * sq_tab).astype(BF16)
            k_ref[h] = (kn[:, sl] + kr).astype(BF16)
            v_ref[h] = vv[:, sl].astype(BF16)


def _proj(x, wts, tabs, *, tm, seq_tiles, absorbed):
    T = x.shape[0]
    nt = T // tm
    row = lambda i: (i, 0)
    if seq_tiles is None:
        tab_map = lambda i: (0, 0)
        u_shape, u_spec = (T, SSM_WIDTH), pl.BlockSpec((tm, SSM_WIDTH), row)
    else:
        tab_map = lambda i: (i % seq_tiles, 0)
        u_shape = (seq_tiles * tm, (nt // seq_tiles) * SSM_WIDTH)
        u_spec = pl.BlockSpec((tm, SSM_WIDTH), lambda i: (i % seq_tiles, i // seq_tiles))
    in_specs = [
        pl.BlockSpec((tm, D_MODEL), row),
        _const_spec(wts["w_in"].shape),
        _const_spec((1, Q_RANK)), _const_spec((1, KV_RANK)),
        pl.BlockSpec((tm, HEAD_PAD), tab_map), pl.BlockSpec((tm, HEAD_PAD), tab_map),
        _const_spec(wts["wq"].shape), _const_spec(wts["wq_rot"].shape),
    ]
    args = [x, wts["w_in"], wts["q_norm_g"], wts["kv_norm_g"], tabs["cos128"], tabs["sin128"],
            wts["wq"], wts["wq_rot"]]
    out_shape = [jax.ShapeDtypeStruct(u_shape, F32),
                 jax.ShapeDtypeStruct((T, KV_RANK), F32),
                 jax.ShapeDtypeStruct((T, HEAD_PAD), F32),
                 jax.ShapeDtypeStruct((T, D_MODEL), BF16),
                 jax.ShapeDtypeStruct((T, D_MODEL), BF16)]
    out_specs = [u_spec, pl.BlockSpec((tm, KV_RANK), row), pl.BlockSpec((tm, HEAD_PAD), row),
                 pl.BlockSpec((tm, D_MODEL), row), pl.BlockSpec((tm, D_MODEL), row)]
    if absorbed:
        in_specs += [_const_spec(wts["wuk_abs"].shape), _const_spec(wts["wq_rope"].shape),
                     _const_spec(wts["wq_rope_rot"].shape),
                     pl.BlockSpec((tm, N_HEADS * QK_ROPE), tab_map),
                     pl.BlockSpec((tm, N_HEADS * QK_ROPE), tab_map)]
        args += [wts["wuk_abs"], wts["wq_rope"], wts["wq_rope_rot"], tabs["cos_heads"], tabs["sin_heads"]]
        out_shape += [jax.ShapeDtypeStruct((T, N_HEADS * KV_RANK), BF16),
                      jax.ShapeDtypeStruct((T, N_HEADS * QK_ROPE), BF16)]
        out_specs += [pl.BlockSpec((tm, N_HEADS * KV_RANK), row), pl.BlockSpec((tm, N_HEADS * QK_ROPE), row)]
    else:
        in_specs += [_const_spec(wts["wk"].shape), _const_spec(wts["wv"].shape)]
        args += [wts["wk"], wts["wv"]]
        head = jax.ShapeDtypeStruct((N_HEADS, T, HEAD_PAD), BF16)
        head_spec = pl.BlockSpec((N_HEADS, tm, HEAD_PAD), lambda i: (0, i, 0))
        out_shape += [head, head, head]
        out_specs += [head_spec, head_spec, head_spec]
    return pl.pallas_call(
        functools.partial(_proj_kernel, absorbed=absorbed),
        grid=(nt,), in_specs=in_specs, out_specs=out_specs, out_shape=out_shape,
        compiler_params=_params(("parallel",)),
        name="proj_sample" if absorbed else "proj_prompt",
    )(*args)


def _s5_kernel(u_ref, s0_ref, lam_ref, wb_ref, wc_ref, dskip_ref, wglu_ref, bglu_ref,
               y_ref, sfin_ref, bu_ref, st_ref, *, lt, bb):
    ti = pl.program_id(1)
    rows = lt * bb
    n_tiles = STATE_COLS // MXU_DIM

    @pl.when(ti == 0)
    def _():
        st_ref[...] = s0_ref[...]

    u = u_ref[...].reshape(rows, SSM_WIDTH)
    ub = u.astype(BF16)
    for j in range(n_tiles):
        k0 = MXU_DIM * (j // (n_tiles // 2))
        bu_ref[:, j * MXU_DIM:(j + 1) * MXU_DIM] = _dot(ub[:, k0:k0 + MXU_DIM], wb_ref[j])

    jg = max(1, 64 // bb)
    for j0 in range(0, n_tiles, jg):
        lrs = [jnp.broadcast_to(lam_ref[0:1, (j0 + q) * LANES:(j0 + q + 1) * LANES], (bb, LANES)) for q in range(jg)]
        lis = [jnp.broadcast_to(lam_ref[1:2, (j0 + q) * LANES:(j0 + q + 1) * LANES], (bb, LANES)) for q in range(jg)]

        def body(t, carry, j0=j0, lrs=lrs, lis=lis):
            r0 = pl.multiple_of(t * bb, bb)
            new = []
            for q in range(jg):
                c0 = (j0 + q) * MXU_DIM
                sr, si = carry[2 * q], carry[2 * q + 1]
                nr = lrs[q] * sr - lis[q] * si + bu_ref[pl.ds(r0, bb), c0:c0 + LANES]
                ni = lrs[q] * si + lis[q] * sr + bu_ref[pl.ds(r0, bb), c0 + LANES:c0 + 2 * LANES]
                bu_ref[pl.ds(r0, bb), c0:c0 + LANES] = nr
                bu_ref[pl.ds(r0, bb), c0 + LANES:c0 + 2 * LANES] = ni
                new += [nr, ni]
            return tuple(new)

        init = []
        for q in range(jg):
            c0 = (j0 + q) * MXU_DIM
            init += [st_ref[:, c0:c0 + LANES], st_ref[:, c0 + LANES:c0 + 2 * LANES]]
        fin = lax.fori_loop(0, lt, body, tuple(init), unroll=min(lt, 8))
        for q in range(jg):
            c0 = (j0 + q) * MXU_DIM
            st_ref[:, c0:c0 + LANES] = fin[2 * q]
            st_ref[:, c0 + LANES:c0 + 2 * LANES] = fin[2 * q + 1]

    half = STATE_COLS // 2
    y = jnp.concatenate([_dot(bu_ref[:, 0:half].astype(BF16), wc_ref[0]),
                         _dot(bu_ref[:, half:STATE_COLS].astype(BF16), wc_ref[1])], axis=1)
    y = y + dskip_ref[...] * u
    z = jax.nn.gelu(y, approximate=True)
    gate = jax.nn.sigmoid(_dot(z.astype(BF16), wglu_ref[...]) + bglu_ref[...])
    y_ref[...] = (z * gate).astype(BF16).reshape(lt, bb, SSM_WIDTH)

    @pl.when(ti == pl.num_programs(1) - 1)
    def _():
        sfin_ref[...] = st_ref[...]


def _s5(u3, s0, wts, *, lt, bb, name):
    S, B, _ = u3.shape
    return pl.pallas_call(
        functools.partial(_s5_kernel, lt=lt, bb=bb),
        grid=(B // bb, S // lt),
        in_specs=[pl.BlockSpec((lt, bb, SSM_WIDTH), lambda b, t: (t, b, 0)),
                  pl.BlockSpec((bb, STATE_COLS), lambda b, t: (b, 0)),
                  _const_spec(wts["lam"].shape), _const_spec(wts["s5_wb"].shape),
                  _const_spec(wts["s5_wc"].shape), _const_spec((1, SSM_WIDTH)),
                  _const_spec(wts["w_glu"].shape), _const_spec((1, SSM_WIDTH))],
        out_specs=[pl.BlockSpec((lt, bb, SSM_WIDTH), lambda b, t: (t, b, 0)),
                   pl.BlockSpec((bb, STATE_COLS), lambda b, t: (b, 0))],
        out_shape=[jax.ShapeDtypeStruct((S, B, SSM_WIDTH), BF16),
                   jax.ShapeDtypeStruct((B, STATE_COLS), F32)],
        scratch_shapes=[pltpu.VMEM((lt * bb, STATE_COLS), F32), pltpu.VMEM((bb, STATE_COLS), F32)],
        compiler_params=_params(("parallel", "arbitrary")),
        name=name,
    )(u3, s0, wts["lam"], wts["s5_wb"], wts["s5_wc"], wts["d_skip"], wts["w_glu"], wts["b_glu"])


def _attn_prompt_kernel(q_ref, k_ref, v_ref, o_ref, *, tq):
    qi = pl.program_id(1)
    row = lax.broadcasted_iota(jnp.int32, (tq, tq), 0)
    col = lax.broadcasted_iota(jnp.int32, (tq, tq), 1)
    causal = col <= row
    outs = []
    for h in range(N_HEADS):
        q = q_ref[h]

        def step(kt, carry, masked, h=h, q=q):
            m, l, acc = carry
            k0 = pl.multiple_of(kt * tq, tq)
            s = _dot_t(q, k_ref[h, pl.ds(k0, tq), :])
            if masked:
                s = jnp.where(causal, s, NEG_INF)
            m_new = jnp.maximum(m, jnp.max(s, axis=1, keepdims=True))
            alpha = jnp.exp(m - m_new)
            p = jnp.exp(s - m_new)
            l = alpha * l + jnp.sum(p, axis=1, keepdims=True)
            acc = alpha * acc + _dot(p.astype(BF16), v_ref[h, pl.ds(k0, tq), :])
            return m_new, l, acc

        init = (jnp.full((tq, 1), NEG_INF, F32), jnp.zeros((tq, 1), F32), jnp.zeros((tq, HEAD_PAD), F32))
        carry = lax.fori_loop(0, qi, functools.partial(step, masked=False), init)
        _, l, acc = step(qi, carry, True)
        outs.append(acc / l)
    pairs = [outs[2 * p] + pltpu.roll(outs[2 * p + 1], V_DIM, axis=1) for p in range(N_HEADS // 2)]
    o_ref[...] = jnp.concatenate(pairs, axis=1).astype(BF16)


def _attn_prompt(q3, k3, v3, *, batch, seq, tq):
    nq = seq // tq
    T = batch * seq
    return pl.pallas_call(
        functools.partial(_attn_prompt_kernel, tq=tq),
        grid=(batch, nq),
        in_specs=[pl.BlockSpec((N_HEADS, tq, HEAD_PAD), lambda b, i: (0, b * nq + i, 0)),
                  pl.BlockSpec((N_HEADS, seq, HEAD_PAD), lambda b, i: (0, b, 0)),
                  pl.BlockSpec((N_HEADS, seq, HEAD_PAD), lambda b, i: (0, b, 0))],
        out_specs=pl.BlockSpec((tq, N_HEADS * V_DIM), lambda b, i: (b * nq + i, 0)),
        out_shape=jax.ShapeDtypeStruct((T, N_HEADS * V_DIM), BF16),
        compiler_params=_params(("parallel", "arbitrary")),
        name="attn_prompt",
    )(q3, k3, v3)


def _attn_sample_kernel(pt_ref, ql_ref, qr_ref, cn_ref, krn_ref, *rest, pp, n_new):
    del pt_ref
    kv_refs = rest[:pp]
    kr_refs = rest[pp:2 * pp]
    o_ref, m_ref, l_ref, acc_ref = rest[2 * pp:]
    pc = pl.program_id(1)

    @pl.when(pc == 0)
    def _():
        m_ref[...] = jnp.full(m_ref.shape, NEG_INF, F32)
        l_ref[...] = jnp.zeros(l_ref.shape, F32)
        acc_ref[...] = jnp.zeros(acc_ref.shape, F32)

    ql = ql_ref[...]
    qr = qr_ref[...]

    def update(s, vals):
        m = m_ref[...]
        m_new = jnp.maximum(m, jnp.max(s, axis=1, keepdims=True))
        alpha = jnp.exp(m - m_new)
        p = jnp.exp(s - m_new)
        l_ref[...] = alpha * l_ref[...] + jnp.sum(p, axis=1, keepdims=True)
        acc_ref[...] = alpha * acc_ref[...] + _dot(p.astype(BF16), vals)
        m_ref[...] = m_new

    kv = jnp.concatenate([r[0] for r in kv_refs], axis=0).astype(BF16)
    kr = jnp.concatenate([r[0] for r in kr_refs], axis=0).astype(BF16)
    update(_dot_t(ql, kv) + _dot_t(qr, kr), kv)

    @pl.when(pc == pl.num_programs(1) - 1)
    def _():
        cn = cn_ref[...].astype(BF16)
        krn = krn_ref[...].astype(BF16)
        s = _dot_t(ql, cn) + _dot_t(qr, krn)
        rows = ql.shape[0]
        t_row = jnp.right_shift(lax.broadcasted_iota(jnp.int32, (rows, n_new), 0), 3)
        t_col = lax.broadcasted_iota(jnp.int32, (rows, n_new), 1)
        update(jnp.where(t_col <= t_row, s, NEG_INF), cn)
        o_ref[...] = (acc_ref[...] / l_ref[...]).astype(BF16)


def _attn_sample(page_table, q_lat, q_rope, ckv_new, kr_new, cache_kv, cache_kr, *, batch, n_new, pp):
    rows = n_new * N_HEADS
    n_pages = page_table.shape[1]

    def page_map(k):
        return lambda b, c, pt: (pt[b, c * pp + k], 0, 0)

    in_specs = [pl.BlockSpec((rows, KV_RANK), lambda b, c, pt: (b, 0)),
                pl.BlockSpec((rows, QK_ROPE), lambda b, c, pt: (b, 0)),
                pl.BlockSpec((n_new, KV_RANK), lambda b, c, pt: (b, 0)),
                pl.BlockSpec((n_new, QK_ROPE), lambda b, c, pt: (b, 0))]
    in_specs += [pl.BlockSpec((1, PAGE_SIZE, KV_RANK), page_map(k)) for k in range(pp)]
    in_specs += [pl.BlockSpec((1, PAGE_SIZE, QK_ROPE), page_map(k)) for k in range(pp)]
    grid_spec = pltpu.PrefetchScalarGridSpec(
        num_scalar_prefetch=1, grid=(batch, n_pages // pp), in_specs=in_specs,
        out_specs=pl.BlockSpec((rows, KV_RANK), lambda b, c, pt: (b, 0)),
        scratch_shapes=[pltpu.VMEM((rows, 1), F32), pltpu.VMEM((rows, 1), F32), pltpu.VMEM((rows, KV_RANK), F32)])
    return pl.pallas_call(
        functools.partial(_attn_sample_kernel, pp=pp, n_new=n_new),
        grid_spec=grid_spec,
        out_shape=jax.ShapeDtypeStruct((batch * rows, KV_RANK), BF16),
        compiler_params=_params(("parallel", "arbitrary")),
        name="attn_sample",
    )(page_table, q_lat, q_rope, ckv_new, kr_new, *([cache_kv] * pp), *([cache_kr] * pp))


def _mm_kernel(x_ref, w_ref, o_ref):
    o_ref[...] = _dot(x_ref[...], w_ref[...]).astype(o_ref.dtype)


def _mm(x, w, *, tm, out_dtype, name):
    M, K = x.shape
    N = w.shape[1]
    return pl.pallas_call(
        _mm_kernel, grid=(M // tm,),
        in_specs=[pl.BlockSpec((tm, K), lambda i: (i, 0)), _const_spec((K, N))],
        out_specs=pl.BlockSpec((tm, N), lambda i: (i, 0)),
        out_shape=jax.ShapeDtypeStruct((M, N), out_dtype),
        compiler_params=_params(("parallel",)), name=name,
    )(x, w)


def _merge_kernel(ssm_ref, mla_ref, gs_ref, gm_ref, x_ref, wbs_ref, wbm_ref, wo_ref, g_ref, b_ref,
                  wr_hi_ref, wr_lo_ref, br_ref, x1_ref, route_ref, cnt_ref, run_ref, *, tm):
    i = pl.program_id(0)

    @pl.when(i == 0)
    def _():
        run_ref[...] = jnp.zeros(run_ref.shape, F32)

    a = _dot(ssm_ref[...], wbs_ref[...])
    m = _dot(mla_ref[...], wbm_ref[...])
    merged = gs_ref[...].astype(F32) * a + gm_ref[...].astype(F32) * m
    y = _dot(merged.astype(BF16), wo_ref[...])
    x1 = _layer_norm(ALPHA * x_ref[...] + y, g_ref[...], b_ref[...])
    x1_ref[...] = x1

    logits = _dot_3pass(x1, wr_hi_ref[...], wr_lo_ref[...]) + br_ref[...]
    lane = lax.broadcasted_iota(jnp.int32, (tm, LANES), 1).astype(F32)
    gl = jnp.where(lane < N_GROUPS, logits, NEG_INF)
    gmax = jnp.max(gl, axis=1, keepdims=True)
    g_sel = jnp.min(jnp.where(gl == gmax, lane, float(LANES)), axis=1, keepdims=True)
    onehot = jnp.where(lane == g_sel, 1.0, 0.0)
    r_i = lax.broadcasted_iota(jnp.int32, (tm, tm), 0)
    c_i = lax.broadcasted_iota(jnp.int32, (tm, tm), 1)
    before = jnp.where(c_i < r_i, 1.0, 0.0).astype(BF16)
    rank_all = _dot(before, onehot.astype(BF16)) + run_ref[0:1, :]
    rank = jnp.sum(onehot * rank_all, axis=1, keepdims=True)
    route_ref[...] = jnp.where(lane == 0.0, g_sel, jnp.where(lane == 1.0, rank, 0.0))
    run_ref[0:1, :] = run_ref[0:1, :] + jnp.sum(onehot, axis=0, keepdims=True)

    @pl.when(i == pl.num_programs(0) - 1)
    def _():
        cnt_ref[...] = run_ref[...]


def _merge(ssm, mla, gs, gm, x, wts, *, tm, seq_tiles):
    T = x.shape[0]
    row = lambda i: (i, 0)
    if seq_tiles is None:
        ssm_spec = pl.BlockSpec((tm, SSM_WIDTH), row)
    else:
        ssm_spec = pl.BlockSpec((tm, SSM_WIDTH), lambda i: (i % seq_tiles, i // seq_tiles))
    return pl.pallas_call(
        functools.partial(_merge_kernel, tm=tm),
        grid=(T // tm,),
        in_specs=[ssm_spec, pl.BlockSpec((tm, N_HEADS * V_DIM), row),
                  pl.BlockSpec((tm, D_MODEL), row), pl.BlockSpec((tm, D_MODEL), row),
                  pl.BlockSpec((tm, D_MODEL), row),
                  _const_spec(wts["w_br_ssm"].shape), _const_spec(wts["w_br_mla"].shape),
                  _const_spec(wts["w_out"].shape), _const_spec((1, D_MODEL)), _const_spec((1, D_MODEL)),
                  _const_spec((D_MODEL, LANES)), _const_spec((D_MODEL, LANES)), _const_spec((1, LANES))],
        out_specs=[pl.BlockSpec((tm, D_MODEL), row), pl.BlockSpec((tm, LANES), row),
                   _const_spec((8, LANES))],
        out_shape=[jax.ShapeDtypeStruct((T, D_MODEL), F32), jax.ShapeDtypeStruct((T, LANES), F32),
                   jax.ShapeDtypeStruct((8, LANES), F32)],
        scratch_shapes=[pltpu.VMEM((8, LANES), F32)],
        compiler_params=_params(("arbitrary",)),
        name="merge_ln1_route",
    )(ssm, mla, gs, gm, x, wts["w_br_ssm"], wts["w_br_mla"], wts["w_out"], wts["ln1_g"], wts["ln1_b"],
      wts["wrg_hi"], wts["wrg_lo"], wts["brg"])


def _dispatch_kernel(pad_ref, x_ref, dest_ref, xs_ref, zero_ref, sem, *, tm):
    i = pl.program_id(0)

    def row_copy(r, d):
        return pltpu.make_async_copy(x_ref.at[pl.ds(r, 1)], xs_ref.at[pl.ds(d, 1)], sem)

    def zero_copy(d):
        return pltpu.make_async_copy(zero_ref.at[pl.ds(0, 1)], xs_ref.at[pl.ds(d, 1)], sem)

    @pl.when(i == 0)
    def _():
        zero_ref[...] = jnp.zeros(zero_ref.shape, F32)
        for g in range(N_GROUPS + 1):
            start = pad_ref[g]
            n = pad_ref[N_GROUPS + 1 + g]

            def issue(r, c, start=start):
                zero_copy(start + r).start()
                return c

            def drain(r, c, start=start):
                zero_copy(start + r).wait()
                return c

            lax.fori_loop(0, n, issue, 0)
            lax.fori_loop(0, n, drain, 0)

    def issue(r, c):
        row_copy(r, dest_ref[0, 0, r]).start()
        return c

    def drain(r, c):
        row_copy(r, dest_ref[0, 0, r]).wait()
        return c

    lax.fori_loop(0, tm, issue, 0)
    lax.fori_loop(0, tm, drain, 0)


def _dispatch(x1, dest, pad_info, *, tm, n_rows):
    T = x1.shape[0]
    nt = T // tm
    grid_spec = pltpu.PrefetchScalarGridSpec(
        num_scalar_prefetch=1, grid=(nt,),
        in_specs=[pl.BlockSpec((tm, D_MODEL), lambda i, pad: (i, 0)),
                  pl.BlockSpec((1, 1, tm), lambda i, pad: (i, 0, 0), memory_space=pltpu.SMEM)],
        out_specs=pl.BlockSpec(memory_space=pl.ANY),
        scratch_shapes=[pltpu.VMEM((8, D_MODEL), F32), pltpu.SemaphoreType.DMA(())])
    return pl.pallas_call(
        functools.partial(_dispatch_kernel, tm=tm),
        grid_spec=grid_spec,
        out_shape=jax.ShapeDtypeStruct((n_rows, D_MODEL), F32),
        compiler_params=_params(("arbitrary",)),
        name="moe_dispatch",
    )(pad_info, x1, dest.reshape(nt, 1, tm))


def _moe_kernel(grp_ref, blk_ref, nv_ref, x_ref, wr_hi_ref, wr_lo_ref, br_ref, wg_ref, wu_ref, wd_ref, o_ref,
                *, tm):
    del blk_ref
    i = pl.program_id(0)

    @pl.when(i < nv_ref[0])
    def _():
        g = grp_ref[i]
        x = x_ref[...]
        logits = _dot_3pass(x, wr_hi_ref[0], wr_lo_ref[0]) + br_ref[0]
        lane = lax.broadcasted_iota(jnp.int32, (tm, LANES), 1).astype(F32)
        is_grp = (lane >= EXPERTS_PER_GROUP) & (lane < EXPERTS_PER_GROUP + N_GROUPS)
        gl = jnp.where(is_grp, logits, NEG_INF)
        gmax = jnp.max(gl, axis=1, keepdims=True)
        gexp = jnp.exp(gl - gmax)
        p_group = (jnp.sum(jnp.where(lane == (EXPERTS_PER_GROUP + g).astype(F32), gexp, 0.0), axis=1, keepdims=True)
                   / jnp.sum(gexp, axis=1, keepdims=True))
        el = jnp.where(lane < EXPERTS_PER_GROUP, logits, NEG_INF)
        v1 = jnp.max(el, axis=1, keepdims=True)
        i1 = jnp.min(jnp.where(el == v1, lane, float(LANES)), axis=1, keepdims=True)
        el2 = jnp.where(lane == i1, NEG_INF, el)
        v2 = jnp.max(el2, axis=1, keepdims=True)
        i2 = jnp.min(jnp.where(el2 == v2, lane, float(LANES)), axis=1, keepdims=True)
        e2 = jnp.exp(v2 - v1)
        w1 = p_group / (1.0 + e2)
        w2 = w1 * e2
        gates = jnp.where(lane == i1, w1, jnp.where(lane == i2, w2, 0.0))

        xb = x.astype(BF16)
        hs = []
        for e in range(EXPERTS_PER_GROUP):
            hg = _dot(xb, wg_ref[0, e])
            hu = _dot(xb, wu_ref[0, e])
            ge = jnp.sum(jnp.where(lane == float(e), gates, 0.0), axis=1, keepdims=True)
            hs.append((jax.nn.silu(hg) * hu * ge).astype(BF16))
        o_ref[...] = _dot(jnp.concatenate(hs, axis=1), wd_ref[0])

    @pl.when(i >= nv_ref[0])
    def _():
        o_ref[...] = jnp.zeros(o_ref.shape, F32)


def _moe(xs, tile_grp, tile_blk, n_valid, wts, *, tm):
    n_rows = xs.shape[0]
    nt = n_rows // tm
    grp3 = lambda i, grp, blk, nv: (grp[i], 0, 0)
    grp4 = lambda i, grp, blk, nv: (grp[i], 0, 0, 0)
    rows = lambda i, grp, blk, nv: (blk[i], 0)
    grid_spec = pltpu.PrefetchScalarGridSpec(
        num_scalar_prefetch=3, grid=(nt,),
        in_specs=[pl.BlockSpec((tm, D_MODEL), rows),
                  pl.BlockSpec((1, D_MODEL, LANES), grp3), pl.BlockSpec((1, D_MODEL, LANES), grp3),
                  pl.BlockSpec((1, 1, LANES), grp3),
                  pl.BlockSpec((1, EXPERTS_PER_GROUP, D_MODEL, D_EXPERT), grp4),
                  pl.BlockSpec((1, EXPERTS_PER_GROUP, D_MODEL, D_EXPERT), grp4),
                  pl.BlockSpec((1, EXPERTS_PER_GROUP * D_EXPERT, D_MODEL), grp3)],
        out_specs=pl.BlockSpec((tm, D_MODEL), lambda i, grp, blk, nv: (i, 0)))
    return pl.pallas_call(
        functools.partial(_moe_kernel, tm=tm),
        grid_spec=grid_spec,
        out_shape=jax.ShapeDtypeStruct((n_rows, D_MODEL), F32),
        compiler_params=_params(("arbitrary",)),
        name="moe_experts",
    )(tile_grp, tile_blk, n_valid, xs, wts["wroute_hi"], wts["wroute_lo"], wts["broute"],
      wts["w_e_gate"], wts["w_e_up"], wts["w_e_down"])


def _final_kernel(x1_ref, dest_ref, ys_ref, p_ref, g_ref, b_ref, wpg_ref, bpg_ref, wpp_ref, o_ref, moe_ref, sem,
                  *, tm):
    def row_copy(r, d):
        return pltpu.make_async_copy(ys_ref.at[pl.ds(d, 1)], moe_ref.at[pl.ds(r, 1)], sem)

    def issue(r, c):
        row_copy(r, dest_ref[0, 0, r]).start()
        return c

    def drain(r, c):
        row_copy(r, dest_ref[0, 0, r]).wait()
        return c

    lax.fori_loop(0, tm, issue, 0)
    lax.fori_loop(0, tm, drain, 0)
    x2 = _layer_norm(ALPHA * x1_ref[...] + moe_ref[...], g_ref[...], b_ref[...])
    gate = jax.nn.sigmoid(_dot(x2.astype(BF16), wpg_ref[...]) + bpg_ref[...])
    o_ref[...] = x2 + gate * _dot(p_ref[...].astype(BF16), wpp_ref[...])


def _final(x1, dest, ys, p, wts, *, tm):
    T = x1.shape[0]
    nt = T // tm
    row = lambda i: (i, 0)
    return pl.pallas_call(
        functools.partial(_final_kernel, tm=tm),
        grid=(nt,),
        in_specs=[pl.BlockSpec((tm, D_MODEL), row),
                  pl.BlockSpec((1, 1, tm), lambda i: (i, 0, 0), memory_space=pltpu.SMEM),
                  pl.BlockSpec(memory_space=pl.ANY),
                  pl.BlockSpec((tm, PLE_DIM), row),
                  _const_spec((1, D_MODEL)), _const_spec((1, D_MODEL)),
                  _const_spec((D_MODEL, D_MODEL)), _const_spec((1, D_MODEL)), _const_spec((PLE_DIM, D_MODEL))],
        out_specs=pl.BlockSpec((tm, D_MODEL), row),
        out_shape=jax.ShapeDtypeStruct((T, D_MODEL), F32),
        scratch_shapes=[pltpu.VMEM((tm, D_MODEL), F32), pltpu.SemaphoreType.DMA(())],
        compiler_params=_params(("arbitrary",)),
        name="combine_ln2_ple",
    )(x1, dest.reshape(nt, 1, tm), ys, p, wts["ln2_g"], wts["ln2_b"], wts["w_ple_gate"], wts["b_ple_gate"],
      wts["w_ple_proj"])


def _prep_weights(w_in, a_re, a_im, log_dt, b_re, b_im, c_re, c_im, d_skip, w_glu, b_glu, w_br_ssm,
                  q_norm_g, w_uq, kv_norm_g, w_uk, w_uv, w_br_mla, w_out, ln1_g, ln1_b,
                  w_gr, b_gr, w_er, b_er, w_e_gate, w_e_up, w_e_down, ln2_g, ln2_b,
                  w_ple_gate, b_ple_gate, w_ple_proj):
    half = QK_ROPE // 2
    rot = lambda w: jnp.concatenate([-w[..., half:], w[..., :half]], axis=-1)
    w = {}

    u_w, cq_w, ckv_w, kr_w, gs_w, gm_w = jnp.split(
        w_in, (512, 768, 1024, 1024 + QK_ROPE, 1024 + QK_ROPE + D_MODEL), axis=1)
    slab = lambda c: jnp.pad(c, ((0, 0), (QK_NOPE, HEAD_PAD - QK_NOPE - QK_ROPE)))
    w["w_in"] = jnp.concatenate([u_w, cq_w, ckv_w, gs_w, gm_w, slab(kr_w), slab(rot(kr_w))], axis=1).astype(BF16)
    w["q_norm_g"] = q_norm_g.reshape(1, Q_RANK)
    w["kv_norm_g"] = kv_norm_g.reshape(1, KV_RANK)

    wq3 = w_uq.reshape(Q_RANK, N_HEADS, QK_NOPE + QK_ROPE)
    nope, rope = wq3[..., :QK_NOPE], wq3[..., QK_NOPE:]
    pad_tail = jnp.zeros((Q_RANK, N_HEADS, HEAD_PAD - QK_NOPE - QK_ROPE), F32)
    w["wq"] = jnp.concatenate([nope, rope, pad_tail], -1).reshape(Q_RANK, N_HEADS * HEAD_PAD).astype(BF16)
    w["wq_rot"] = jnp.concatenate([jnp.zeros_like(nope), rot(rope), pad_tail], -1).reshape(
        Q_RANK, N_HEADS * HEAD_PAD).astype(BF16)
    w["wq_rope"] = rope.reshape(Q_RANK, N_HEADS * QK_ROPE).astype(BF16)
    w["wq_rope_rot"] = rot(rope).reshape(Q_RANK, N_HEADS * QK_ROPE).astype(BF16)
    head_pad = lambda a: jnp.pad(a, ((0, 0), (0, 0), (0, HEAD_PAD - a.shape[-1]))).reshape(
        a.shape[0], N_HEADS * HEAD_PAD).astype(BF16)
    w["wk"] = head_pad(w_uk)
    w["wv"] = head_pad(w_uv)
    w["wuk_abs"] = jnp.pad(w_uk.transpose(1, 2, 0), ((0, 0), (0, HEAD_PAD - QK_NOPE), (0, 0))).astype(BF16)
    w["wuv_blockdiag"] = (jnp.eye(N_HEADS, dtype=F32)[:, None, :, None]
                          * w_uv.transpose(1, 0, 2)[:, :, None, :]).reshape(
                              N_HEADS * KV_RANK, N_HEADS * V_DIM).astype(BF16)

    dt = jnp.exp(log_dt)[:, None]
    mag = jnp.exp(dt * a_re)
    lr = mag * jnp.cos(dt * a_im)
    li = mag * jnp.sin(dt * a_im)
    den = a_re * a_re + a_im * a_im
    fr = ((lr - 1.0) * a_re + li * a_im) / den
    fi = (li * a_re - (lr - 1.0) * a_im) / den
    bbr = fr[..., None] * b_re - fi[..., None] * b_im
    bbi = fr[..., None] * b_im + fi[..., None] * b_re
    w["lam"] = jnp.stack([lr.reshape(-1), li.reshape(-1)])
    n_tiles = SSM_GROUPS // 2
    col_group = jnp.broadcast_to((2 * jnp.arange(n_tiles)[:, None, None, None]
                                  + jnp.arange(2)[None, None, :, None]), (n_tiles, 2, 2, SSM_STATE)).reshape(-1)
    ch_group = jnp.arange(SSM_WIDTH) // SSM_GROUP
    mask = ch_group[:, None] == col_group[None, :]
    bcols = jnp.stack([bbr, bbi]).reshape(2, n_tiles, 2, SSM_STATE, SSM_GROUP).transpose(
        1, 0, 2, 3, 4).reshape(STATE_COLS, SSM_GROUP)
    bmat = jnp.where(mask, jnp.tile(bcols.T, (SSM_GROUPS, 1)), 0.0)
    bmat4 = bmat.reshape(2, MXU_DIM, n_tiles, MXU_DIM)
    w["s5_wb"] = jnp.stack([bmat4[j // (n_tiles // 2), :, j, :] for j in range(n_tiles)]).astype(BF16)
    ccols = jnp.stack([c_re, -c_im]).reshape(2, n_tiles, 2, SSM_GROUP, SSM_STATE).transpose(
        1, 0, 2, 4, 3).reshape(STATE_COLS, SSM_GROUP)
    cmat = jnp.where(mask.T, jnp.tile(ccols, (1, SSM_GROUPS)), 0.0)
    hc = STATE_COLS // 2
    w["s5_wc"] = jnp.stack([cmat[:hc, :MXU_DIM], cmat[hc:, MXU_DIM:]]).astype(BF16)
    w["d_skip"] = d_skip.reshape(1, SSM_WIDTH)
    w["w_glu"] = w_glu.astype(BF16)
    w["b_glu"] = b_glu.reshape(1, SSM_WIDTH)

    w["w_br_ssm"] = w_br_ssm.astype(BF16)
    w["w_br_mla"] = w_br_mla.astype(BF16)
    w["w_out"] = w_out.astype(BF16)
    w["ln1_g"], w["ln1_b"] = ln1_g.reshape(1, D_MODEL), ln1_b.reshape(1, D_MODEL)
    w["ln2_g"], w["ln2_b"] = ln2_g.reshape(1, D_MODEL), ln2_b.reshape(1, D_MODEL)

    wrg = jnp.pad(w_gr, ((0, 0), (0, LANES - N_GROUPS)))
    w["wrg_hi"], w["wrg_lo"] = _split_bf16(wrg)
    w["brg"] = jnp.pad(b_gr, (0, LANES - N_GROUPS)).reshape(1, LANES)
    er = w_er.reshape(D_MODEL, N_GROUPS, EXPERTS_PER_GROUP).transpose(1, 0, 2)
    gr = jnp.broadcast_to(w_gr[None], (N_GROUPS, D_MODEL, N_GROUPS))
    wroute = jnp.pad(jnp.concatenate([er, gr], -1), ((0, 0), (0, 0), (0, LANES - EXPERTS_PER_GROUP - N_GROUPS)))
    w["wroute_hi"], w["wroute_lo"] = _split_bf16(wroute)
    broute = jnp.concatenate([b_er.reshape(N_GROUPS, EXPERTS_PER_GROUP),
                              jnp.broadcast_to(b_gr[None], (N_GROUPS, N_GROUPS))], -1)
    w["broute"] = jnp.pad(broute, ((0, 0), (0, LANES - EXPERTS_PER_GROUP - N_GROUPS))).reshape(N_GROUPS, 1, LANES)
    w["w_e_gate"] = w_e_gate.astype(BF16).reshape(N_GROUPS, EXPERTS_PER_GROUP, D_MODEL, D_EXPERT)
    w["w_e_up"] = w_e_up.astype(BF16).reshape(N_GROUPS, EXPERTS_PER_GROUP, D_MODEL, D_EXPERT)
    w["w_e_down"] = w_e_down.astype(BF16).reshape(N_GROUPS, EXPERTS_PER_GROUP * D_EXPERT, D_MODEL)
    w["w_ple_gate"] = w_ple_gate.astype(BF16)
    w["b_ple_gate"] = b_ple_gate.reshape(1, D_MODEL)
    w["w_ple_proj"] = w_ple_proj.astype(BF16)
    return w


def _rope_tables(pos, reps):
    half = QK_ROPE // 2
    inv = jnp.exp(-math.log(ROPE_THETA) * jnp.arange(half, dtype=F32) / half)
    ang = pos.astype(F32)[:, None] * inv[None, :]
    cos2 = jnp.tile(jnp.cos(ang), (1, 2))
    sin2 = jnp.tile(jnp.sin(ang), (1, 2))
    slab = lambda t: jnp.tile(jnp.pad(t, ((0, 0), (QK_NOPE, HEAD_PAD - QK_NOPE - QK_ROPE))), (reps, 1))
    heads = lambda t: jnp.tile(t, (reps, N_HEADS))
    return {"cos128": slab(cos2), "sin128": slab(sin2), "cos_heads": heads(cos2), "sin_heads": heads(sin2)}


def _pack_state(s_re, s_im):
    B = s_re.shape[0]
    st = jnp.stack([s_re.reshape(B, SSM_GROUPS // 2, 2, SSM_STATE), s_im.reshape(B, SSM_GROUPS // 2, 2, SSM_STATE)],
                   axis=2)
    return st.reshape(B, STATE_COLS)


def _unpack_state(s):
    B = s.shape[0]
    st = s.reshape(B, SSM_GROUPS // 2, 2, 2, SSM_STATE)
    return (st[:, :, 0].reshape(B, SSM_GROUPS, SSM_STATE), st[:, :, 1].reshape(B, SSM_GROUPS, SSM_STATE))


def _route_plan(route, counts, *, tm, n_tok):
    g_sel = route[:, 0].astype(jnp.int32)
    rank = route[:, 1].astype(jnp.int32)
    cnt = counts[0, :N_GROUPS].astype(jnp.int32)
    padded = ((cnt + tm - 1) // tm) * tm
    ends = jnp.cumsum(padded)
    offs = ends - padded
    dest = offs[g_sel] + rank
    n_tiles = n_tok // tm + N_GROUPS
    n_valid = ends[-1] // tm
    tile_start = jnp.arange(n_tiles, dtype=jnp.int32) * tm
    tile_grp = jnp.minimum(jnp.sum(tile_start[:, None] >= ends[None, :], axis=1), N_GROUPS - 1).astype(jnp.int32)
    last = jnp.maximum(n_valid - 1, 0)
    tile_blk = jnp.minimum(jnp.arange(n_tiles, dtype=jnp.int32), last)
    tile_grp = jnp.where(jnp.arange(n_tiles) < n_valid, tile_grp, tile_grp[last])
    n_rows = n_tiles * tm
    pad_info = jnp.concatenate([offs + cnt, ends[-1:], padded - cnt, n_rows - ends[-1:]]).astype(jnp.int32)
    return dest, tile_grp, tile_blk, n_valid.reshape(1).astype(jnp.int32), pad_info, n_rows


def _token_tail(ssm, mla, gs, gm, x, p, wts, *, tm, seq_tiles, tm_moe):
    T = x.shape[0]
    x1, route, counts = _merge(ssm, mla, gs, gm, x, wts, tm=tm, seq_tiles=seq_tiles)
    dest, tile_grp, tile_blk, n_valid, pad_info, n_rows = _route_plan(route, counts, tm=tm_moe, n_tok=T)
    xs = _dispatch(x1, dest, pad_info, tm=tm, n_rows=n_rows)
    ys = _moe(xs, tile_grp, tile_blk, n_valid, wts, tm=tm_moe)
    return _final(x1, dest, ys, p, wts, tm=tm)


def kernel(x_prompt, x_sample, p_prompt, p_sample, cache_kv, cache_k_rope, state_ssm_re, state_ssm_im, page_table, w_in, a_re, a_im, log_dt, b_re, b_im, c_re, c_im, d_skip, w_glu, b_glu, w_br_ssm, q_norm_g, w_uq, kv_norm_g, w_uk, w_uv, w_br_mla, w_out, ln1_g, ln1_b, w_gr, b_gr, w_er, b_er, w_e_gate, w_e_up, w_e_down, ln2_g, ln2_b, w_ple_gate, b_ple_gate, w_ple_proj):
    B, S, _ = x_prompt.shape
    Bs, Ss, _ = x_sample.shape
    Tp, Ts = B * S, Bs * Ss
    wts = _prep_weights(w_in[0], a_re[0], a_im[0], log_dt[0], b_re[0], b_im[0], c_re[0], c_im[0], d_skip[0],
                        w_glu[0], b_glu[0], w_br_ssm[0], q_norm_g[0], w_uq[0], kv_norm_g[0], w_uk[0], w_uv[0],
                        w_br_mla[0], w_out[0], ln1_g[0], ln1_b[0], w_gr[0], b_gr[0], w_er[0], b_er[0],
                        w_e_gate[0], w_e_up[0], w_e_down[0], ln2_g[0], ln2_b[0], w_ple_gate[0], b_ple_gate[0],
                        w_ple_proj[0])
    tm = 256
    tm_moe = 256

    tabs_p = _rope_tables(jnp.arange(S, dtype=jnp.int32), 1)
    seq_tiles = S // tm
    u_t, ckv_p, kr_p, gs_p, gm_p, q3, k3, v3 = _proj(
        x_prompt.reshape(Tp, D_MODEL), wts, tabs_p, tm=tm, seq_tiles=seq_tiles, absorbed=False)
    zero_state = jnp.zeros((B, STATE_COLS), F32)
    ssm_p, sfin_p = _s5(u_t.reshape(S, B, SSM_WIDTH), zero_state, wts, lt=32, bb=B, name="s5_prompt")
    mla_p = _attn_prompt(q3, k3, v3, batch=B, seq=S, tq=512)
    y_p = _token_tail(ssm_p.reshape(S, B * SSM_WIDTH), mla_p, gs_p, gm_p, x_prompt.reshape(Tp, D_MODEL),
                      p_prompt.reshape(Tp, PLE_DIM), wts, tm=tm, seq_tiles=seq_tiles, tm_moe=tm_moe)

    tabs_s = _rope_tables(PAST_LEN + jnp.arange(Ss, dtype=jnp.int32), tm // Ss)
    u_s, ckv_s, kr_s, gs_s, gm_s, q_lat, q_rope = _proj(
        x_sample.reshape(Ts, D_MODEL), wts, tabs_s, tm=tm, seq_tiles=None, absorbed=True)
    s0 = _pack_state(state_ssm_re[0], state_ssm_im[0])
    u_s_t = u_s.reshape(Bs, Ss, SSM_WIDTH).transpose(1, 0, 2)
    ssm_s_t, sfin_s = _s5(u_s_t, s0, wts, lt=Ss, bb=64, name="s5_sample")
    ssm_s = ssm_s_t.transpose(1, 0, 2).reshape(Ts, SSM_WIDTH)
    kr_s32 = kr_s[:, QK_NOPE:QK_NOPE + QK_ROPE]
    n_pool = cache_kv.shape[1]
    o_lat = _attn_sample(page_table, q_lat.reshape(Ts * N_HEADS, KV_RANK), q_rope.reshape(Ts * N_HEADS, QK_ROPE),
                         ckv_s, kr_s32, cache_kv.reshape(n_pool, PAGE_SIZE, KV_RANK),
                         cache_k_rope.reshape(n_pool, PAGE_SIZE, QK_ROPE), batch=Bs, n_new=Ss, pp=8)
    mla_s = _mm(o_lat.reshape(Ts, N_HEADS * KV_RANK), wts["wuv_blockdiag"], tm=tm, out_dtype=BF16, name="uv_sample")
    y_s = _token_tail(ssm_s, mla_s, gs_s, gm_s, x_sample.reshape(Ts, D_MODEL), p_sample.reshape(Ts, PLE_DIM), wts,
                      tm=tm, seq_tiles=None, tm_moe=tm_moe)

    sre_p, sim_p = _unpack_state(sfin_p)
    sre_s, sim_s = _unpack_state(sfin_s)
    kr_p32 = kr_p[:, QK_NOPE:QK_NOPE + QK_ROPE]
    return (y_p.reshape(B, S, D_MODEL), y_s.reshape(Bs, Ss, D_MODEL),
            ckv_p.reshape(1, B, S, KV_RANK), kr_p32.reshape(1, B, S, QK_ROPE),
            sre_p[None], sim_p[None],
            ckv_s.reshape(1, Bs, Ss, KV_RANK), kr_s32.reshape(1, Bs, Ss, QK_ROPE),
            sre_s[None], sim_s[None])
```

```python
import functools
import math

import jax
import jax.numpy as jnp
from jax import lax
from jax.experimental import pallas as pl
from jax.experimental.pallas import tpu as pltpu

F32 = jnp.float32
BF16 = jnp.bfloat16

D_MODEL = 1024
DEPTH = 1
PAST_LEN = 16384
PAGE_SIZE = 128
SSM_WIDTH = 512
SSM_GROUP = 16
SSM_GROUPS = 32
SSM_STATE = 64
STATE_COLS = 2 * SSM_GROUPS * SSM_STATE
N_HEADS = 8
QK_NOPE = 64
QK_ROPE = 32
V_DIM = 64
Q_RANK = 256
KV_RANK = 256
HEAD_PAD = 128
ROPE_THETA = 10000.0
ATTN_SCALE = (QK_NOPE + QK_ROPE) ** -0.5
N_GROUPS = 4
EXPERTS_PER_GROUP = 8
D_EXPERT = 256
PLE_DIM = 256
LN_EPS = 1e-5
RMS_EPS = 1e-6
ALPHA = (2 * DEPTH) ** 0.25

LANES = 128
MXU_DIM = 256
VMEM_LIMIT = 56 * 1024 * 1024

ROW_DMA_UNROLL = 8

NEG_INF = float("-inf")


def _params(sem, vmem=VMEM_LIMIT):
    return pltpu.CompilerParams(dimension_semantics=sem, vmem_limit_bytes=vmem)


def _const_spec(shape):
    zeros = (0,) * len(shape)
    return pl.BlockSpec(shape, lambda *_: zeros)


def _dot(a, b):
    return jnp.dot(a, b, preferred_element_type=F32)


def _dot_t(a, b):
    return lax.dot_general(a, b, (((1,), (1,)), ((), ())), preferred_element_type=F32)


def _split_bf16(a):
    hi = a.astype(BF16)
    lo = (a - hi.astype(F32)).astype(BF16)
    return hi, lo


def _dot_3pass(x, w_hi, w_lo):
    x_hi, x_lo = _split_bf16(x)
    return _dot(x_hi, w_hi) + (_dot(x_lo, w_hi) + _dot(x_hi, w_lo))


def _layer_norm(x, g, b):
    mu = jnp.mean(x, axis=-1, keepdims=True)
    xc = x - mu
    var = jnp.mean(xc * xc, axis=-1, keepdims=True)
    return xc * lax.rsqrt(var + LN_EPS) * g + b


def _rms_norm(x, g):
    ms = jnp.mean(x * x, axis=-1, keepdims=True)
    return x * lax.rsqrt(ms + RMS_EPS) * g


def _proj_kernel(x_ref, w_ref, qg_ref, kg_ref, cos_ref, sin_ref, wq_ref, wqr_ref, *rest, absorbed):
    if absorbed:
        (wuk_ref, wqc_ref, wqcr_ref, cos8_ref, sin8_ref,
         u_ref, ckv_ref, kr_ref, gs_ref, gm_ref, ql_ref, qr_ref) = rest
    else:
        wk_ref, wv_ref, u_ref, ckv_ref, kr_ref, gs_ref, gm_ref, q_ref, k_ref, v_ref = rest
    xb = x_ref[...].astype(BF16)
    u_ref[...] = _dot(xb, w_ref[:, 0:512])
    cq = _rms_norm(_dot(xb, w_ref[:, 512:768]), qg_ref[...])
    ckv = _rms_norm(_dot(xb, w_ref[:, 768:1024]), kg_ref[...])
    ckv_ref[...] = ckv
    gs_ref[...] = jax.nn.sigmoid(_dot(xb, w_ref[:, 1024:2048])).astype(BF16)
    gm_ref[...] = jax.nn.sigmoid(_dot(xb, w_ref[:, 2048:3072])).astype(BF16)
    cos = cos_ref[...]
    sin = sin_ref[...]
    kr = _dot(xb, w_ref[:, 3072:3200]) * cos + _dot(xb, w_ref[:, 3200:3328]) * sin
    kr_ref[...] = kr

    cqb = cq.astype(BF16)
    qa = _dot(cqb, wq_ref[...])
    qb = _dot(cqb, wqr_ref[...])
    lane = lax.broadcasted_iota(jnp.int32, cos.shape, 1)
    cq_tab = (cos + jnp.where(lane < QK_NOPE, 1.0, 0.0)) * ATTN_SCALE
    sq_tab = sin * ATTN_SCALE
    ckvb = ckv.astype(BF16)
    if absorbed:
        for h in range(N_HEADS):
            sl = slice(h * HEAD_PAD, (h + 1) * HEAD_PAD)
            qh = (qa[:, sl] * cq_tab + qb[:, sl] * sq_tab).astype(BF16)
            ql_ref[:, h * KV_RANK:(h + 1) * KV_RANK] = _dot(qh, wuk_ref[h]).astype(BF16)
        qc = _dot(cqb, wqc_ref[...]) * cos8_ref[...] + _dot(cqb, wqcr_ref[...]) * sin8_ref[...]
        qr_ref[...] = (qc * ATTN_SCALE).astype(BF16)
    else:
        kn = _dot(ckvb, wk_ref[...])
        vv = _dot(ckvb, wv_ref[...])
        for h in range(N_HEADS):
            sl = slice(h * HEAD_PAD, (h + 1) * HEAD_PAD)
            q_ref[h] = (qa[:, sl] * cq_tab + qb[:, sl] * sq_tab).astype(BF16)
            k_ref[h] = (kn[:, sl] + kr).astype(BF16)
            v_ref[h] = vv[:, sl].astype(BF16)


def _proj(x, wts, tabs, *, tm, seq_tiles, absorbed):
    T = x.shape[0]
    nt = T // tm
    row = lambda i: (i, 0)
    if seq_tiles is None:
        tab_map = lambda i: (0, 0)
        u_shape, u_spec = (T, SSM_WIDTH), pl.BlockSpec((tm, SSM_WIDTH), row)
    else:
        tab_map = lambda i: (i % seq_tiles, 0)
        u_shape = (seq_tiles * tm, (nt // seq_tiles) * SSM_WIDTH)
        u_spec = pl.BlockSpec((tm, SSM_WIDTH), lambda i: (i % seq_tiles, i // seq_tiles))
    in_specs = [
        pl.BlockSpec((tm, D_MODEL), row),
        _const_spec(wts["w_in"].shape),
        _const_spec((1, Q_RANK)), _const_spec((1, KV_RANK)),
        pl.BlockSpec((tm, HEAD_PAD), tab_map), pl.BlockSpec((tm, HEAD_PAD), tab_map),
        _const_spec(wts["wq"].shape), _const_spec(wts["wq_rot"].shape),
    ]
    args = [x, wts["w_in"], wts["q_norm_g"], wts["kv_norm_g"], tabs["cos128"], tabs["sin128"],
            wts["wq"], wts["wq_rot"]]
    out_shape = [jax.ShapeDtypeStruct(u_shape, F32),
                 jax.ShapeDtypeStruct((T, KV_RANK), F32),
                 jax.ShapeDtypeStruct((T, HEAD_PAD), F32),
                 jax.ShapeDtypeStruct((T, D_MODEL), BF16),
                 jax.ShapeDtypeStruct((T, D_MODEL), BF16)]
    out_specs = [u_spec, pl.BlockSpec((tm, KV_RANK), row), pl.BlockSpec((tm, HEAD_PAD), row),
                 pl.BlockSpec((tm, D_MODEL), row), pl.BlockSpec((tm, D_MODEL), row)]
    if absorbed:
        in_specs += [_const_spec(wts["wuk_abs"].shape), _const_spec(wts["wq_rope"].shape),
                     _const_spec(wts["wq_rope_rot"].shape),
                     pl.BlockSpec((tm, N_HEADS * QK_ROPE), tab_map),
                     pl.BlockSpec((tm, N_HEADS * QK_ROPE), tab_map)]
        args += [wts["wuk_abs"], wts["wq_rope"], wts["wq_rope_rot"], tabs["cos_heads"], tabs["sin_heads"]]
        out_shape += [jax.ShapeDtypeStruct((T, N_HEADS * KV_RANK), BF16),
                      jax.ShapeDtypeStruct((T, N_HEADS * QK_ROPE), BF16)]
        out_specs += [pl.BlockSpec((tm, N_HEADS * KV_RANK), row), pl.BlockSpec((tm, N_HEADS * QK_ROPE), row)]
    else:
        in_specs += [_const_spec(wts["wk"].shape), _const_spec(wts["wv"].shape)]
        args += [wts["wk"], wts["wv"]]
        head = jax.ShapeDtypeStruct((N_HEADS, T, HEAD_PAD), BF16)
        head_spec = pl.BlockSpec((N_HEADS, tm, HEAD_PAD), lambda i: (0, i, 0))
        out_shape += [head, head, head]
        out_specs += [head_spec, head_spec, head_spec]
    return pl.pallas_call(
        functools.partial(_proj_kernel, absorbed=absorbed),
        grid=(nt,), in_specs=in_specs, out_specs=out_specs, out_shape=out_shape,
        compiler_params=_params(("parallel",)),
        name="proj_sample" if absorbed else "proj_prompt",
    )(*args)


def _s5_kernel(u_ref, s0_ref, lam_ref, wb_ref, wc_ref, dskip_ref, wglu_ref, bglu_ref,
               y_ref, sfin_ref, bu_ref, st_ref, *, lt, bb):
    ti = pl.program_id(1)
    rows = lt * bb
    n_tiles = STATE_COLS // MXU_DIM

    @pl.when(ti == 0)
    def _():
        st_ref[...] = s0_ref[...]

    u = u_ref[...].reshape(rows, SSM_WIDTH)
    ub = u.astype(BF16)
    for j in range(n_tiles):
        k0 = MXU_DIM * (j // (n_tiles // 2))
        bu_ref[:, j * MXU_DIM:(j + 1) * MXU_DIM] = _dot(ub[:, k0:k0 + MXU_DIM], wb_ref[j])

    jg = max(1, 64 // bb)
    for j0 in range(0, n_tiles, jg):
        lrs = [jnp.broadcast_to(lam_ref[0:1, (j0 + q) * LANES:(j0 + q + 1) * LANES], (bb, LANES)) for q in range(jg)]
        lis = [jnp.broadcast_to(lam_ref[1:2, (j0 + q) * LANES:(j0 + q + 1) * LANES], (bb, LANES)) for q in range(jg)]

        def body(t, carry, j0=j0, lrs=lrs, lis=lis):
            r0 = pl.multiple_of(t * bb, bb)
            new = []
            for q in range(jg):
                c0 = (j0 + q) * MXU_DIM
                sr, si = carry[2 * q], carry[2 * q + 1]
                nr = lrs[q] * sr - lis[q] * si + bu_ref[pl.ds(r0, bb), c0:c0 + LANES]
                ni = lrs[q] * si + lis[q] * sr + bu_ref[pl.ds(r0, bb), c0 + LANES:c0 + 2 * LANES]
                bu_ref[pl.ds(r0, bb), c0:c0 + LANES] = nr
                bu_ref[pl.ds(r0, bb), c0 + LANES:c0 + 2 * LANES] = ni
                new += [nr, ni]
            return tuple(new)

        init = []
        for q in range(jg):
            c0 = (j0 + q) * MXU_DIM
            init += [st_ref[:, c0:c0 + LANES], st_ref[:, c0 + LANES:c0 + 2 * LANES]]
        fin = lax.fori_loop(0, lt, body, tuple(init), unroll=min(lt, 8))
        for q in range(jg):
            c0 = (j0 + q) * MXU_DIM
            st_ref[:, c0:c0 + LANES] = fin[2 * q]
            st_ref[:, c0 + LANES:c0 + 2 * LANES] = fin[2 * q + 1]

    half = STATE_COLS // 2
    y = jnp.concatenate([_dot(bu_ref[:, 0:half].astype(BF16), wc_ref[0]),
                         _dot(bu_ref[:, half:STATE_COLS].astype(BF16), wc_ref[1])], axis=1)
    y = y + dskip_ref[...] * u
    z = jax.nn.gelu(y, approximate=True)
    gate = jax.nn.sigmoid(_dot(z.astype(BF16), wglu_ref[...]) + bglu_ref[...])
    y_ref[...] = (z * gate).astype(BF16).reshape(lt, bb, SSM_WIDTH)

    @pl.when(ti == pl.num_programs(1) - 1)
    def _():
        sfin_ref[...] = st_ref[...]


def _s5(u3, s0, wts, *, lt, bb, name):
    S, B, _ = u3.shape
    return pl.pallas_call(
        functools.partial(_s5_kernel, lt=lt, bb=bb),
        grid=(B // bb, S // lt),
        in_specs=[pl.BlockSpec((lt, bb, SSM_WIDTH), lambda b, t: (t, b, 0)),
                  pl.BlockSpec((bb, STATE_COLS), lambda b, t: (b, 0)),
                  _const_spec(wts["lam"].shape), _const_spec(wts["s5_wb"].shape),
                  _const_spec(wts["s5_wc"].shape), _const_spec((1, SSM_WIDTH)),
                  _const_spec(wts["w_glu"].shape), _const_spec((1, SSM_WIDTH))],
        out_specs=[pl.BlockSpec((lt, bb, SSM_WIDTH), lambda b, t: (t, b, 0)),
                   pl.BlockSpec((bb, STATE_COLS), lambda b, t: (b, 0))],
        out_shape=[jax.ShapeDtypeStruct((S, B, SSM_WIDTH), BF16),
                   jax.ShapeDtypeStruct((B, STATE_COLS), F32)],
        scratch_shapes=[pltpu.VMEM((lt * bb, STATE_COLS), F32), pltpu.VMEM((bb, STATE_COLS), F32)],
        compiler_params=_params(("parallel", "arbitrary")),
        name=name,
    )(u3, s0, wts["lam"], wts["s5_wb"], wts["s5_wc"], wts["d_skip"], wts["w_glu"], wts["b_glu"])


def _attn_prompt_kernel(q_ref, k_ref, v_ref, o_ref, *, tq):
    qi = pl.program_id(1)
    row = lax.broadcasted_iota(jnp.int32, (tq, tq), 0)
    col = lax.broadcasted_iota(jnp.int32, (tq, tq), 1)
    causal = col <= row
    outs = []
    for h in range(N_HEADS):
        q = q_ref[h]

        def step(kt, carry, masked, h=h, q=q):
            m, l, acc = carry
            k0 = pl.multiple_of(kt * tq, tq)
            s = _dot_t(q, k_ref[h, pl.ds(k0, tq), :])
            if masked:
                s = jnp.where(causal, s, NEG_INF)
            m_new = jnp.maximum(m, jnp.max(s, axis=1, keepdims=True))
            alpha = jnp.exp(m - m_new)
            p = jnp.exp(s - m_new)
            l = alpha * l + jnp.sum(p, axis=1, keepdims=True)
            acc = alpha * acc + _dot(p.astype(BF16), v_ref[h, pl.ds(k0, tq), :])
            return m_new, l, acc

        init = (jnp.full((tq, 1), NEG_INF, F32), jnp.zeros((tq, 1), F32), jnp.zeros((tq, HEAD_PAD), F32))
        carry = lax.fori_loop(0, qi, functools.partial(step, masked=False), init)
        _, l, acc = step(qi, carry, True)
        outs.append(acc / l)
    pairs = [outs[2 * p] + pltpu.roll(outs[2 * p + 1], V_DIM, axis=1) for p in range(N_HEADS // 2)]
    o_ref[...] = jnp.concatenate(pairs, axis=1).astype(BF16)


def _attn_prompt(q3, k3, v3, *, batch, seq, tq):
    nq = seq // tq
    T = batch * seq
    return pl.pallas_call(
        functools.partial(_attn_prompt_kernel, tq=tq),
        grid=(batch, nq),
        in_specs=[pl.BlockSpec((N_HEADS, tq, HEAD_PAD), lambda b, i: (0, b * nq + i, 0)),
                  pl.BlockSpec((N_HEADS, seq, HEAD_PAD), lambda b, i: (0, b, 0)),
                  pl.BlockSpec((N_HEADS, seq, HEAD_PAD), lambda b, i: (0, b, 0))],
        out_specs=pl.BlockSpec((tq, N_HEADS * V_DIM), lambda b, i: (b * nq + i, 0)),
        out_shape=jax.ShapeDtypeStruct((T, N_HEADS * V_DIM), BF16),
        compiler_params=_params(("parallel", "arbitrary")),
        name="attn_prompt",
    )(q3, k3, v3)


def _attn_sample_kernel(pt_ref, ql_ref, qr_ref, cn_ref, krn_ref, kv_hbm, kr_hbm, o_ref,
                        kvbuf, krbuf, kv_sem, kr_sem, m_ref, l_ref, acc_ref, *, cp, n_new):
    pc = pl.program_id(1)
    n_chunks = pl.num_programs(1)
    step = pl.program_id(0) * n_chunks + pc
    n_steps = pl.num_programs(0) * n_chunks
    slot = lax.rem(step, 2)

    def page_copies(b, c, sl):
        out = []
        for k in range(cp):
            page = pt_ref[b, c * cp + k]
            out.append(pltpu.make_async_copy(kv_hbm.at[page], kvbuf.at[sl, k], kv_sem.at[sl]))
            out.append(pltpu.make_async_copy(kr_hbm.at[page], krbuf.at[sl, k], kr_sem.at[sl]))
        return out

    @pl.when(step == 0)
    def _():
        for d in page_copies(0, 0, 0):
            d.start()

    @pl.when(step + 1 < n_steps)
    def _():
        nxt = step + 1
        for d in page_copies(lax.div(nxt, n_chunks), lax.rem(nxt, n_chunks), 1 - slot):
            d.start()

    for d in page_copies(pl.program_id(0), pc, slot):
        d.wait()

    @pl.when(pc == 0)
    def _():
        m_ref[...] = jnp.full(m_ref.shape, NEG_INF, F32)
        l_ref[...] = jnp.zeros(l_ref.shape, F32)
        acc_ref[...] = jnp.zeros(acc_ref.shape, F32)

    ql = ql_ref[...]
    qr = qr_ref[...]

    def update(state, s, vals):
        m, l, acc = state
        m_new = jnp.maximum(m, jnp.max(s, axis=1, keepdims=True))
        alpha = jnp.exp(m - m_new)
        p = jnp.exp(s - m_new)
        return (m_new, alpha * l + jnp.sum(p, axis=1, keepdims=True),
                alpha * acc + _dot(p.astype(BF16), vals))

    kv = kvbuf[slot].reshape(cp * PAGE_SIZE, KV_RANK).astype(BF16)
    kr_t = jnp.concatenate([krbuf[slot, k] for k in range(cp)], axis=1).astype(BF16)
    state = update((m_ref[...], l_ref[...], acc_ref[...]), _dot_t(ql, kv) + _dot(qr, kr_t), kv)
    m_ref[...], l_ref[...], acc_ref[...] = state

    @pl.when(pc == n_chunks - 1)
    def _():
        cn = cn_ref[...].astype(BF16)
        krn = krn_ref[...].astype(BF16)
        s = _dot_t(ql, cn) + _dot_t(qr, krn)
        rows = ql.shape[0]
        t_row = jnp.right_shift(lax.broadcasted_iota(jnp.int32, (rows, n_new), 0), 3)
        t_col = lax.broadcasted_iota(jnp.int32, (rows, n_new), 1)
        _, l, acc = update(state, jnp.where(t_col <= t_row, s, NEG_INF), cn)
        o_ref[...] = (acc / l).astype(BF16)


def _attn_sample(page_table, q_lat, q_rope, ckv_new, kr_new, cache_kv, cache_kr_t, *, batch, n_new, cp):
    rows = n_new * N_HEADS
    n_pages = page_table.shape[1]
    in_specs = [pl.BlockSpec((rows, KV_RANK), lambda b, c, pt: (b, 0)),
                pl.BlockSpec((rows, QK_ROPE), lambda b, c, pt: (b, 0)),
                pl.BlockSpec((n_new, KV_RANK), lambda b, c, pt: (b, 0)),
                pl.BlockSpec((n_new, QK_ROPE), lambda b, c, pt: (b, 0)),
                pl.BlockSpec(memory_space=pl.ANY), pl.BlockSpec(memory_space=pl.ANY)]
    grid_spec = pltpu.PrefetchScalarGridSpec(
        num_scalar_prefetch=1, grid=(batch, n_pages // cp), in_specs=in_specs,
        out_specs=pl.BlockSpec((rows, KV_RANK), lambda b, c, pt: (b, 0)),
        scratch_shapes=[pltpu.VMEM((2, cp, PAGE_SIZE, KV_RANK), F32), pltpu.VMEM((2, cp, QK_ROPE, PAGE_SIZE), F32),
                        pltpu.SemaphoreType.DMA((2,)), pltpu.SemaphoreType.DMA((2,)),
                        pltpu.VMEM((rows, 1), F32), pltpu.VMEM((rows, 1), F32), pltpu.VMEM((rows, KV_RANK), F32)])
    return pl.pallas_call(
        functools.partial(_attn_sample_kernel, cp=cp, n_new=n_new),
        grid_spec=grid_spec,
        out_shape=jax.ShapeDtypeStruct((batch * rows, KV_RANK), BF16),
        compiler_params=_params(("arbitrary", "arbitrary")),
        name="attn_sample",
    )(page_table, q_lat, q_rope, ckv_new, kr_new, cache_kv, cache_kr_t)


def _mm_kernel(x_ref, w_ref, o_ref):
    o_ref[...] = _dot(x_ref[...], w_ref[...]).astype(o_ref.dtype)


def _mm(x, w, *, tm, out_dtype, name):
    M, K = x.shape
    N = w.shape[1]
    return pl.pallas_call(
        _mm_kernel, grid=(M // tm,),
        in_specs=[pl.BlockSpec((tm, K), lambda i: (i, 0)), _const_spec((K, N))],
        out_specs=pl.BlockSpec((tm, N), lambda i: (i, 0)),
        out_shape=jax.ShapeDtypeStruct((M, N), out_dtype),
        compiler_params=_params(("parallel",)), name=name,
    )(x, w)


def _merge_kernel(ssm_ref, mla_ref, gs_ref, gm_ref, x_ref, wbs_ref, wbm_ref, wo_ref, g_ref, b_ref,
                  wr_hi_ref, wr_lo_ref, br_ref, x1_ref, route_ref, cnt_ref, run_ref, *, tm):
    i = pl.program_id(0)

    @pl.when(i == 0)
    def _():
        run_ref[...] = jnp.zeros(run_ref.shape, F32)

    a = _dot(ssm_ref[...], wbs_ref[...])
    m = _dot(mla_ref[...], wbm_ref[...])
    merged = gs_ref[...].astype(F32) * a + gm_ref[...].astype(F32) * m
    y = _dot(merged.astype(BF16), wo_ref[...])
    x1 = _layer_norm(ALPHA * x_ref[...] + y, g_ref[...], b_ref[...])
    x1_ref[...] = x1

    logits = _dot_3pass(x1, wr_hi_ref[...], wr_lo_ref[...]) + br_ref[...]
    lane = lax.broadcasted_iota(jnp.int32, (tm, LANES), 1).astype(F32)
    gl = jnp.where(lane < N_GROUPS, logits, NEG_INF)
    gmax = jnp.max(gl, axis=1, keepdims=True)
    g_sel = jnp.min(jnp.where(gl == gmax, lane, float(LANES)), axis=1, keepdims=True)
    onehot = jnp.where(lane == g_sel, 1.0, 0.0)
    r_i = lax.broadcasted_iota(jnp.int32, (tm, tm), 0)
    c_i = lax.broadcasted_iota(jnp.int32, (tm, tm), 1)
    before = jnp.where(c_i < r_i, 1.0, 0.0).astype(BF16)
    rank_all = _dot(before, onehot.astype(BF16)) + run_ref[0:1, :]
    rank = jnp.sum(onehot * rank_all, axis=1, keepdims=True)
    route_ref[...] = jnp.where(lane == 0.0, g_sel, jnp.where(lane == 1.0, rank, 0.0))
    run_ref[0:1, :] = run_ref[0:1, :] + jnp.sum(onehot, axis=0, keepdims=True)

    @pl.when(i == pl.num_programs(0) - 1)
    def _():
        cnt_ref[...] = run_ref[...]


def _merge(ssm, mla, gs, gm, x, wts, *, tm, seq_tiles):
    T = x.shape[0]
    row = lambda i: (i, 0)
    if seq_tiles is None:
        ssm_spec = pl.BlockSpec((tm, SSM_WIDTH), row)
    else:
        ssm_spec = pl.BlockSpec((tm, SSM_WIDTH), lambda i: (i % seq_tiles, i // seq_tiles))
    return pl.pallas_call(
        functools.partial(_merge_kernel, tm=tm),
        grid=(T // tm,),
        in_specs=[ssm_spec, pl.BlockSpec((tm, N_HEADS * V_DIM), row),
                  pl.BlockSpec((tm, D_MODEL), row), pl.BlockSpec((tm, D_MODEL), row),
                  pl.BlockSpec((tm, D_MODEL), row),
                  _const_spec(wts["w_br_ssm"].shape), _const_spec(wts["w_br_mla"].shape),
                  _const_spec(wts["w_out"].shape), _const_spec((1, D_MODEL)), _const_spec((1, D_MODEL)),
                  _const_spec((D_MODEL, LANES)), _const_spec((D_MODEL, LANES)), _const_spec((1, LANES))],
        out_specs=[pl.BlockSpec((tm, D_MODEL), row), pl.BlockSpec((tm, LANES), row),
                   _const_spec((8, LANES))],
        out_shape=[jax.ShapeDtypeStruct((T, D_MODEL), F32), jax.ShapeDtypeStruct((T, LANES), F32),
                   jax.ShapeDtypeStruct((8, LANES), F32)],
        scratch_shapes=[pltpu.VMEM((8, LANES), F32)],
        compiler_params=_params(("arbitrary",)),
        name="merge_ln1_route",
    )(ssm, mla, gs, gm, x, wts["w_br_ssm"], wts["w_br_mla"], wts["w_out"], wts["ln1_g"], wts["ln1_b"],
      wts["wrg_hi"], wts["wrg_lo"], wts["brg"])


def _dispatch_kernel(pad_ref, x_ref, dest_ref, xs_ref, zero_ref, sem, *, tm):
    i = pl.program_id(0)

    def row_copy(r, d):
        return pltpu.make_async_copy(x_ref.at[pl.ds(r, 1)], xs_ref.at[pl.ds(d, 1)], sem)

    def zero_copy(d):
        return pltpu.make_async_copy(zero_ref.at[pl.ds(0, 1)], xs_ref.at[pl.ds(d, 1)], sem)

    @pl.when(i == 0)
    def _():
        zero_ref[...] = jnp.zeros(zero_ref.shape, F32)
        for g in range(N_GROUPS + 1):
            start = pad_ref[g]
            n = pad_ref[N_GROUPS + 1 + g]

            def issue(r, c, start=start):
                zero_copy(start + r).start()
                return c

            def drain(r, c, start=start):
                zero_copy(start + r).wait()
                return c

            lax.fori_loop(0, n, issue, 0)
            lax.fori_loop(0, n, drain, 0)

    def issue(r8, c):
        for k in range(ROW_DMA_UNROLL):
            r = r8 * ROW_DMA_UNROLL + k
            row_copy(r, dest_ref[0, 0, r]).start(priority=k % 2)
        return c

    lax.fori_loop(0, tm // ROW_DMA_UNROLL, issue, 0)
    pltpu.make_async_copy(x_ref, xs_ref.at[pl.ds(0, tm)], sem).wait()


def _dispatch(x1, dest, pad_info, *, tm, n_rows):
    T = x1.shape[0]
    nt = T // tm
    grid_spec = pltpu.PrefetchScalarGridSpec(
        num_scalar_prefetch=1, grid=(nt,),
        in_specs=[pl.BlockSpec((tm, D_MODEL), lambda i, pad: (i, 0)),
                  pl.BlockSpec((1, 1, tm), lambda i, pad: (i, 0, 0), memory_space=pltpu.SMEM)],
        out_specs=pl.BlockSpec(memory_space=pl.ANY),
        scratch_shapes=[pltpu.VMEM((8, D_MODEL), F32), pltpu.SemaphoreType.DMA(())])
    return pl.pallas_call(
        functools.partial(_dispatch_kernel, tm=tm),
        grid_spec=grid_spec,
        out_shape=jax.ShapeDtypeStruct((n_rows, D_MODEL), F32),
        compiler_params=_params(("arbitrary",)),
        name="moe_dispatch",
    )(pad_info, x1, dest.reshape(nt, 1, tm))


def _moe_kernel(grp_ref, blk_ref, nv_ref, x_ref, wr_hi_ref, wr_lo_ref, br_ref, wg_ref, wu_ref, wd_ref, o_ref,
                *, tm):
    del blk_ref
    i = pl.program_id(0)

    @pl.when(i < nv_ref[0])
    def _():
        g = grp_ref[i]
        x = x_ref[...]
        logits = _dot_3pass(x, wr_hi_ref[0], wr_lo_ref[0]) + br_ref[0]
        lane = lax.broadcasted_iota(jnp.int32, (tm, LANES), 1).astype(F32)
        is_grp = (lane >= EXPERTS_PER_GROUP) & (lane < EXPERTS_PER_GROUP + N_GROUPS)
        gl = jnp.where(is_grp, logits, NEG_INF)
        gmax = jnp.max(gl, axis=1, keepdims=True)
        gexp = jnp.exp(gl - gmax)
        p_group = (jnp.sum(jnp.where(lane == (EXPERTS_PER_GROUP + g).astype(F32), gexp, 0.0), axis=1, keepdims=True)
                   / jnp.sum(gexp, axis=1, keepdims=True))
        el = jnp.where(lane < EXPERTS_PER_GROUP, logits, NEG_INF)
        v1 = jnp.max(el, axis=1, keepdims=True)
        i1 = jnp.min(jnp.where(el == v1, lane, float(LANES)), axis=1, keepdims=True)
        el2 = jnp.where(lane == i1, NEG_INF, el)
        v2 = jnp.max(el2, axis=1, keepdims=True)
        i2 = jnp.min(jnp.where(el2 == v2, lane, float(LANES)), axis=1, keepdims=True)
        e2 = jnp.exp(v2 - v1)
        w1 = p_group / (1.0 + e2)
        w2 = w1 * e2
        gates = jnp.where(lane == i1, w1, jnp.where(lane == i2, w2, 0.0))

        xb = x.astype(BF16)
        hs = []
        for e in range(EXPERTS_PER_GROUP):
            hg = _dot(xb, wg_ref[0, e])
            hu = _dot(xb, wu_ref[0, e])
            ge = jnp.sum(jnp.where(lane == float(e), gates, 0.0), axis=1, keepdims=True)
            hs.append((jax.nn.silu(hg) * hu * ge).astype(BF16))
        o_ref[...] = _dot(jnp.concatenate(hs, axis=1), wd_ref[0])

    @pl.when(i >= nv_ref[0])
    def _():
        o_ref[...] = jnp.zeros(o_ref.shape, F32)


def _moe(xs, tile_grp, tile_blk, n_valid, wts, *, tm):
    n_rows = xs.shape[0]
    nt = n_rows // tm
    grp3 = lambda i, grp, blk, nv: (grp[i], 0, 0)
    grp4 = lambda i, grp, blk, nv: (grp[i], 0, 0, 0)
    rows = lambda i, grp, blk, nv: (blk[i], 0)
    grid_spec = pltpu.PrefetchScalarGridSpec(
        num_scalar_prefetch=3, grid=(nt,),
        in_specs=[pl.BlockSpec((tm, D_MODEL), rows),
                  pl.BlockSpec((1, D_MODEL, LANES), grp3), pl.BlockSpec((1, D_MODEL, LANES), grp3),
                  pl.BlockSpec((1, 1, LANES), grp3),
                  pl.BlockSpec((1, EXPERTS_PER_GROUP, D_MODEL, D_EXPERT), grp4),
                  pl.BlockSpec((1, EXPERTS_PER_GROUP, D_MODEL, D_EXPERT), grp4),
                  pl.BlockSpec((1, EXPERTS_PER_GROUP * D_EXPERT, D_MODEL), grp3)],
        out_specs=pl.BlockSpec((tm, D_MODEL), lambda i, grp, blk, nv: (i, 0)))
    return pl.pallas_call(
        functools.partial(_moe_kernel, tm=tm),
        grid_spec=grid_spec,
        out_shape=jax.ShapeDtypeStruct((n_rows, D_MODEL), F32),
        compiler_params=_params(("arbitrary",)),
        name="moe_experts",
    )(tile_grp, tile_blk, n_valid, xs, wts["wroute_hi"], wts["wroute_lo"], wts["broute"],
      wts["w_e_gate"], wts["w_e_up"], wts["w_e_down"])


def _final_kernel(x1_ref, dest_ref, dest_next_ref, ys_ref, p_ref, g_ref, b_ref, wpg_ref, bpg_ref, wpp_ref, o_ref,
                  moe_ref, sem, *, tm):
    i = pl.program_id(0)
    slot = lax.rem(i, 2)

    def gather(d_ref, sl):
        def issue(r8, c):
            for k in range(ROW_DMA_UNROLL):
                r = r8 * ROW_DMA_UNROLL + k
                pltpu.make_async_copy(ys_ref.at[pl.ds(d_ref[0, 0, r], 1)], moe_ref.at[sl, pl.ds(r, 1)],
                                      sem.at[sl]).start(priority=k % 2)
            return c

        lax.fori_loop(0, tm // ROW_DMA_UNROLL, issue, 0)

    @pl.when(i == 0)
    def _():
        gather(dest_ref, 0)

    @pl.when(i + 1 < pl.num_programs(0))
    def _():
        gather(dest_next_ref, 1 - slot)

    pltpu.make_async_copy(ys_ref.at[pl.ds(0, tm)], moe_ref.at[slot], sem.at[slot]).wait()
    x2 = _layer_norm(ALPHA * x1_ref[...] + moe_ref[slot], g_ref[...], b_ref[...])
    gate = jax.nn.sigmoid(_dot(x2.astype(BF16), wpg_ref[...]) + bpg_ref[...])
    o_ref[...] = x2 + gate * _dot(p_ref[...].astype(BF16), wpp_ref[...])


def _final(x1, dest, ys, p, wts, *, tm):
    T = x1.shape[0]
    nt = T // tm
    row = lambda i: (i, 0)
    return pl.pallas_call(
        functools.partial(_final_kernel, tm=tm),
        grid=(nt,),
        in_specs=[pl.BlockSpec((tm, D_MODEL), row),
                  pl.BlockSpec((1, 1, tm), lambda i: (i, 0, 0), memory_space=pltpu.SMEM),
                  pl.BlockSpec((1, 1, tm), lambda i: (jnp.minimum(i + 1, nt - 1), 0, 0), memory_space=pltpu.SMEM),
                  pl.BlockSpec(memory_space=pl.ANY),
                  pl.BlockSpec((tm, PLE_DIM), row),
                  _const_spec((1, D_MODEL)), _const_spec((1, D_MODEL)),
                  _const_spec((D_MODEL, D_MODEL)), _const_spec((1, D_MODEL)), _const_spec((PLE_DIM, D_MODEL))],
        out_specs=pl.BlockSpec((tm, D_MODEL), row),
        out_shape=jax.ShapeDtypeStruct((T, D_MODEL), F32),
        scratch_shapes=[pltpu.VMEM((2, tm, D_MODEL), F32), pltpu.SemaphoreType.DMA((2,))],
        compiler_params=_params(("arbitrary",)),
        name="combine_ln2_ple",
    )(x1, dest.reshape(nt, 1, tm), dest.reshape(nt, 1, tm), ys, p, wts["ln2_g"], wts["ln2_b"], wts["w_ple_gate"], wts["b_ple_gate"],
      wts["w_ple_proj"])


def _prep_weights(w_in, a_re, a_im, log_dt, b_re, b_im, c_re, c_im, d_skip, w_glu, b_glu, w_br_ssm,
                  q_norm_g, w_uq, kv_norm_g, w_uk, w_uv, w_br_mla, w_out, ln1_g, ln1_b,
                  w_gr, b_gr, w_er, b_er, w_e_gate, w_e_up, w_e_down, ln2_g, ln2_b,
                  w_ple_gate, b_ple_gate, w_ple_proj):
    half = QK_ROPE // 2
    rot = lambda w: jnp.concatenate([-w[..., half:], w[..., :half]], axis=-1)
    w = {}

    u_w, cq_w, ckv_w, kr_w, gs_w, gm_w = jnp.split(
        w_in, (512, 768, 1024, 1024 + QK_ROPE, 1024 + QK_ROPE + D_MODEL), axis=1)
    slab = lambda c: jnp.pad(c, ((0, 0), (QK_NOPE, HEAD_PAD - QK_NOPE - QK_ROPE)))
    w["w_in"] = jnp.concatenate([u_w, cq_w, ckv_w, gs_w, gm_w, slab(kr_w), slab(rot(kr_w))], axis=1).astype(BF16)
    w["q_norm_g"] = q_norm_g.reshape(1, Q_RANK)
    w["kv_norm_g"] = kv_norm_g.reshape(1, KV_RANK)

    wq3 = w_uq.reshape(Q_RANK, N_HEADS, QK_NOPE + QK_ROPE)
    nope, rope = wq3[..., :QK_NOPE], wq3[..., QK_NOPE:]
    pad_tail = jnp.zeros((Q_RANK, N_HEADS, HEAD_PAD - QK_NOPE - QK_ROPE), F32)
    w["wq"] = jnp.concatenate([nope, rope, pad_tail], -1).reshape(Q_RANK, N_HEADS * HEAD_PAD).astype(BF16)
    w["wq_rot"] = jnp.concatenate([jnp.zeros_like(nope), rot(rope), pad_tail], -1).reshape(
        Q_RANK, N_HEADS * HEAD_PAD).astype(BF16)
    w["wq_rope"] = rope.reshape(Q_RANK, N_HEADS * QK_ROPE).astype(BF16)
    w["wq_rope_rot"] = rot(rope).reshape(Q_RANK, N_HEADS * QK_ROPE).astype(BF16)
    head_pad = lambda a: jnp.pad(a, ((0, 0), (0, 0), (0, HEAD_PAD - a.shape[-1]))).reshape(
        a.shape[0], N_HEADS * HEAD_PAD).astype(BF16)
    w["wk"] = head_pad(w_uk)
    w["wv"] = head_pad(w_uv)
    w["wuk_abs"] = jnp.pad(w_uk.transpose(1, 2, 0), ((0, 0), (0, HEAD_PAD - QK_NOPE), (0, 0))).astype(BF16)
    w["wuv_blockdiag"] = (jnp.eye(N_HEADS, dtype=F32)[:, None, :, None]
                          * w_uv.transpose(1, 0, 2)[:, :, None, :]).reshape(
                              N_HEADS * KV_RANK, N_HEADS * V_DIM).astype(BF16)

    dt = jnp.exp(log_dt)[:, None]
    mag = jnp.exp(dt * a_re)
    lr = mag * jnp.cos(dt * a_im)
    li = mag * jnp.sin(dt * a_im)
    den = a_re * a_re + a_im * a_im
    fr = ((lr - 1.0) * a_re + li * a_im) / den
    fi = (li * a_re - (lr - 1.0) * a_im) / den
    bbr = fr[..., None] * b_re - fi[..., None] * b_im
    bbi = fr[..., None] * b_im + fi[..., None] * b_re
    w["lam"] = jnp.stack([lr.reshape(-1), li.reshape(-1)])
    n_tiles = SSM_GROUPS // 2
    col_group = jnp.broadcast_to((2 * jnp.arange(n_tiles)[:, None, None, None]
                                  + jnp.arange(2)[None, None, :, None]), (n_tiles, 2, 2, SSM_STATE)).reshape(-1)
    ch_group = jnp.arange(SSM_WIDTH) // SSM_GROUP
    mask = ch_group[:, None] == col_group[None, :]
    bcols = jnp.stack([bbr, bbi]).reshape(2, n_tiles, 2, SSM_STATE, SSM_GROUP).transpose(
        1, 0, 2, 3, 4).reshape(STATE_COLS, SSM_GROUP)
    bmat = jnp.where(mask, jnp.tile(bcols.T, (SSM_GROUPS, 1)), 0.0)
    bmat4 = bmat.reshape(2, MXU_DIM, n_tiles, MXU_DIM)
    w["s5_wb"] = jnp.stack([bmat4[j // (n_tiles // 2), :, j, :] for j in range(n_tiles)]).astype(BF16)
    ccols = jnp.stack([c_re, -c_im]).reshape(2, n_tiles, 2, SSM_GROUP, SSM_STATE).transpose(
        1, 0, 2, 4, 3).reshape(STATE_COLS, SSM_GROUP)
    cmat = jnp.where(mask.T, jnp.tile(ccols, (1, SSM_GROUPS)), 0.0)
    hc = STATE_COLS // 2
    w["s5_wc"] = jnp.stack([cmat[:hc, :MXU_DIM], cmat[hc:, MXU_DIM:]]).astype(BF16)
    w["d_skip"] = d_skip.reshape(1, SSM_WIDTH)
    w["w_glu"] = w_glu.astype(BF16)
    w["b_glu"] = b_glu.reshape(1, SSM_WIDTH)

    w["w_br_ssm"] = w_br_ssm.astype(BF16)
    w["w_br_mla"] = w_br_mla.astype(BF16)
    w["w_out"] = w_out.astype(BF16)
    w["ln1_g"], w["ln1_b"] = ln1_g.reshape(1, D_MODEL), ln1_b.reshape(1, D_MODEL)
    w["ln2_g"], w["ln2_b"] = ln2_g.reshape(1, D_MODEL), ln2_b.reshape(1, D_MODEL)

    wrg = jnp.pad(w_gr, ((0, 0), (0, LANES - N_GROUPS)))
    w["wrg_hi"], w["wrg_lo"] = _split_bf16(wrg)
    w["brg"] = jnp.pad(b_gr, (0, LANES - N_GROUPS)).reshape(1, LANES)
    er = w_er.reshape(D_MODEL, N_GROUPS, EXPERTS_PER_GROUP).transpose(1, 0, 2)
    gr = jnp.broadcast_to(w_gr[None], (N_GROUPS, D_MODEL, N_GROUPS))
    wroute = jnp.pad(jnp.concatenate([er, gr], -1), ((0, 0), (0, 0), (0, LANES - EXPERTS_PER_GROUP - N_GROUPS)))
    w["wroute_hi"], w["wroute_lo"] = _split_bf16(wroute)
    broute = jnp.concatenate([b_er.reshape(N_GROUPS, EXPERTS_PER_GROUP),
                              jnp.broadcast_to(b_gr[None], (N_GROUPS, N_GROUPS))], -1)
    w["broute"] = jnp.pad(broute, ((0, 0), (0, LANES - EXPERTS_PER_GROUP - N_GROUPS))).reshape(N_GROUPS, 1, LANES)
    w["w_e_gate"] = w_e_gate.astype(BF16).reshape(N_GROUPS, EXPERTS_PER_GROUP, D_MODEL, D_EXPERT)
    w["w_e_up"] = w_e_up.astype(BF16).reshape(N_GROUPS, EXPERTS_PER_GROUP, D_MODEL, D_EXPERT)
    w["w_e_down"] = w_e_down.astype(BF16).reshape(N_GROUPS, EXPERTS_PER_GROUP * D_EXPERT, D_MODEL)
    w["w_ple_gate"] = w_ple_gate.astype(BF16)
    w["b_ple_gate"] = b_ple_gate.reshape(1, D_MODEL)
    w["w_ple_proj"] = w_ple_proj.astype(BF16)
    return w


def _rope_tables(pos, reps):
    half = QK_ROPE // 2
    inv = jnp.exp(-math.log(ROPE_THETA) * jnp.arange(half, dtype=F32) / half)
    ang = pos.astype(F32)[:, None] * inv[None, :]
    cos2 = jnp.tile(jnp.cos(ang), (1, 2))
    sin2 = jnp.tile(jnp.sin(ang), (1, 2))
    slab = lambda t: jnp.tile(jnp.pad(t, ((0, 0), (QK_NOPE, HEAD_PAD - QK_NOPE - QK_ROPE))), (reps, 1))
    heads = lambda t: jnp.tile(t, (reps, N_HEADS))
    return {"cos128": slab(cos2), "sin128": slab(sin2), "cos_heads": heads(cos2), "sin_heads": heads(sin2)}


def _pack_state(s_re, s_im):
    B = s_re.shape[0]
    st = jnp.stack([s_re.reshape(B, SSM_GROUPS // 2, 2, SSM_STATE), s_im.reshape(B, SSM_GROUPS // 2, 2, SSM_STATE)],
                   axis=2)
    return st.reshape(B, STATE_COLS)


def _unpack_state(s):
    B = s.shape[0]
    st = s.reshape(B, SSM_GROUPS // 2, 2, 2, SSM_STATE)
    return (st[:, :, 0].reshape(B, SSM_GROUPS, SSM_STATE), st[:, :, 1].reshape(B, SSM_GROUPS, SSM_STATE))


def _route_plan(route, counts, *, tm, n_tok):
    g_sel = route[:, 0].astype(jnp.int32)
    rank = route[:, 1].astype(jnp.int32)
    cnt = counts[0, :N_GROUPS].astype(jnp.int32)
    padded = ((cnt + tm - 1) // tm) * tm
    ends = jnp.cumsum(padded)
    offs = ends - padded
    dest = offs[g_sel] + rank
    n_tiles = n_tok // tm + N_GROUPS
    n_valid = ends[-1] // tm
    tile_start = jnp.arange(n_tiles, dtype=jnp.int32) * tm
    tile_grp = jnp.minimum(jnp.sum(tile_start[:, None] >= ends[None, :], axis=1), N_GROUPS - 1).astype(jnp.int32)
    last = jnp.maximum(n_valid - 1, 0)
    tile_blk = jnp.minimum(jnp.arange(n_tiles, dtype=jnp.int32), last)
    tile_grp = jnp.where(jnp.arange(n_tiles) < n_valid, tile_grp, tile_grp[last])
    n_rows = n_tiles * tm
    pad_info = jnp.concatenate([offs + cnt, ends[-1:], padded - cnt, n_rows - ends[-1:]]).astype(jnp.int32)
    return dest, tile_grp, tile_blk, n_valid.reshape(1).astype(jnp.int32), pad_info, n_rows


def _token_tail(ssm, mla, gs, gm, x, p, wts, *, tm, seq_tiles, tm_moe):
    T = x.shape[0]
    x1, route, counts = _merge(ssm, mla, gs, gm, x, wts, tm=tm, seq_tiles=seq_tiles)
    dest, tile_grp, tile_blk, n_valid, pad_info, n_rows = _route_plan(route, counts, tm=tm_moe, n_tok=T)
    xs = _dispatch(x1, dest, pad_info, tm=tm, n_rows=n_rows)
    ys = _moe(xs, tile_grp, tile_blk, n_valid, wts, tm=tm_moe)
    return _final(x1, dest, ys, p, wts, tm=tm)


def kernel(x_prompt, x_sample, p_prompt, p_sample, cache_kv, cache_k_rope, state_ssm_re, state_ssm_im, page_table, w_in, a_re, a_im, log_dt, b_re, b_im, c_re, c_im, d_skip, w_glu, b_glu, w_br_ssm, q_norm_g, w_uq, kv_norm_g, w_uk, w_uv, w_br_mla, w_out, ln1_g, ln1_b, w_gr, b_gr, w_er, b_er, w_e_gate, w_e_up, w_e_down, ln2_g, ln2_b, w_ple_gate, b_ple_gate, w_ple_proj):
    B, S, _ = x_prompt.shape
    Bs, Ss, _ = x_sample.shape
    Tp, Ts = B * S, Bs * Ss
    wts = _prep_weights(w_in[0], a_re[0], a_im[0], log_dt[0], b_re[0], b_im[0], c_re[0], c_im[0], d_skip[0],
                        w_glu[0], b_glu[0], w_br_ssm[0], q_norm_g[0], w_uq[0], kv_norm_g[0], w_uk[0], w_uv[0],
                        w_br_mla[0], w_out[0], ln1_g[0], ln1_b[0], w_gr[0], b_gr[0], w_er[0], b_er[0],
                        w_e_gate[0], w_e_up[0], w_e_down[0], ln2_g[0], ln2_b[0], w_ple_gate[0], b_ple_gate[0],
                        w_ple_proj[0])
    tm = 256
    tm_moe = 256

    tabs_p = _rope_tables(jnp.arange(S, dtype=jnp.int32), 1)
    seq_tiles = S // tm
    u_t, ckv_p, kr_p, gs_p, gm_p, q3, k3, v3 = _proj(
        x_prompt.reshape(Tp, D_MODEL), wts, tabs_p, tm=tm, seq_tiles=seq_tiles, absorbed=False)
    zero_state = jnp.zeros((B, STATE_COLS), F32)
    ssm_p, sfin_p = _s5(u_t.reshape(S, B, SSM_WIDTH), zero_state, wts, lt=32, bb=B, name="s5_prompt")
    mla_p = _attn_prompt(q3, k3, v3, batch=B, seq=S, tq=512)
    y_p = _token_tail(ssm_p.reshape(S, B * SSM_WIDTH), mla_p, gs_p, gm_p, x_prompt.reshape(Tp, D_MODEL),
                      p_prompt.reshape(Tp, PLE_DIM), wts, tm=tm, seq_tiles=seq_tiles, tm_moe=tm_moe)

    tabs_s = _rope_tables(PAST_LEN + jnp.arange(Ss, dtype=jnp.int32), tm // Ss)
    u_s, ckv_s, kr_s, gs_s, gm_s, q_lat, q_rope = _proj(
        x_sample.reshape(Ts, D_MODEL), wts, tabs_s, tm=tm, seq_tiles=None, absorbed=True)
    s0 = _pack_state(state_ssm_re[0], state_ssm_im[0])
    u_s_t = u_s.reshape(Bs, Ss, SSM_WIDTH).transpose(1, 0, 2)
    ssm_s_t, sfin_s = _s5(u_s_t, s0, wts, lt=Ss, bb=64, name="s5_sample")
    ssm_s = ssm_s_t.transpose(1, 0, 2).reshape(Ts, SSM_WIDTH)
    kr_s32 = kr_s[:, QK_NOPE:QK_NOPE + QK_ROPE]
    n_pool = cache_kv.shape[1]
    o_lat = _attn_sample(page_table, q_lat.reshape(Ts * N_HEADS, KV_RANK), q_rope.reshape(Ts * N_HEADS, QK_ROPE),
                         ckv_s, kr_s32, cache_kv.reshape(n_pool, PAGE_SIZE, KV_RANK),
                         cache_k_rope.reshape(n_pool, PAGE_SIZE, QK_ROPE).transpose(0, 2, 1),
                         batch=Bs, n_new=Ss, cp=32)
    mla_s = _mm(o_lat.reshape(Ts, N_HEADS * KV_RANK), wts["wuv_blockdiag"], tm=tm, out_dtype=BF16, name="uv_sample")
    y_s = _token_tail(ssm_s, mla_s, gs_s, gm_s, x_sample.reshape(Ts, D_MODEL), p_sample.reshape(Ts, PLE_DIM), wts,
                      tm=tm, seq_tiles=None, tm_moe=tm_moe)

    sre_p, sim_p = _unpack_state(sfin_p)
    sre_s, sim_s = _unpack_state(sfin_s)
    kr_p32 = kr_p[:, QK_NOPE:QK_NOPE + QK_ROPE]
    return (y_p.reshape(B, S, D_MODEL), y_s.reshape(Bs, Ss, D_MODEL),
            ckv_p.reshape(1, B, S, KV_RANK), kr_p32.reshape(1, B, S, QK_ROPE),
            sre_p[None], sim_p[None],
            ckv_s.reshape(1, Bs, Ss, KV_RANK), kr_s32.reshape(1, Bs, Ss, QK_ROPE),
            sre_s[None], sim_s[None])
```

```python
import functools
import math

import jax
import jax.numpy as jnp
from jax import lax
from jax.experimental import pallas as pl
from jax.experimental.pallas import tpu as pltpu

F32 = jnp.float32
BF16 = jnp.bfloat16

D_MODEL = 1024
DEPTH = 1
PAST_LEN = 16384
PAGE_SIZE = 128
SSM_WIDTH = 512
SSM_GROUP = 16
SSM_GROUPS = 32
SSM_STATE = 64
STATE_COLS = 2 * SSM_GROUPS * SSM_STATE
N_HEADS = 8
QK_NOPE = 64
QK_ROPE = 32
V_DIM = 64
Q_RANK = 256
KV_RANK = 256
HEAD_PAD = 128
ROPE_THETA = 10000.0
ATTN_SCALE = (QK_NOPE + QK_ROPE) ** -0.5
Q_SCALE = ATTN_SCALE * math.log2(math.e)
N_GROUPS = 4
EXPERTS_PER_GROUP = 8
D_EXPERT = 256
PLE_DIM = 256
LN_EPS = 1e-5
RMS_EPS = 1e-6
ALPHA = (2 * DEPTH) ** 0.25

LANES = 128
MXU_DIM = 256
VMEM_LIMIT = 56 * 1024 * 1024

ROW_DMA_UNROLL = 256

NEG_INF = float("-inf")


def _params(sem, vmem=VMEM_LIMIT):
    return pltpu.CompilerParams(dimension_semantics=sem, vmem_limit_bytes=vmem)


def _const_spec(shape):
    zeros = (0,) * len(shape)
    return pl.BlockSpec(shape, lambda *_: zeros)


def _dot(a, b):
    return jnp.dot(a, b, preferred_element_type=F32)


def _dot_t(a, b):
    return lax.dot_general(a, b, (((1,), (1,)), ((), ())), preferred_element_type=F32)


def _layer_norm(x, g, b):
    mu = jnp.mean(x, axis=-1, keepdims=True)
    xc = x - mu
    var = jnp.mean(xc * xc, axis=-1, keepdims=True)
    return xc * lax.rsqrt(var + LN_EPS) * g + b


def _rms_norm(x, g):
    ms = jnp.mean(x * x, axis=-1, keepdims=True)
    return x * lax.rsqrt(ms + RMS_EPS) * g


def _proj_kernel(x_ref, w_ref, qg_ref, kg_ref, cos_ref, sin_ref, wq_ref, wqr_ref, *rest, absorbed):
    if absorbed:
        (wuk_ref, wqc_ref, wqcr_ref, cos8_ref, sin8_ref,
         u_ref, ckv_ref, kr_ref, gs_ref, gm_ref, ql_ref, qr_ref) = rest
    else:
        wk_ref, wv_ref, u_ref, ckv_ref, kr_ref, gs_ref, gm_ref, q_ref, k_ref, v_ref = rest
    xb = x_ref[...].astype(BF16)
    u_ref[...] = _dot(xb, w_ref[:, 0:512])
    cq = _rms_norm(_dot(xb, w_ref[:, 512:768]), qg_ref[...])
    ckv = _rms_norm(_dot(xb, w_ref[:, 768:1024]), kg_ref[...])
    ckv_ref[...] = ckv
    gs_ref[...] = jax.nn.sigmoid(_dot(xb, w_ref[:, 1024:2048])).astype(BF16)
    gm_ref[...] = jax.nn.sigmoid(_dot(xb, w_ref[:, 2048:3072])).astype(BF16)
    cos = cos_ref[...]
    sin = sin_ref[...]
    kr = _dot(xb, w_ref[:, 3072:3200]) * cos + _dot(xb, w_ref[:, 3200:3328]) * sin
    kr_ref[...] = kr

    cqb = cq.astype(BF16)
    qa = _dot(cqb, wq_ref[...])
    qb = _dot(cqb, wqr_ref[...])
    lane = lax.broadcasted_iota(jnp.int32, cos.shape, 1)
    cq_tab = (cos + jnp.where(lane < QK_NOPE, 1.0, 0.0)) * Q_SCALE
    sq_tab = sin * Q_SCALE
    ckvb = ckv.astype(BF16)
    if absorbed:
        for h in range(N_HEADS):
            sl = slice(h * HEAD_PAD, (h + 1) * HEAD_PAD)
            qh = (qa[:, sl] * cq_tab + qb[:, sl] * sq_tab).astype(BF16)
            ql_ref[:, h * KV_RANK:(h + 1) * KV_RANK] = _dot(qh, wuk_ref[h]).astype(BF16)
        qc = _dot(cqb, wqc_ref[...]) * cos8_ref[...] + _dot(cqb, wqcr_ref[...]) * sin8_ref[...]
        qr_ref[...] = (qc * Q_SCALE).astype(BF16)
    else:
        kn = _dot(ckvb, wk_ref[...])
        vv = _dot(ckvb, wv_ref[...])
        for h in range(N_HEADS):
            sl = slice(h * HEAD_PAD, (h + 1) * HEAD_PAD)
            q_ref[h] = (qa[:, sl] * cq_tab + qb[:, sl] * sq_tab).astype(BF16)
            k_ref[h] = (kn[:, sl] + kr).astype(BF16)
            v_ref[h] = jnp.where(lane == V_DIM, 1.0, vv[:, sl]).astype(BF16)


def _proj(x, wts, tabs, *, tm, seq_tiles, absorbed):
    T = x.shape[0]
    nt = T // tm
    row = lambda i: (i, 0)
    if seq_tiles is None:
        tab_map = lambda i: (0, 0)
        u_shape, u_spec = (T, SSM_WIDTH), pl.BlockSpec((tm, SSM_WIDTH), row)
    else:
        tab_map = lambda i: (i % seq_tiles, 0)
        u_shape = (seq_tiles * tm, (nt // seq_tiles) * SSM_WIDTH)
        u_spec = pl.BlockSpec((tm, SSM_WIDTH), lambda i: (i % seq_tiles, i // seq_tiles))
    in_specs = [
        pl.BlockSpec((tm, D_MODEL), row),
        _const_spec(wts["w_in"].shape),
        _const_spec((1, Q_RANK)), _const_spec((1, KV_RANK)),
        pl.BlockSpec((tm, HEAD_PAD), tab_map), pl.BlockSpec((tm, HEAD_PAD), tab_map),
        _const_spec(wts["wq"].shape), _const_spec(wts["wq_rot"].shape),
    ]
    args = [x, wts["w_in"], wts["q_norm_g"], wts["kv_norm_g"], tabs["cos128"], tabs["sin128"],
            wts["wq"], wts["wq_rot"]]
    out_shape = [jax.ShapeDtypeStruct(u_shape, F32),
                 jax.ShapeDtypeStruct((T, KV_RANK), F32),
                 jax.ShapeDtypeStruct((T, HEAD_PAD), F32),
                 jax.ShapeDtypeStruct((T, D_MODEL), BF16),
                 jax.ShapeDtypeStruct((T, D_MODEL), BF16)]
    out_specs = [u_spec, pl.BlockSpec((tm, KV_RANK), row), pl.BlockSpec((tm, HEAD_PAD), row),
                 pl.BlockSpec((tm, D_MODEL), row), pl.BlockSpec((tm, D_MODEL), row)]
    if absorbed:
        in_specs += [_const_spec(wts["wuk_abs"].shape), _const_spec(wts["wq_rope"].shape),
                     _const_spec(wts["wq_rope_rot"].shape),
                     pl.BlockSpec((tm, N_HEADS * QK_ROPE), tab_map),
                     pl.BlockSpec((tm, N_HEADS * QK_ROPE), tab_map)]
        args += [wts["wuk_abs"], wts["wq_rope"], wts["wq_rope_rot"], tabs["cos_heads"], tabs["sin_heads"]]
        out_shape += [jax.ShapeDtypeStruct((T, N_HEADS * KV_RANK), BF16),
                      jax.ShapeDtypeStruct((T, N_HEADS * QK_ROPE), BF16)]
        out_specs += [pl.BlockSpec((tm, N_HEADS * KV_RANK), row), pl.BlockSpec((tm, N_HEADS * QK_ROPE), row)]
    else:
        in_specs += [_const_spec(wts["wk"].shape), _const_spec(wts["wv"].shape)]
        args += [wts["wk"], wts["wv"]]
        head = jax.ShapeDtypeStruct((N_HEADS, T, HEAD_PAD), BF16)
        head_spec = pl.BlockSpec((N_HEADS, tm, HEAD_PAD), lambda i: (0, i, 0))
        out_shape += [head, head, head]
        out_specs += [head_spec, head_spec, head_spec]
    return pl.pallas_call(
        functools.partial(_proj_kernel, absorbed=absorbed),
        grid=(nt,), in_specs=in_specs, out_specs=out_specs, out_shape=out_shape,
        compiler_params=_params(("parallel",)),
        name="proj_sample" if absorbed else "proj_prompt",
    )(*args)


def _s5_kernel(u_ref, s0_ref, lam_ref, wb_ref, wc_ref, dskip_ref, wglu_ref, bglu_ref,
               y_ref, sfin_ref, bu_ref, st_ref, *, lt, bb):
    ti = pl.program_id(1)
    rows = lt * bb
    n_tiles = STATE_COLS // MXU_DIM

    @pl.when(ti == 0)
    def _():
        st_ref[...] = s0_ref[...]

    u = u_ref[...].reshape(rows, SSM_WIDTH)
    ub = u.astype(BF16)
    for j in range(n_tiles):
        k0 = MXU_DIM * (j // (n_tiles // 2))
        bu_ref[:, j * MXU_DIM:(j + 1) * MXU_DIM] = _dot(ub[:, k0:k0 + MXU_DIM], wb_ref[j])

    jg = max(1, 64 // bb)
    for j0 in range(0, n_tiles, jg):
        lrs = [jnp.broadcast_to(lam_ref[0:1, (j0 + q) * LANES:(j0 + q + 1) * LANES], (bb, LANES)) for q in range(jg)]
        lis = [jnp.broadcast_to(lam_ref[1:2, (j0 + q) * LANES:(j0 + q + 1) * LANES], (bb, LANES)) for q in range(jg)]

        def body(t, carry, j0=j0, lrs=lrs, lis=lis):
            r0 = pl.multiple_of(t * bb, bb)
            new = []
            for q in range(jg):
                c0 = (j0 + q) * MXU_DIM
                sr, si = carry[2 * q], carry[2 * q + 1]
                nr = lrs[q] * sr - lis[q] * si + bu_ref[pl.ds(r0, bb), c0:c0 + LANES]
                ni = lrs[q] * si + lis[q] * sr + bu_ref[pl.ds(r0, bb), c0 + LANES:c0 + 2 * LANES]
                bu_ref[pl.ds(r0, bb), c0:c0 + LANES] = nr
                bu_ref[pl.ds(r0, bb), c0 + LANES:c0 + 2 * LANES] = ni
                new += [nr, ni]
            return tuple(new)

        init = []
        for q in range(jg):
            c0 = (j0 + q) * MXU_DIM
            init += [st_ref[:, c0:c0 + LANES], st_ref[:, c0 + LANES:c0 + 2 * LANES]]
        fin = lax.fori_loop(0, lt, body, tuple(init), unroll=min(lt, 8))
        for q in range(jg):
            c0 = (j0 + q) * MXU_DIM
            st_ref[:, c0:c0 + LANES] = fin[2 * q]
            st_ref[:, c0 + LANES:c0 + 2 * LANES] = fin[2 * q + 1]

    half = STATE_COLS // 2
    y = jnp.concatenate([_dot(bu_ref[:, 0:half].astype(BF16), wc_ref[0]),
                         _dot(bu_ref[:, half:STATE_COLS].astype(BF16), wc_ref[1])], axis=1)
    y = y + dskip_ref[...] * u
    z = jax.nn.gelu(y, approximate=True)
    gate = jax.nn.sigmoid(_dot(z.astype(BF16), wglu_ref[...]) + bglu_ref[...])
    y_ref[...] = (z * gate).astype(BF16).reshape(lt, bb, SSM_WIDTH)

    @pl.when(ti == pl.num_programs(1) - 1)
    def _():
        sfin_ref[...] = st_ref[...]


def _s5(u3, s0, wts, *, lt, bb, name):
    S, B, _ = u3.shape
    return pl.pallas_call(
        functools.partial(_s5_kernel, lt=lt, bb=bb),
        grid=(B // bb, S // lt),
        in_specs=[pl.BlockSpec((lt, bb, SSM_WIDTH), lambda b, t: (t, b, 0)),
                  pl.BlockSpec((bb, STATE_COLS), lambda b, t: (b, 0)),
                  _const_spec(wts["lam"].shape), _const_spec(wts["s5_wb"].shape),
                  _const_spec(wts["s5_wc"].shape), _const_spec((1, SSM_WIDTH)),
                  _const_spec(wts["w_glu"].shape), _const_spec((1, SSM_WIDTH))],
        out_specs=[pl.BlockSpec((lt, bb, SSM_WIDTH), lambda b, t: (t, b, 0)),
                   pl.BlockSpec((bb, STATE_COLS), lambda b, t: (b, 0))],
        out_shape=[jax.ShapeDtypeStruct((S, B, SSM_WIDTH), BF16),
                   jax.ShapeDtypeStruct((B, STATE_COLS), F32)],
        scratch_shapes=[pltpu.VMEM((lt * bb, STATE_COLS), F32), pltpu.VMEM((bb, STATE_COLS), F32)],
        compiler_params=_params(("parallel", "arbitrary")),
        name=name,
    )(u3, s0, wts["lam"], wts["s5_wb"], wts["s5_wc"], wts["d_skip"], wts["w_glu"], wts["b_glu"])


def _attn_prompt_kernel(q_ref, k_ref, v_ref, o_ref, *, tq, tk):
    qi = pl.program_id(1)
    n_diag = tq // tk
    row = lax.broadcasted_iota(jnp.int32, (tq, tk), 0)
    col = lax.broadcasted_iota(jnp.int32, (tq, tk), 1)
    n_full = qi * n_diag
    lane = lax.broadcasted_iota(jnp.int32, (tq, HEAD_PAD), 1)
    pairs = []
    for h0 in range(0, N_HEADS, 2):
        heads = (h0, h0 + 1)
        qs = [q_ref[h] for h in heads]

        def step(kt, carry, diag=None, heads=heads, qs=qs):
            k0 = pl.multiple_of(kt * tk, tk)
            new = []
            for j, h in enumerate(heads):
                m, acc = carry[2 * j], carry[2 * j + 1]
                s = _dot_t(qs[j], k_ref[h, pl.ds(k0, tk), :])
                if diag is not None:
                    s = jnp.where(col + diag * tk <= row, s, NEG_INF)
                m_new = jnp.maximum(m, jnp.max(s, axis=1, keepdims=True))
                p = jnp.exp2(s - m_new).astype(BF16)
                acc = jnp.exp2(m - m_new) * acc + _dot(p, v_ref[h, pl.ds(k0, tk), :])
                new += [m_new, acc]
            return tuple(new)

        carry = (jnp.full((tq, 1), NEG_INF, F32), jnp.zeros((tq, HEAD_PAD), F32)) * 2
        carry = lax.fori_loop(0, n_full, step, carry)
        for d in range(n_diag):
            carry = step(n_full + d, carry, diag=d)
        o0, o1 = [jnp.where(lane < V_DIM, acc / acc[:, V_DIM:V_DIM + 1], 0.0) for acc in (carry[1], carry[3])]
        pairs.append(o0 + pltpu.roll(o1, V_DIM, axis=1))
    o_ref[...] = jnp.concatenate(pairs, axis=1).astype(BF16)


def _attn_prompt(q3, k3, v3, *, batch, seq, tq, tk):
    nq = seq // tq
    T = batch * seq
    return pl.pallas_call(
        functools.partial(_attn_prompt_kernel, tq=tq, tk=tk),
        grid=(batch, nq),
        in_specs=[pl.BlockSpec((N_HEADS, tq, HEAD_PAD), lambda b, i: (0, b * nq + i, 0)),
                  pl.BlockSpec((N_HEADS, seq, HEAD_PAD), lambda b, i: (0, b, 0)),
                  pl.BlockSpec((N_HEADS, seq, HEAD_PAD), lambda b, i: (0, b, 0))],
        out_specs=pl.BlockSpec((tq, N_HEADS * V_DIM), lambda b, i: (b * nq + i, 0)),
        out_shape=jax.ShapeDtypeStruct((T, N_HEADS * V_DIM), BF16),
        compiler_params=_params(("parallel", "arbitrary")),
        name="attn_prompt",
    )(q3, k3, v3)


def _attn_sample_kernel(pt_ref, ql_ref, qr_ref, cn_ref, krn_ref, kv_hbm, kr_hbm, o_ref,
                        kvbuf, krbuf, kv_sem, kr_sem, kvb_ref, s_ref, m_ref, l_ref, acc_ref,
                        *, cp, n_chunks, n_new):
    assert n_chunks % 2 == 0
    pc = pl.program_id(1)
    step = pl.program_id(0) * n_chunks + pc
    n_steps = pl.num_programs(0) * n_chunks
    slot = lax.rem(step, 2)

    def page_copies(b, c, sl):
        out = []
        for k in range(cp):
            page = pt_ref[b, c * cp + k]
            out.append(pltpu.make_async_copy(kv_hbm.at[page], kvbuf.at[sl, k], kv_sem.at[sl]))
            out.append(pltpu.make_async_copy(kr_hbm.at[page], krbuf.at[sl, k], kr_sem.at[sl]))
        return out

    @pl.when(step == 0)
    def _():
        for d in page_copies(0, 0, 0):
            d.start()

    @pl.when(step + 1 < n_steps)
    def _():
        nxt = step + 1
        for d in page_copies(lax.div(nxt, n_chunks), lax.rem(nxt, n_chunks), 1 - slot):
            d.start()

    for d in page_copies(pl.program_id(0), pc, slot):
        d.wait()

    def reset_state():
        m_ref[...] = jnp.full(m_ref.shape, NEG_INF, F32)
        l_ref[...] = jnp.zeros(l_ref.shape, F32)
        acc_ref[...] = jnp.zeros(acc_ref.shape, F32)

    @pl.when(step == 0)
    def _():
        reset_state()
        kvb_ref[1] = jnp.zeros(kvb_ref.shape[1:], BF16)
        s_ref[1] = jnp.zeros(s_ref.shape[1:], F32)

    ql = ql_ref[...]
    qr = qr_ref[...]

    def update(state, s, vals):
        m, l, acc = state
        m_new = jnp.maximum(m, jnp.max(s, axis=1, keepdims=True))
        alpha = jnp.exp2(m - m_new)
        p = jnp.exp2(s - m_new)
        return (m_new, alpha * l + jnp.sum(p, axis=1, keepdims=True),
                alpha * acc + _dot(p.astype(BF16), vals))

    def fold(sl):
        state = update((m_ref[...], l_ref[...], acc_ref[...]), s_ref[sl], kvb_ref[sl])
        m_ref[...], l_ref[...], acc_ref[...] = state
        return state

    def body(sl):
        kv = kvbuf[sl].reshape(cp * PAGE_SIZE, KV_RANK).astype(BF16)
        kr_t = jnp.concatenate([krbuf[sl, k] for k in range(cp)], axis=1).astype(BF16)
        kvb_ref[sl] = kv
        s_ref[sl] = _dot_t(ql, kv) + _dot(qr, kr_t)
        fold(1 - sl)

    for sl in range(2):
        pl.when(slot == sl)(functools.partial(body, sl))

    @pl.when(pc == 0)
    def _():
        reset_state()

    @pl.when(pc == n_chunks - 1)
    def _():
        state = fold((n_chunks - 1) % 2)
        cn = cn_ref[...].astype(BF16)
        krn = krn_ref[...].astype(BF16)
        s = _dot_t(ql, cn) + _dot_t(qr, krn)
        rows = ql.shape[0]
        t_row = jnp.right_shift(lax.broadcasted_iota(jnp.int32, (rows, n_new), 0), 3)
        t_col = lax.broadcasted_iota(jnp.int32, (rows, n_new), 1)
        _, l, acc = update(state, jnp.where(t_col <= t_row, s, NEG_INF), cn)
        o_ref[...] = (acc / l).astype(BF16)


def _attn_sample(page_table, q_lat, q_rope, ckv_new, kr_new, cache_kv, cache_kr_t, *, batch, n_new, cp):
    rows = n_new * N_HEADS
    n_pages = page_table.shape[1]
    in_specs = [pl.BlockSpec((rows, KV_RANK), lambda b, c, pt: (b, 0)),
                pl.BlockSpec((rows, QK_ROPE), lambda b, c, pt: (b, 0)),
                pl.BlockSpec((n_new, KV_RANK), lambda b, c, pt: (b, 0)),
                pl.BlockSpec((n_new, QK_ROPE), lambda b, c, pt: (b, 0)),
                pl.BlockSpec(memory_space=pl.ANY), pl.BlockSpec(memory_space=pl.ANY)]
    grid_spec = pltpu.PrefetchScalarGridSpec(
        num_scalar_prefetch=1, grid=(batch, n_pages // cp), in_specs=in_specs,
        out_specs=pl.BlockSpec((rows, KV_RANK), lambda b, c, pt: (b, 0)),
        scratch_shapes=[pltpu.VMEM((2, cp, PAGE_SIZE, KV_RANK), F32), pltpu.VMEM((2, cp, QK_ROPE, PAGE_SIZE), F32),
                        pltpu.SemaphoreType.DMA((2,)), pltpu.SemaphoreType.DMA((2,)),
                        pltpu.VMEM((2, cp * PAGE_SIZE, KV_RANK), BF16), pltpu.VMEM((2, rows, cp * PAGE_SIZE), F32),
                        pltpu.VMEM((rows, 1), F32), pltpu.VMEM((rows, 1), F32), pltpu.VMEM((rows, KV_RANK), F32)])
    return pl.pallas_call(
        functools.partial(_attn_sample_kernel, cp=cp, n_chunks=n_pages // cp, n_new=n_new),
        grid_spec=grid_spec,
        out_shape=jax.ShapeDtypeStruct((batch * rows, KV_RANK), BF16),
        compiler_params=_params(("arbitrary", "arbitrary")),
        name="attn_sample",
    )(page_table, q_lat, q_rope, ckv_new, kr_new, cache_kv, cache_kr_t)


def _mm_kernel(x_ref, w_ref, o_ref):
    o_ref[...] = _dot(x_ref[...], w_ref[...]).astype(o_ref.dtype)


def _mm(x, w, *, tm, out_dtype, name):
    M, K = x.shape
    N = w.shape[1]
    return pl.pallas_call(
        _mm_kernel, grid=(M // tm,),
        in_specs=[pl.BlockSpec((tm, K), lambda i: (i, 0)), _const_spec((K, N))],
        out_specs=pl.BlockSpec((tm, N), lambda i: (i, 0)),
        out_shape=jax.ShapeDtypeStruct((M, N), out_dtype),
        compiler_params=_params(("parallel",)), name=name,
    )(x, w)


def _merge_kernel(ssm_ref, mla_ref, gs_ref, gm_ref, x_ref, wbs_ref, wbm_ref, wo_ref, g_ref, b_ref,
                  wr_ref, br_ref, x1_ref, route_ref, cnt_ref, run_ref, *, tm):
    i = pl.program_id(0)

    @pl.when(i == 0)
    def _():
        run_ref[...] = jnp.zeros(run_ref.shape, F32)

    a = _dot(ssm_ref[...], wbs_ref[...])
    m = _dot(mla_ref[...], wbm_ref[...])
    merged = gs_ref[...].astype(F32) * a + gm_ref[...].astype(F32) * m
    y = _dot(merged.astype(BF16), wo_ref[...])
    x1 = _layer_norm(ALPHA * x_ref[...] + y, g_ref[...], b_ref[...])
    x1_ref[...] = x1

    logits = _dot(x1.astype(BF16), wr_ref[...]) + br_ref[...]
    lane = lax.broadcasted_iota(jnp.int32, (tm, LANES), 1).astype(F32)
    gl = jnp.where(lane < N_GROUPS, logits, NEG_INF)
    gmax = jnp.max(gl, axis=1, keepdims=True)
    g_sel = jnp.min(jnp.where(gl == gmax, lane, float(LANES)), axis=1, keepdims=True)
    onehot = jnp.where(lane == g_sel, 1.0, 0.0)
    r_i = lax.broadcasted_iota(jnp.int32, (tm, tm), 0)
    c_i = lax.broadcasted_iota(jnp.int32, (tm, tm), 1)
    before = jnp.where(c_i < r_i, 1.0, 0.0).astype(BF16)
    rank_all = _dot(before, onehot.astype(BF16)) + run_ref[0:1, :]
    rank = jnp.sum(onehot * rank_all, axis=1, keepdims=True)
    route_ref[...] = jnp.where(lane == 0.0, g_sel, jnp.where(lane == 1.0, rank, 0.0))
    run_ref[0:1, :] = run_ref[0:1, :] + jnp.sum(onehot, axis=0, keepdims=True)

    @pl.when(i == pl.num_programs(0) - 1)
    def _():
        cnt_ref[...] = run_ref[...]


def _merge(ssm, mla, gs, gm, x, wts, *, tm, seq_tiles):
    T = x.shape[0]
    row = lambda i: (i, 0)
    if seq_tiles is None:
        ssm_spec = pl.BlockSpec((tm, SSM_WIDTH), row)
    else:
        ssm_spec = pl.BlockSpec((tm, SSM_WIDTH), lambda i: (i % seq_tiles, i // seq_tiles))
    return pl.pallas_call(
        functools.partial(_merge_kernel, tm=tm),
        grid=(T // tm,),
        in_specs=[ssm_spec, pl.BlockSpec((tm, N_HEADS * V_DIM), row),
                  pl.BlockSpec((tm, D_MODEL), row), pl.BlockSpec((tm, D_MODEL), row),
                  pl.BlockSpec((tm, D_MODEL), row),
                  _const_spec(wts["w_br_ssm"].shape), _const_spec(wts["w_br_mla"].shape),
                  _const_spec(wts["w_out"].shape), _const_spec((1, D_MODEL)), _const_spec((1, D_MODEL)),
                  _const_spec((D_MODEL, LANES)), _const_spec((1, LANES))],
        out_specs=[pl.BlockSpec((tm, D_MODEL), row), pl.BlockSpec((tm, LANES), row),
                   _const_spec((8, LANES))],
        out_shape=[jax.ShapeDtypeStruct((T, D_MODEL), F32), jax.ShapeDtypeStruct((T, LANES), F32),
                   jax.ShapeDtypeStruct((8, LANES), F32)],
        scratch_shapes=[pltpu.VMEM((8, LANES), F32)],
        compiler_params=_params(("arbitrary",)),
        name="merge_ln1_route",
    )(ssm, mla, gs, gm, x, wts["w_br_ssm"], wts["w_br_mla"], wts["w_out"], wts["ln1_g"], wts["ln1_b"],
      wts["wrg"], wts["brg"])


def _dispatch_kernel(pad_ref, x_ref, dest_ref, xs_ref, zero_ref, sem, *, tm):
    i = pl.program_id(0)

    def row_copy(r, d):
        return pltpu.make_async_copy(x_ref.at[pl.ds(r, 1)], xs_ref.at[pl.ds(d, 1)], sem)

    def zero_copy(d):
        return pltpu.make_async_copy(zero_ref.at[pl.ds(0, 1)], xs_ref.at[pl.ds(d, 1)], sem)

    @pl.when(i == 0)
    def _():
        zero_ref[...] = jnp.zeros(zero_ref.shape, F32)
        for g in range(N_GROUPS + 1):
            start = pad_ref[g]
            n = pad_ref[N_GROUPS + 1 + g]

            def issue(r, c, start=start):
                zero_copy(start + r).start()
                return c

            def drain(r, c, start=start):
                zero_copy(start + r).wait()
                return c

            lax.fori_loop(0, n, issue, 0)
            lax.fori_loop(0, n, drain, 0)

    def issue(r8, c):
        for k in range(ROW_DMA_UNROLL):
            r = r8 * ROW_DMA_UNROLL + k
            row_copy(r, dest_ref[0, 0, r]).start(priority=k % 2)
        return c

    lax.fori_loop(0, tm // ROW_DMA_UNROLL, issue, 0)
    pltpu.make_async_copy(x_ref, xs_ref.at[pl.ds(0, tm)], sem).wait()


def _dispatch(x1, dest, pad_info, *, tm, n_rows):
    T = x1.shape[0]
    nt = T // tm
    grid_spec = pltpu.PrefetchScalarGridSpec(
        num_scalar_prefetch=1, grid=(nt,),
        in_specs=[pl.BlockSpec((tm, D_MODEL), lambda i, pad: (i, 0)),
                  pl.BlockSpec((1, 1, tm), lambda i, pad: (i, 0, 0), memory_space=pltpu.SMEM)],
        out_specs=pl.BlockSpec(memory_space=pl.ANY),
        scratch_shapes=[pltpu.VMEM((8, D_MODEL), F32), pltpu.SemaphoreType.DMA(())])
    return pl.pallas_call(
        functools.partial(_dispatch_kernel, tm=tm),
        grid_spec=grid_spec,
        out_shape=jax.ShapeDtypeStruct((n_rows, D_MODEL), F32),
        compiler_params=_params(("arbitrary",)),
        name="moe_dispatch",
    )(pad_info, x1, dest.reshape(nt, 1, tm))


def _moe_kernel(grp_ref, blk_ref, nv_ref, x_ref, wr_ref, br_ref, wg_ref, wu_ref, wd_ref, o_ref, *, tm):
    del blk_ref
    i = pl.program_id(0)

    @pl.when(i < nv_ref[0])
    def _():
        g = grp_ref[i]
        x = x_ref[...]
        xb = x.astype(BF16)
        logits = _dot(xb, wr_ref[0]) + br_ref[0]
        lane = lax.broadcasted_iota(jnp.int32, (tm, LANES), 1).astype(F32)
        is_grp = (lane >= EXPERTS_PER_GROUP) & (lane < EXPERTS_PER_GROUP + N_GROUPS)
        gl = jnp.where(is_grp, logits, NEG_INF)
        gmax = jnp.max(gl, axis=1, keepdims=True)
        gexp = jnp.exp(gl - gmax)
        p_group = (jnp.sum(jnp.where(lane == (EXPERTS_PER_GROUP + g).astype(F32), gexp, 0.0), axis=1, keepdims=True)
                   / jnp.sum(gexp, axis=1, keepdims=True))
        el = jnp.where(lane < EXPERTS_PER_GROUP, logits, NEG_INF)
        v1 = jnp.max(el, axis=1, keepdims=True)
        i1 = jnp.min(jnp.where(el == v1, lane, float(LANES)), axis=1, keepdims=True)
        el2 = jnp.where(lane == i1, NEG_INF, el)
        v2 = jnp.max(el2, axis=1, keepdims=True)
        i2 = jnp.min(jnp.where(el2 == v2, lane, float(LANES)), axis=1, keepdims=True)
        e2 = jnp.exp(v2 - v1)
        w1 = p_group / (1.0 + e2)
        w2 = w1 * e2
        gates = jnp.where(lane == i1, w1, jnp.where(lane == i2, w2, 0.0))

        hs = []
        for e in range(EXPERTS_PER_GROUP):
            hg = _dot(xb, wg_ref[0, e])
            hu = _dot(xb, wu_ref[0, e])
            ge = jnp.sum(jnp.where(lane == float(e), gates, 0.0), axis=1, keepdims=True)
            hs.append((jax.nn.silu(hg) * hu * ge).astype(BF16))
        o_ref[...] = _dot(jnp.concatenate(hs, axis=1), wd_ref[0])

    @pl.when(i >= nv_ref[0])
    def _():
        o_ref[...] = jnp.zeros(o_ref.shape, F32)


def _moe(xs, tile_grp, tile_blk, n_valid, wts, *, tm):
    n_rows = xs.shape[0]
    nt = n_rows // tm
    grp3 = lambda i, grp, blk, nv: (grp[i], 0, 0)
    grp4 = lambda i, grp, blk, nv: (grp[i], 0, 0, 0)
    rows = lambda i, grp, blk, nv: (blk[i], 0)
    grid_spec = pltpu.PrefetchScalarGridSpec(
        num_scalar_prefetch=3, grid=(nt,),
        in_specs=[pl.BlockSpec((tm, D_MODEL), rows),
                  pl.BlockSpec((1, D_MODEL, LANES), grp3), pl.BlockSpec((1, 1, LANES), grp3),
                  pl.BlockSpec((1, EXPERTS_PER_GROUP, D_MODEL, D_EXPERT), grp4),
                  pl.BlockSpec((1, EXPERTS_PER_GROUP, D_MODEL, D_EXPERT), grp4),
                  pl.BlockSpec((1, EXPERTS_PER_GROUP * D_EXPERT, D_MODEL), grp3)],
        out_specs=pl.BlockSpec((tm, D_MODEL), lambda i, grp, blk, nv: (i, 0)))
    return pl.pallas_call(
        functools.partial(_moe_kernel, tm=tm),
        grid_spec=grid_spec,
        out_shape=jax.ShapeDtypeStruct((n_rows, D_MODEL), F32),
        compiler_params=_params(("arbitrary",)),
        name="moe_experts",
    )(tile_grp, tile_blk, n_valid, xs, wts["wroute"], wts["broute"],
      wts["w_e_gate"], wts["w_e_up"], wts["w_e_down"])


def _final_kernel(x1_ref, dest_ref, dest_next_ref, ys_ref, p_ref, g_ref, b_ref, wpg_ref, bpg_ref, wpp_ref, o_ref,
                  moe_ref, sem, *, tm):
    i = pl.program_id(0)
    slot = lax.rem(i, 2)

    def gather(d_ref, sl):
        def issue(r8, c):
            for k in range(ROW_DMA_UNROLL):
                r = r8 * ROW_DMA_UNROLL + k
                pltpu.make_async_copy(ys_ref.at[pl.ds(d_ref[0, 0, r], 1)], moe_ref.at[sl, pl.ds(r, 1)],
                                      sem.at[sl]).start(priority=k % 2)
            return c

        lax.fori_loop(0, tm // ROW_DMA_UNROLL, issue, 0)

    @pl.when(i == 0)
    def _():
        gather(dest_ref, 0)

    @pl.when(i + 1 < pl.num_programs(0))
    def _():
        gather(dest_next_ref, 1 - slot)

    pltpu.make_async_copy(ys_ref.at[pl.ds(0, tm)], moe_ref.at[slot], sem.at[slot]).wait()
    x2 = _layer_norm(ALPHA * x1_ref[...] + moe_ref[slot], g_ref[...], b_ref[...])
    gate = jax.nn.sigmoid(_dot(x2.astype(BF16), wpg_ref[...]) + bpg_ref[...])
    o_ref[...] = x2 + gate * _dot(p_ref[...].astype(BF16), wpp_ref[...])


def _final(x1, dest, ys, p, wts, *, tm):
    T = x1.shape[0]
    nt = T // tm
    row = lambda i: (i, 0)
    return pl.pallas_call(
        functools.partial(_final_kernel, tm=tm),
        grid=(nt,),
        in_specs=[pl.BlockSpec((tm, D_MODEL), row),
                  pl.BlockSpec((1, 1, tm), lambda i: (i, 0, 0), memory_space=pltpu.SMEM),
                  pl.BlockSpec((1, 1, tm), lambda i: (jnp.minimum(i + 1, nt - 1), 0, 0), memory_space=pltpu.SMEM),
                  pl.BlockSpec(memory_space=pl.ANY),
                  pl.BlockSpec((tm, PLE_DIM), row),
                  _const_spec((1, D_MODEL)), _const_spec((1, D_MODEL)),
                  _const_spec((D_MODEL, D_MODEL)), _const_spec((1, D_MODEL)), _const_spec((PLE_DIM, D_MODEL))],
        out_specs=pl.BlockSpec((tm, D_MODEL), row),
        out_shape=jax.ShapeDtypeStruct((T, D_MODEL), F32),
        scratch_shapes=[pltpu.VMEM((2, tm, D_MODEL), F32), pltpu.SemaphoreType.DMA((2,))],
        compiler_params=_params(("arbitrary",)),
        name="combine_ln2_ple",
    )(x1, dest.reshape(nt, 1, tm), dest.reshape(nt, 1, tm), ys, p, wts["ln2_g"], wts["ln2_b"], wts["w_ple_gate"], wts["b_ple_gate"],
      wts["w_ple_proj"])


def _prep_weights(w_in, a_re, a_im, log_dt, b_re, b_im, c_re, c_im, d_skip, w_glu, b_glu, w_br_ssm,
                  q_norm_g, w_uq, kv_norm_g, w_uk, w_uv, w_br_mla, w_out, ln1_g, ln1_b,
                  w_gr, b_gr, w_er, b_er, w_e_gate, w_e_up, w_e_down, ln2_g, ln2_b,
                  w_ple_gate, b_ple_gate, w_ple_proj):
    half = QK_ROPE // 2
    rot = lambda w: jnp.concatenate([-w[..., half:], w[..., :half]], axis=-1)
    w = {}

    u_w, cq_w, ckv_w, kr_w, gs_w, gm_w = jnp.split(
        w_in, (512, 768, 1024, 1024 + QK_ROPE, 1024 + QK_ROPE + D_MODEL), axis=1)
    slab = lambda c: jnp.pad(c, ((0, 0), (QK_NOPE, HEAD_PAD - QK_NOPE - QK_ROPE)))
    w["w_in"] = jnp.concatenate([u_w, cq_w, ckv_w, gs_w, gm_w, slab(kr_w), slab(rot(kr_w))], axis=1).astype(BF16)
    w["q_norm_g"] = q_norm_g.reshape(1, Q_RANK)
    w["kv_norm_g"] = kv_norm_g.reshape(1, KV_RANK)

    wq3 = w_uq.reshape(Q_RANK, N_HEADS, QK_NOPE + QK_ROPE)
    nope, rope = wq3[..., :QK_NOPE], wq3[..., QK_NOPE:]
    pad_tail = jnp.zeros((Q_RANK, N_HEADS, HEAD_PAD - QK_NOPE - QK_ROPE), F32)
    w["wq"] = jnp.concatenate([nope, rope, pad_tail], -1).reshape(Q_RANK, N_HEADS * HEAD_PAD).astype(BF16)
    w["wq_rot"] = jnp.concatenate([jnp.zeros_like(nope), rot(rope), pad_tail], -1).reshape(
        Q_RANK, N_HEADS * HEAD_PAD).astype(BF16)
    w["wq_rope"] = rope.reshape(Q_RANK, N_HEADS * QK_ROPE).astype(BF16)
    w["wq_rope_rot"] = rot(rope).reshape(Q_RANK, N_HEADS * QK_ROPE).astype(BF16)
    head_pad = lambda a: jnp.pad(a, ((0, 0), (0, 0), (0, HEAD_PAD - a.shape[-1]))).reshape(
        a.shape[0], N_HEADS * HEAD_PAD).astype(BF16)
    w["wk"] = head_pad(w_uk)
    w["wv"] = head_pad(w_uv)
    w["wuk_abs"] = jnp.pad(w_uk.transpose(1, 2, 0), ((0, 0), (0, HEAD_PAD - QK_NOPE), (0, 0))).astype(BF16)
    w["wuv_blockdiag"] = (jnp.eye(N_HEADS, dtype=F32)[:, None, :, None]
                          * w_uv.transpose(1, 0, 2)[:, :, None, :]).reshape(
                              N_HEADS * KV_RANK, N_HEADS * V_DIM).astype(BF16)

    dt = jnp.exp(log_dt)[:, None]
    mag = jnp.exp(dt * a_re)
    lr = mag * jnp.cos(dt * a_im)
    li = mag * jnp.sin(dt * a_im)
    den = a_re * a_re + a_im * a_im
    fr = ((lr - 1.0) * a_re + li * a_im) / den
    fi = (li * a_re - (lr - 1.0) * a_im) / den
    bbr = fr[..., None] * b_re - fi[..., None] * b_im
    bbi = fr[..., None] * b_im + fi[..., None] * b_re
    w["lam"] = jnp.stack([lr.reshape(-1), li.reshape(-1)])
    n_tiles = SSM_GROUPS // 2
    col_group = jnp.broadcast_to((2 * jnp.arange(n_tiles)[:, None, None, None]
                                  + jnp.arange(2)[None, None, :, None]), (n_tiles, 2, 2, SSM_STATE)).reshape(-1)
    ch_group = jnp.arange(SSM_WIDTH) // SSM_GROUP
    mask = ch_group[:, None] == col_group[None, :]
    bcols = jnp.stack([bbr, bbi]).reshape(2, n_tiles, 2, SSM_STATE, SSM_GROUP).transpose(
        1, 0, 2, 3, 4).reshape(STATE_COLS, SSM_GROUP)
    bmat = jnp.where(mask, jnp.tile(bcols.T, (SSM_GROUPS, 1)), 0.0)
    bmat4 = bmat.reshape(2, MXU_DIM, n_tiles, MXU_DIM)
    w["s5_wb"] = jnp.stack([bmat4[j // (n_tiles // 2), :, j, :] for j in range(n_tiles)]).astype(BF16)
    ccols = jnp.stack([c_re, -c_im]).reshape(2, n_tiles, 2, SSM_GROUP, SSM_STATE).transpose(
        1, 0, 2, 4, 3).reshape(STATE_COLS, SSM_GROUP)
    cmat = jnp.where(mask.T, jnp.tile(ccols, (1, SSM_GROUPS)), 0.0)
    hc = STATE_COLS // 2
    w["s5_wc"] = jnp.stack([cmat[:hc, :MXU_DIM], cmat[hc:, MXU_DIM:]]).astype(BF16)
    w["d_skip"] = d_skip.reshape(1, SSM_WIDTH)
    w["w_glu"] = w_glu.astype(BF16)
    w["b_glu"] = b_glu.reshape(1, SSM_WIDTH)

    w["w_br_ssm"] = w_br_ssm.astype(BF16)
    w["w_br_mla"] = w_br_mla.astype(BF16)
    w["w_out"] = w_out.astype(BF16)
    w["ln1_g"], w["ln1_b"] = ln1_g.reshape(1, D_MODEL), ln1_b.reshape(1, D_MODEL)
    w["ln2_g"], w["ln2_b"] = ln2_g.reshape(1, D_MODEL), ln2_b.reshape(1, D_MODEL)

    wrg = jnp.pad(w_gr, ((0, 0), (0, LANES - N_GROUPS)))
    w["wrg"] = wrg.astype(BF16)
    w["brg"] = jnp.pad(b_gr, (0, LANES - N_GROUPS)).reshape(1, LANES)
    er = w_er.reshape(D_MODEL, N_GROUPS, EXPERTS_PER_GROUP).transpose(1, 0, 2)
    gr = jnp.broadcast_to(w_gr[None], (N_GROUPS, D_MODEL, N_GROUPS))
    wroute = jnp.pad(jnp.concatenate([er, gr], -1), ((0, 0), (0, 0), (0, LANES - EXPERTS_PER_GROUP - N_GROUPS)))
    w["wroute"] = wroute.astype(BF16)
    broute = jnp.concatenate([b_er.reshape(N_GROUPS, EXPERTS_PER_GROUP),
                              jnp.broadcast_to(b_gr[None], (N_GROUPS, N_GROUPS))], -1)
    w["broute"] = jnp.pad(broute, ((0, 0), (0, LANES - EXPERTS_PER_GROUP - N_GROUPS))).reshape(N_GROUPS, 1, LANES)
    w["w_e_gate"] = w_e_gate.astype(BF16).reshape(N_GROUPS, EXPERTS_PER_GROUP, D_MODEL, D_EXPERT)
    w["w_e_up"] = w_e_up.astype(BF16).reshape(N_GROUPS, EXPERTS_PER_GROUP, D_MODEL, D_EXPERT)
    w["w_e_down"] = w_e_down.astype(BF16).reshape(N_GROUPS, EXPERTS_PER_GROUP * D_EXPERT, D_MODEL)
    w["w_ple_gate"] = w_ple_gate.astype(BF16)
    w["b_ple_gate"] = b_ple_gate.reshape(1, D_MODEL)
    w["w_ple_proj"] = w_ple_proj.astype(BF16)
    return w


def _rope_tables(pos, reps):
    half = QK_ROPE // 2
    inv = jnp.exp(-math.log(ROPE_THETA) * jnp.arange(half, dtype=F32) / half)
    ang = pos.astype(F32)[:, None] * inv[None, :]
    cos2 = jnp.tile(jnp.cos(ang), (1, 2))
    sin2 = jnp.tile(jnp.sin(ang), (1, 2))
    slab = lambda t: jnp.tile(jnp.pad(t, ((0, 0), (QK_NOPE, HEAD_PAD - QK_NOPE - QK_ROPE))), (reps, 1))
    heads = lambda t: jnp.tile(t, (reps, N_HEADS))
    return {"cos128": slab(cos2), "sin128": slab(sin2), "cos_heads": heads(cos2), "sin_heads": heads(sin2)}


def _pack_state(s_re, s_im):
    B = s_re.shape[0]
    st = jnp.stack([s_re.reshape(B, SSM_GROUPS // 2, 2, SSM_STATE), s_im.reshape(B, SSM_GROUPS // 2, 2, SSM_STATE)],
                   axis=2)
    return st.reshape(B, STATE_COLS)


def _unpack_state(s):
    B = s.shape[0]
    st = s.reshape(B, SSM_GROUPS // 2, 2, 2, SSM_STATE)
    return (st[:, :, 0].reshape(B, SSM_GROUPS, SSM_STATE), st[:, :, 1].reshape(B, SSM_GROUPS, SSM_STATE))


def _route_plan(route, counts, *, tm, n_tok):
    g_sel = route[:, 0].astype(jnp.int32)
    rank = route[:, 1].astype(jnp.int32)
    cnt = counts[0, :N_GROUPS].astype(jnp.int32)
    padded = ((cnt + tm - 1) // tm) * tm
    ends = jnp.cumsum(padded)
    offs = ends - padded
    dest = offs[g_sel] + rank
    n_tiles = n_tok // tm + N_GROUPS
    n_valid = ends[-1] // tm
    tile_start = jnp.arange(n_tiles, dtype=jnp.int32) * tm
    tile_grp = jnp.minimum(jnp.sum(tile_start[:, None] >= ends[None, :], axis=1), N_GROUPS - 1).astype(jnp.int32)
    last = jnp.maximum(n_valid - 1, 0)
    tile_blk = jnp.minimum(jnp.arange(n_tiles, dtype=jnp.int32), last)
    tile_grp = jnp.where(jnp.arange(n_tiles) < n_valid, tile_grp, tile_grp[last])
    n_rows = n_tiles * tm
    pad_info = jnp.concatenate([offs + cnt, ends[-1:], padded - cnt, n_rows - ends[-1:]]).astype(jnp.int32)
    return dest, tile_grp, tile_blk, n_valid.reshape(1).astype(jnp.int32), pad_info, n_rows


def _token_tail(ssm, mla, gs, gm, x, p, wts, *, tm, seq_tiles, tm_moe):
    T = x.shape[0]
    x1, route, counts = _merge(ssm, mla, gs, gm, x, wts, tm=tm, seq_tiles=seq_tiles)
    dest, tile_grp, tile_blk, n_valid, pad_info, n_rows = _route_plan(route, counts, tm=tm_moe, n_tok=T)
    xs = _dispatch(x1, dest, pad_info, tm=tm, n_rows=n_rows)
    ys = _moe(xs, tile_grp, tile_blk, n_valid, wts, tm=tm_moe)
    return _final(x1, dest, ys, p, wts, tm=tm)


def kernel(x_prompt, x_sample, p_prompt, p_sample, cache_kv, cache_k_rope, state_ssm_re, state_ssm_im, page_table, w_in, a_re, a_im, log_dt, b_re, b_im, c_re, c_im, d_skip, w_glu, b_glu, w_br_ssm, q_norm_g, w_uq, kv_norm_g, w_uk, w_uv, w_br_mla, w_out, ln1_g, ln1_b, w_gr, b_gr, w_er, b_er, w_e_gate, w_e_up, w_e_down, ln2_g, ln2_b, w_ple_gate, b_ple_gate, w_ple_proj):
    B, S, _ = x_prompt.shape
    Bs, Ss, _ = x_sample.shape
    Tp, Ts = B * S, Bs * Ss
    wts = _prep_weights(w_in[0], a_re[0], a_im[0], log_dt[0], b_re[0], b_im[0], c_re[0], c_im[0], d_skip[0],
                        w_glu[0], b_glu[0], w_br_ssm[0], q_norm_g[0], w_uq[0], kv_norm_g[0], w_uk[0], w_uv[0],
                        w_br_mla[0], w_out[0], ln1_g[0], ln1_b[0], w_gr[0], b_gr[0], w_er[0], b_er[0],
                        w_e_gate[0], w_e_up[0], w_e_down[0], ln2_g[0], ln2_b[0], w_ple_gate[0], b_ple_gate[0],
                        w_ple_proj[0])
    tm = 256
    tm_moe = 256

    tabs_p = _rope_tables(jnp.arange(S, dtype=jnp.int32), 1)
    seq_tiles = S // tm
    u_t, ckv_p, kr_p, gs_p, gm_p, q3, k3, v3 = _proj(
        x_prompt.reshape(Tp, D_MODEL), wts, tabs_p, tm=tm, seq_tiles=seq_tiles, absorbed=False)
    zero_state = jnp.zeros((B, STATE_COLS), F32)
    ssm_p, sfin_p = _s5(u_t.reshape(S, B, SSM_WIDTH), zero_state, wts, lt=32, bb=B, name="s5_prompt")
    mla_p = _attn_prompt(q3, k3, v3, batch=B, seq=S, tq=512, tk=512)
    y_p = _token_tail(ssm_p.reshape(S, B * SSM_WIDTH), mla_p, gs_p, gm_p, x_prompt.reshape(Tp, D_MODEL),
                      p_prompt.reshape(Tp, PLE_DIM), wts, tm=tm, seq_tiles=seq_tiles, tm_moe=tm_moe)

    tabs_s = _rope_tables(PAST_LEN + jnp.arange(Ss, dtype=jnp.int32), tm // Ss)
    u_s, ckv_s, kr_s, gs_s, gm_s, q_lat, q_rope = _proj(
        x_sample.reshape(Ts, D_MODEL), wts, tabs_s, tm=tm, seq_tiles=None, absorbed=True)
    s0 = _pack_state(state_ssm_re[0], state_ssm_im[0])
    u_s_t = u_s.reshape(Bs, Ss, SSM_WIDTH).transpose(1, 0, 2)
    ssm_s_t, sfin_s = _s5(u_s_t, s0, wts, lt=Ss, bb=64, name="s5_sample")
    ssm_s = ssm_s_t.transpose(1, 0, 2).reshape(Ts, SSM_WIDTH)
    kr_s32 = kr_s[:, QK_NOPE:QK_NOPE + QK_ROPE]
    n_pool = cache_kv.shape[1]
    o_lat = _attn_sample(page_table, q_lat.reshape(Ts * N_HEADS, KV_RANK), q_rope.reshape(Ts * N_HEADS, QK_ROPE),
                         ckv_s, kr_s32, cache_kv.reshape(n_pool, PAGE_SIZE, KV_RANK),
                         cache_k_rope.reshape(n_pool, PAGE_SIZE, QK_ROPE).transpose(0, 2, 1),
                         batch=Bs, n_new=Ss, cp=32)
    mla_s = _mm(o_lat.reshape(Ts, N_HEADS * KV_RANK), wts["wuv_blockdiag"], tm=tm, out_dtype=BF16, name="uv_sample")
    y_s = _token_tail(ssm_s, mla_s, gs_s, gm_s, x_sample.reshape(Ts, D_MODEL), p_sample.reshape(Ts, PLE_DIM), wts,
                      tm=tm, seq_tiles=None, tm_moe=tm_moe)

    sre_p, sim_p = _unpack_state(sfin_p)
    sre_s, sim_s = _unpack_state(sfin_s)
    kr_p32 = kr_p[:, QK_NOPE:QK_NOPE + QK_ROPE]
    return (y_p.reshape(B, S, D_MODEL), y_s.reshape(Bs, Ss, D_MODEL),
            ckv_p.reshape(1, B, S, KV_RANK), kr_p32.reshape(1, B, S, QK_ROPE),
            sre_p[None], sim_p[None],
            ckv_s.reshape(1, Bs, Ss, KV_RANK), kr_s32.reshape(1, Bs, Ss, QK_ROPE),
            sre_s[None], sim_s[None])
```

```python
import functools
import math

import jax
import jax.numpy as jnp
from jax import lax
from jax.experimental import pallas as pl
from jax.experimental.pallas import tpu as pltpu

F32 = jnp.float32
BF16 = jnp.bfloat16

D_MODEL = 1024
DEPTH = 1
PAST_LEN = 16384
PAGE_SIZE = 128
SSM_WIDTH = 512
SSM_GROUP = 16
SSM_GROUPS = 32
SSM_STATE = 64
STATE_COLS = 2 * SSM_GROUPS * SSM_STATE
N_HEADS = 8
QK_NOPE = 64
QK_ROPE = 32
V_DIM = 64
Q_RANK = 256
KV_RANK = 256
HEAD_PAD = 128
ROPE_THETA = 10000.0
ATTN_SCALE = (QK_NOPE + QK_ROPE) ** -0.5
Q_SCALE = ATTN_SCALE * math.log2(math.e)
N_GROUPS = 4
EXPERTS_PER_GROUP = 8
D_EXPERT = 256
PLE_DIM = 256
LN_EPS = 1e-5
RMS_EPS = 1e-6
ALPHA = (2 * DEPTH) ** 0.25

LANES = 128
MXU_DIM = 256
VMEM_LIMIT = 56 * 1024 * 1024

ROW_DMA_UNROLL = 256

NEG_INF = float("-inf")


def _params(sem, vmem=VMEM_LIMIT):
    return pltpu.CompilerParams(dimension_semantics=sem, vmem_limit_bytes=vmem)


def _const_spec(shape):
    zeros = (0,) * len(shape)
    return pl.BlockSpec(shape, lambda *_: zeros)


def _dot(a, b):
    return jnp.dot(a, b, preferred_element_type=F32)


def _dot_t(a, b):
    return lax.dot_general(a, b, (((1,), (1,)), ((), ())), preferred_element_type=F32)


def _layer_norm(x, g, b):
    mu = jnp.mean(x, axis=-1, keepdims=True)
    xc = x - mu
    var = jnp.mean(xc * xc, axis=-1, keepdims=True)
    return xc * lax.rsqrt(var + LN_EPS) * g + b


def _rms_norm(x, g):
    ms = jnp.mean(x * x, axis=-1, keepdims=True)
    return x * lax.rsqrt(ms + RMS_EPS) * g


def _proj_kernel(x_ref, w_ref, qg_ref, kg_ref, cos_ref, sin_ref, wq_ref, wqr_ref, *rest, absorbed):
    if absorbed:
        (wuk_ref, wqc_ref, wqcr_ref, cos8_ref, sin8_ref,
         u_ref, ckv_ref, kr_ref, gs_ref, gm_ref, ql_ref, qr_ref) = rest
    else:
        wk_ref, wv_ref, u_ref, ckv_ref, kr_ref, gs_ref, gm_ref, q_ref, k_ref, v_ref = rest
    xb = x_ref[...].astype(BF16)
    u_ref[...] = _dot(xb, w_ref[:, 0:512])
    cq = _rms_norm(_dot(xb, w_ref[:, 512:768]), qg_ref[...])
    ckv = _rms_norm(_dot(xb, w_ref[:, 768:1024]), kg_ref[...])
    ckv_ref[...] = ckv
    gs_ref[...] = jax.nn.sigmoid(_dot(xb, w_ref[:, 1024:2048])).astype(BF16)
    gm_ref[...] = jax.nn.sigmoid(_dot(xb, w_ref[:, 2048:3072])).astype(BF16)
    cos = cos_ref[...]
    sin = sin_ref[...]
    kr = _dot(xb, w_ref[:, 3072:3200]) * cos + _dot(xb, w_ref[:, 3200:3328]) * sin
    kr_ref[...] = kr

    cqb = cq.astype(BF16)
    qa = _dot(cqb, wq_ref[...])
    qb = _dot(cqb, wqr_ref[...])
    lane = lax.broadcasted_iota(jnp.int32, cos.shape, 1)
    cq_tab = (cos + jnp.where(lane < QK_NOPE, 1.0, 0.0)) * Q_SCALE
    sq_tab = sin * Q_SCALE
    ckvb = ckv.astype(BF16)
    if absorbed:
        for h in range(N_HEADS):
            sl = slice(h * HEAD_PAD, (h + 1) * HEAD_PAD)
            qh = (qa[:, sl] * cq_tab + qb[:, sl] * sq_tab).astype(BF16)
            ql_ref[:, h * KV_RANK:(h + 1) * KV_RANK] = _dot(qh, wuk_ref[h]).astype(BF16)
        qc = _dot(cqb, wqc_ref[...]) * cos8_ref[...] + _dot(cqb, wqcr_ref[...]) * sin8_ref[...]
        qr_ref[...] = (qc * Q_SCALE).astype(BF16)
    else:
        kn = _dot(ckvb, wk_ref[...])
        vv = _dot(ckvb, wv_ref[...])
        for h in range(N_HEADS):
            sl = slice(h * HEAD_PAD, (h + 1) * HEAD_PAD)
            q_ref[h] = (qa[:, sl] * cq_tab + qb[:, sl] * sq_tab).astype(BF16)
            k_ref[h] = (kn[:, sl] + kr).astype(BF16)
            v_ref[h] = jnp.where(lane == V_DIM, 1.0, vv[:, sl]).astype(BF16)


def _proj(x, wts, tabs, *, tm, seq_tiles, absorbed):
    T = x.shape[0]
    nt = T // tm
    row = lambda i: (i, 0)
    if seq_tiles is None:
        tab_map = lambda i: (0, 0)
        u_shape, u_spec = (T, SSM_WIDTH), pl.BlockSpec((tm, SSM_WIDTH), row)
    else:
        tab_map = lambda i: (i % seq_tiles, 0)
        u_shape = (seq_tiles * tm, (nt // seq_tiles) * SSM_WIDTH)
        u_spec = pl.BlockSpec((tm, SSM_WIDTH), lambda i: (i % seq_tiles, i // seq_tiles))
    in_specs = [
        pl.BlockSpec((tm, D_MODEL), row),
        _const_spec(wts["w_in"].shape),
        _const_spec((1, Q_RANK)), _const_spec((1, KV_RANK)),
        pl.BlockSpec((tm, HEAD_PAD), tab_map), pl.BlockSpec((tm, HEAD_PAD), tab_map),
        _const_spec(wts["wq"].shape), _const_spec(wts["wq_rot"].shape),
    ]
    args = [x, wts["w_in"], wts["q_norm_g"], wts["kv_norm_g"], tabs["cos128"], tabs["sin128"],
            wts["wq"], wts["wq_rot"]]
    out_shape = [jax.ShapeDtypeStruct(u_shape, F32),
                 jax.ShapeDtypeStruct((T, KV_RANK), F32),
                 jax.ShapeDtypeStruct((T, HEAD_PAD), F32),
                 jax.ShapeDtypeStruct((T, D_MODEL), BF16),
                 jax.ShapeDtypeStruct((T, D_MODEL), BF16)]
    out_specs = [u_spec, pl.BlockSpec((tm, KV_RANK), row), pl.BlockSpec((tm, HEAD_PAD), row),
                 pl.BlockSpec((tm, D_MODEL), row), pl.BlockSpec((tm, D_MODEL), row)]
    if absorbed:
        in_specs += [_const_spec(wts["wuk_abs"].shape), _const_spec(wts["wq_rope"].shape),
                     _const_spec(wts["wq_rope_rot"].shape),
                     pl.BlockSpec((tm, N_HEADS * QK_ROPE), tab_map),
                     pl.BlockSpec((tm, N_HEADS * QK_ROPE), tab_map)]
        args += [wts["wuk_abs"], wts["wq_rope"], wts["wq_rope_rot"], tabs["cos_heads"], tabs["sin_heads"]]
        out_shape += [jax.ShapeDtypeStruct((T, N_HEADS * KV_RANK), BF16),
                      jax.ShapeDtypeStruct((T, N_HEADS * QK_ROPE), BF16)]
        out_specs += [pl.BlockSpec((tm, N_HEADS * KV_RANK), row), pl.BlockSpec((tm, N_HEADS * QK_ROPE), row)]
    else:
        in_specs += [_const_spec(wts["wk"].shape), _const_spec(wts["wv"].shape)]
        args += [wts["wk"], wts["wv"]]
        head = jax.ShapeDtypeStruct((N_HEADS, T, HEAD_PAD), BF16)
        head_spec = pl.BlockSpec((N_HEADS, tm, HEAD_PAD), lambda i: (0, i, 0))
        out_shape += [head, head, head]
        out_specs += [head_spec, head_spec, head_spec]
    return pl.pallas_call(
        functools.partial(_proj_kernel, absorbed=absorbed),
        grid=(nt,), in_specs=in_specs, out_specs=out_specs, out_shape=out_shape,
        compiler_params=_params(("parallel",)),
        name="proj_sample" if absorbed else "proj_prompt",
    )(*args)


def _s5_kernel(u_ref, s0_ref, lam_ref, wb_ref, wc_ref, dskip_ref, wglu_ref, bglu_ref,
               y_ref, sfin_ref, bu_ref, st_ref, *, lt, bb):
    ti = pl.program_id(1)
    rows = lt * bb
    n_tiles = STATE_COLS // MXU_DIM

    @pl.when(ti == 0)
    def _():
        st_ref[...] = s0_ref[...]

    u = u_ref[...].reshape(rows, SSM_WIDTH)
    ub = u.astype(BF16)
    for j in range(n_tiles):
        k0 = MXU_DIM * (j // (n_tiles // 2))
        bu_ref[:, j * MXU_DIM:(j + 1) * MXU_DIM] = _dot(ub[:, k0:k0 + MXU_DIM], wb_ref[j])

    jg = max(1, 64 // bb)
    for j0 in range(0, n_tiles, jg):
        lrs = [jnp.broadcast_to(lam_ref[0:1, (j0 + q) * LANES:(j0 + q + 1) * LANES], (bb, LANES)) for q in range(jg)]
        lis = [jnp.broadcast_to(lam_ref[1:2, (j0 + q) * LANES:(j0 + q + 1) * LANES], (bb, LANES)) for q in range(jg)]

        def body(t, carry, j0=j0, lrs=lrs, lis=lis):
            r0 = pl.multiple_of(t * bb, bb)
            new = []
            for q in range(jg):
                c0 = (j0 + q) * MXU_DIM
                sr, si = carry[2 * q], carry[2 * q + 1]
                nr = lrs[q] * sr - lis[q] * si + bu_ref[pl.ds(r0, bb), c0:c0 + LANES]
                ni = lrs[q] * si + lis[q] * sr + bu_ref[pl.ds(r0, bb), c0 + LANES:c0 + 2 * LANES]
                bu_ref[pl.ds(r0, bb), c0:c0 + LANES] = nr
                bu_ref[pl.ds(r0, bb), c0 + LANES:c0 + 2 * LANES] = ni
                new += [nr, ni]
            return tuple(new)

        init = []
        for q in range(jg):
            c0 = (j0 + q) * MXU_DIM
            init += [st_ref[:, c0:c0 + LANES], st_ref[:, c0 + LANES:c0 + 2 * LANES]]
        fin = lax.fori_loop(0, lt, body, tuple(init), unroll=min(lt, 8))
        for q in range(jg):
            c0 = (j0 + q) * MXU_DIM
            st_ref[:, c0:c0 + LANES] = fin[2 * q]
            st_ref[:, c0 + LANES:c0 + 2 * LANES] = fin[2 * q + 1]

    half = STATE_COLS // 2
    y = jnp.concatenate([_dot(bu_ref[:, 0:half].astype(BF16), wc_ref[0]),
                         _dot(bu_ref[:, half:STATE_COLS].astype(BF16), wc_ref[1])], axis=1)
    y = y + dskip_ref[...] * u
    z = jax.nn.gelu(y, approximate=True)
    gate = jax.nn.sigmoid(_dot(z.astype(BF16), wglu_ref[...]) + bglu_ref[...])
    y_ref[...] = (z * gate).astype(BF16).reshape(lt, bb, SSM_WIDTH)

    @pl.when(ti == pl.num_programs(1) - 1)
    def _():
        sfin_ref[...] = st_ref[...]


def _s5(u3, s0, wts, *, lt, bb, name):
    S, B, _ = u3.shape
    return pl.pallas_call(
        functools.partial(_s5_kernel, lt=lt, bb=bb),
        grid=(B // bb, S // lt),
        in_specs=[pl.BlockSpec((lt, bb, SSM_WIDTH), lambda b, t: (t, b, 0)),
                  pl.BlockSpec((bb, STATE_COLS), lambda b, t: (b, 0)),
                  _const_spec(wts["lam"].shape), _const_spec(wts["s5_wb"].shape),
                  _const_spec(wts["s5_wc"].shape), _const_spec((1, SSM_WIDTH)),
                  _const_spec(wts["w_glu"].shape), _const_spec((1, SSM_WIDTH))],
        out_specs=[pl.BlockSpec((lt, bb, SSM_WIDTH), lambda b, t: (t, b, 0)),
                   pl.BlockSpec((bb, STATE_COLS), lambda b, t: (b, 0))],
        out_shape=[jax.ShapeDtypeStruct((S, B, SSM_WIDTH), BF16),
                   jax.ShapeDtypeStruct((B, STATE_COLS), F32)],
        scratch_shapes=[pltpu.VMEM((lt * bb, STATE_COLS), F32), pltpu.VMEM((bb, STATE_COLS), F32)],
        compiler_params=_params(("parallel", "arbitrary")),
        name=name,
    )(u3, s0, wts["lam"], wts["s5_wb"], wts["s5_wc"], wts["d_skip"], wts["w_glu"], wts["b_glu"])


def _attn_prompt_kernel(q_ref, k_ref, v_ref, o_ref, *, tq, tk):
    qi = pl.program_id(1)
    n_diag = tq // tk
    row = lax.broadcasted_iota(jnp.int32, (tq, tk), 0)
    col = lax.broadcasted_iota(jnp.int32, (tq, tk), 1)
    n_full = qi * n_diag
    lane = lax.broadcasted_iota(jnp.int32, (tq, HEAD_PAD), 1)
    pairs = []
    for h0 in range(0, N_HEADS, 2):
        heads = (h0, h0 + 1)
        qs = [q_ref[h] for h in heads]

        def step(kt, carry, diag=None, heads=heads, qs=qs):
            k0 = pl.multiple_of(kt * tk, tk)
            new = []
            for j, h in enumerate(heads):
                m, acc = carry[2 * j], carry[2 * j + 1]
                s = _dot_t(qs[j], k_ref[h, pl.ds(k0, tk), :])
                if diag is not None:
                    s = jnp.where(col + diag * tk <= row, s, NEG_INF)
                m_new = jnp.maximum(m, jnp.max(s, axis=1, keepdims=True))
                p = jnp.exp2(s - m_new).astype(BF16)
                acc = jnp.exp2(m - m_new) * acc + _dot(p, v_ref[h, pl.ds(k0, tk), :])
                new += [m_new, acc]
            return tuple(new)

        carry = (jnp.full((tq, 1), NEG_INF, F32), jnp.zeros((tq, HEAD_PAD), F32)) * 2
        carry = lax.fori_loop(0, n_full, step, carry)
        for d in range(n_diag):
            carry = step(n_full + d, carry, diag=d)
        o0, o1 = [jnp.where(lane < V_DIM, acc / acc[:, V_DIM:V_DIM + 1], 0.0) for acc in (carry[1], carry[3])]
        pairs.append(o0 + pltpu.roll(o1, V_DIM, axis=1))
    o_ref[...] = jnp.concatenate(pairs, axis=1).astype(BF16)


def _attn_prompt(q3, k3, v3, *, batch, seq, tq, tk):
    nq = seq // tq
    T = batch * seq
    return pl.pallas_call(
        functools.partial(_attn_prompt_kernel, tq=tq, tk=tk),
        grid=(batch, nq),
        in_specs=[pl.BlockSpec((N_HEADS, tq, HEAD_PAD), lambda b, i: (0, b * nq + i, 0)),
                  pl.BlockSpec((N_HEADS, seq, HEAD_PAD), lambda b, i: (0, b, 0)),
                  pl.BlockSpec((N_HEADS, seq, HEAD_PAD), lambda b, i: (0, b, 0))],
        out_specs=pl.BlockSpec((tq, N_HEADS * V_DIM), lambda b, i: (b * nq + i, 0)),
        out_shape=jax.ShapeDtypeStruct((T, N_HEADS * V_DIM), BF16),
        compiler_params=_params(("parallel", "arbitrary")),
        name="attn_prompt",
    )(q3, k3, v3)


def _attn_sample_kernel(pt_ref, ql_ref, qr_ref, qlp_ref, qrp_ref, cn_ref, krn_ref, kv_hbm, kr_hbm, o_ref,
                        kvbuf, krbuf, kv_sem, kr_sem, kvb_ref, s_ref, m_ref, l_ref, acc_ref,
                        *, cp, n_chunks, n_steps, n_new):
    step = pl.program_id(0)
    slot = lax.rem(step, 2)

    def page_copies(chunk, sl):
        b = lax.div(chunk, n_chunks)
        c = lax.rem(chunk, n_chunks)
        out = []
        for k in range(cp):
            page = pt_ref[b, c * cp + k]
            out.append(pltpu.make_async_copy(kv_hbm.at[page], kvbuf.at[sl, k], kv_sem.at[sl]))
            out.append(pltpu.make_async_copy(kr_hbm.at[page], krbuf.at[sl, k], kr_sem.at[sl]))
        return out

    @pl.when(step == 0)
    def _():
        for d in page_copies(0, 0):
            d.start()

    @pl.when(step + 1 < n_steps)
    def _():
        for d in page_copies(step + 1, 1 - slot):
            d.start()

    @pl.when(step < n_steps)
    def _():
        for d in page_copies(step, slot):
            d.wait()

    def reset_state():
        m_ref[...] = jnp.full(m_ref.shape, NEG_INF, F32)
        l_ref[...] = jnp.zeros(l_ref.shape, F32)
        acc_ref[...] = jnp.zeros(acc_ref.shape, F32)

    @pl.when(step == 0)
    def _():
        reset_state()
        kvb_ref[1] = jnp.zeros(kvb_ref.shape[1:], BF16)
        s_ref[1] = jnp.zeros(s_ref.shape[1:], F32)

    ql = ql_ref[...]
    qr = qr_ref[...]

    def update(state, s, vals):
        m, l, acc = state
        m_new = jnp.maximum(m, jnp.max(s, axis=1, keepdims=True))
        alpha = jnp.exp2(m - m_new)
        p = jnp.exp2(s - m_new)
        return (m_new, alpha * l + jnp.sum(p, axis=1, keepdims=True),
                alpha * acc + _dot(p.astype(BF16), vals))

    def fold(sl):
        state = update((m_ref[...], l_ref[...], acc_ref[...]), s_ref[sl], kvb_ref[sl])
        m_ref[...], l_ref[...], acc_ref[...] = state
        return state

    def body(sl):
        kv = kvbuf[sl].reshape(cp * PAGE_SIZE, KV_RANK).astype(BF16)
        kr_t = jnp.concatenate([krbuf[sl, k] for k in range(cp)], axis=1).astype(BF16)
        kvb_ref[sl] = kv
        s_ref[sl] = _dot_t(ql, kv) + _dot(qr, kr_t)
        fold(1 - sl)

    for sl in range(2):
        pl.when(slot == sl)(functools.partial(body, sl))

    @pl.when(lax.rem(step, n_chunks) == 0)
    def _():
        @pl.when(step > 0)
        def _():
            qlp = qlp_ref[...]
            qrp = qrp_ref[...]
            cn = cn_ref[...].astype(BF16)
            krn = krn_ref[...].astype(BF16)
            s = _dot_t(qlp, cn) + _dot_t(qrp, krn)
            rows = qlp.shape[0]
            t_row = jnp.right_shift(lax.broadcasted_iota(jnp.int32, (rows, n_new), 0), 3)
            t_col = lax.broadcasted_iota(jnp.int32, (rows, n_new), 1)
            state = (m_ref[...], l_ref[...], acc_ref[...])
            _, l, acc = update(state, jnp.where(t_col <= t_row, s, NEG_INF), cn)
            o_ref[...] = (acc / l).astype(BF16)

        reset_state()


def _attn_sample(page_table, q_lat, q_rope, ckv_new, kr_new, cache_kv, cache_kr_t, *, batch, n_new, cp):
    rows = n_new * N_HEADS
    n_chunks = page_table.shape[1] // cp
    n_steps = batch * n_chunks
    cur = lambda t, pt: (jnp.minimum(t, n_steps - 1) // n_chunks, 0)
    prev = lambda t, pt: (jnp.maximum(t - 1, 0) // n_chunks, 0)
    in_specs = [pl.BlockSpec((rows, KV_RANK), cur), pl.BlockSpec((rows, QK_ROPE), cur),
                pl.BlockSpec((rows, KV_RANK), prev), pl.BlockSpec((rows, QK_ROPE), prev),
                pl.BlockSpec((n_new, KV_RANK), prev), pl.BlockSpec((n_new, QK_ROPE), prev),
                pl.BlockSpec(memory_space=pl.ANY), pl.BlockSpec(memory_space=pl.ANY)]
    grid_spec = pltpu.PrefetchScalarGridSpec(
        num_scalar_prefetch=1, grid=(n_steps + 1,), in_specs=in_specs,
        out_specs=pl.BlockSpec((rows, KV_RANK), prev),
        scratch_shapes=[pltpu.VMEM((2, cp, PAGE_SIZE, KV_RANK), F32), pltpu.VMEM((2, cp, QK_ROPE, PAGE_SIZE), F32),
                        pltpu.SemaphoreType.DMA((2,)), pltpu.SemaphoreType.DMA((2,)),
                        pltpu.VMEM((2, cp * PAGE_SIZE, KV_RANK), BF16), pltpu.VMEM((2, rows, cp * PAGE_SIZE), F32),
                        pltpu.VMEM((rows, 1), F32), pltpu.VMEM((rows, 1), F32), pltpu.VMEM((rows, KV_RANK), F32)])
    return pl.pallas_call(
        functools.partial(_attn_sample_kernel, cp=cp, n_chunks=n_chunks, n_steps=n_steps, n_new=n_new),
        grid_spec=grid_spec,
        out_shape=jax.ShapeDtypeStruct((batch * rows, KV_RANK), BF16),
        compiler_params=_params(("arbitrary",)),
        name="attn_sample",
    )(page_table, q_lat, q_rope, q_lat, q_rope, ckv_new, kr_new, cache_kv, cache_kr_t)


def _mm_kernel(x_ref, w_ref, o_ref):
    o_ref[...] = _dot(x_ref[...], w_ref[...]).astype(o_ref.dtype)


def _mm(x, w, *, tm, out_dtype, name):
    M, K = x.shape
    N = w.shape[1]
    return pl.pallas_call(
        _mm_kernel, grid=(M // tm,),
        in_specs=[pl.BlockSpec((tm, K), lambda i: (i, 0)), _const_spec((K, N))],
        out_specs=pl.BlockSpec((tm, N), lambda i: (i, 0)),
        out_shape=jax.ShapeDtypeStruct((M, N), out_dtype),
        compiler_params=_params(("parallel",)), name=name,
    )(x, w)


def _merge_kernel(ssm_ref, mla_ref, gs_ref, gm_ref, x_ref, wbs_ref, wbm_ref, wo_ref, g_ref, b_ref,
                  wr_ref, br_ref, x1_ref, route_ref, cnt_ref, run_ref, *, tm):
    i = pl.program_id(0)

    @pl.when(i == 0)
    def _():
        run_ref[...] = jnp.zeros(run_ref.shape, F32)

    a = _dot(ssm_ref[...], wbs_ref[...])
    m = _dot(mla_ref[...], wbm_ref[...])
    merged = gs_ref[...].astype(F32) * a + gm_ref[...].astype(F32) * m
    y = _dot(merged.astype(BF16), wo_ref[...])
    x1 = _layer_norm(ALPHA * x_ref[...] + y, g_ref[...], b_ref[...])
    x1_ref[...] = x1

    logits = _dot(x1.astype(BF16), wr_ref[...]) + br_ref[...]
    lane = lax.broadcasted_iota(jnp.int32, (tm, LANES), 1).astype(F32)
    gl = jnp.where(lane < N_GROUPS, logits, NEG_INF)
    gmax = jnp.max(gl, axis=1, keepdims=True)
    g_sel = jnp.min(jnp.where(gl == gmax, lane, float(LANES)), axis=1, keepdims=True)
    onehot = jnp.where(lane == g_sel, 1.0, 0.0)
    r_i = lax.broadcasted_iota(jnp.int32, (tm, tm), 0)
    c_i = lax.broadcasted_iota(jnp.int32, (tm, tm), 1)
    before = jnp.where(c_i < r_i, 1.0, 0.0).astype(BF16)
    rank_all = _dot(before, onehot.astype(BF16)) + run_ref[0:1, :]
    rank = jnp.sum(onehot * rank_all, axis=1, keepdims=True)
    route_ref[...] = jnp.where(lane == 0.0, g_sel, jnp.where(lane == 1.0, rank, 0.0))
    run_ref[0:1, :] = run_ref[0:1, :] + jnp.sum(onehot, axis=0, keepdims=True)

    @pl.when(i == pl.num_programs(0) - 1)
    def _():
        cnt_ref[...] = run_ref[...]


def _merge(ssm, mla, gs, gm, x, wts, *, tm, seq_tiles):
    T = x.shape[0]
    row = lambda i: (i, 0)
    if seq_tiles is None:
        ssm_spec = pl.BlockSpec((tm, SSM_WIDTH), row)
    else:
        ssm_spec = pl.BlockSpec((tm, SSM_WIDTH), lambda i: (i % seq_tiles, i // seq_tiles))
    return pl.pallas_call(
        functools.partial(_merge_kernel, tm=tm),
        grid=(T // tm,),
        in_specs=[ssm_spec, pl.BlockSpec((tm, N_HEADS * V_DIM), row),
                  pl.BlockSpec((tm, D_MODEL), row), pl.BlockSpec((tm, D_MODEL), row),
                  pl.BlockSpec((tm, D_MODEL), row),
                  _const_spec(wts["w_br_ssm"].shape), _const_spec(wts["w_br_mla"].shape),
                  _const_spec(wts["w_out"].shape), _const_spec((1, D_MODEL)), _const_spec((1, D_MODEL)),
                  _const_spec((D_MODEL, LANES)), _const_spec((1, LANES))],
        out_specs=[pl.BlockSpec((tm, D_MODEL), row), pl.BlockSpec((tm, LANES), row),
                   _const_spec((8, LANES))],
        out_shape=[jax.ShapeDtypeStruct((T, D_MODEL), F32), jax.ShapeDtypeStruct((T, LANES), F32),
                   jax.ShapeDtypeStruct((8, LANES), F32)],
        scratch_shapes=[pltpu.VMEM((8, LANES), F32)],
        compiler_params=_params(("arbitrary",)),
        name="merge_ln1_route",
    )(ssm, mla, gs, gm, x, wts["w_br_ssm"], wts["w_br_mla"], wts["w_out"], wts["ln1_g"], wts["ln1_b"],
      wts["wrg"], wts["brg"])


def _dispatch_kernel(pad_ref, x_ref, dest_ref, xs_ref, zero_ref, sem, *, tm):
    i = pl.program_id(0)

    def row_copy(r, d):
        return pltpu.make_async_copy(x_ref.at[pl.ds(r, 1)], xs_ref.at[pl.ds(d, 1)], sem)

    def zero_copy(d):
        return pltpu.make_async_copy(zero_ref.at[pl.ds(0, 1)], xs_ref.at[pl.ds(d, 1)], sem)

    @pl.when(i == 0)
    def _():
        zero_ref[...] = jnp.zeros(zero_ref.shape, F32)
        for g in range(N_GROUPS + 1):
            start = pad_ref[g]
            n = pad_ref[N_GROUPS + 1 + g]

            def issue(r, c, start=start):
                zero_copy(start + r).start()
                return c

            def drain(r, c, start=start):
                zero_copy(start + r).wait()
                return c

            lax.fori_loop(0, n, issue, 0)
            lax.fori_loop(0, n, drain, 0)

    def issue(r8, c):
        for k in range(ROW_DMA_UNROLL):
            r = r8 * ROW_DMA_UNROLL + k
            row_copy(r, dest_ref[0, 0, r]).start(priority=k % 2)
        return c

    lax.fori_loop(0, tm // ROW_DMA_UNROLL, issue, 0)
    pltpu.make_async_copy(x_ref, xs_ref.at[pl.ds(0, tm)], sem).wait()


def _dispatch(x1, dest, pad_info, *, tm, n_rows):
    T = x1.shape[0]
    nt = T // tm
    grid_spec = pltpu.PrefetchScalarGridSpec(
        num_scalar_prefetch=1, grid=(nt,),
        in_specs=[pl.BlockSpec((tm, D_MODEL), lambda i, pad: (i, 0)),
                  pl.BlockSpec((1, 1, tm), lambda i, pad: (i, 0, 0), memory_space=pltpu.SMEM)],
        out_specs=pl.BlockSpec(memory_space=pl.ANY),
        scratch_shapes=[pltpu.VMEM((8, D_MODEL), F32), pltpu.SemaphoreType.DMA(())])
    return pl.pallas_call(
        functools.partial(_dispatch_kernel, tm=tm),
        grid_spec=grid_spec,
        out_shape=jax.ShapeDtypeStruct((n_rows, D_MODEL), F32),
        compiler_params=_params(("arbitrary",)),
        name="moe_dispatch",
    )(pad_info, x1, dest.reshape(nt, 1, tm))


def _moe_kernel(grp_ref, blk_ref, nv_ref, x_ref, wr_ref, br_ref, wg_ref, wu_ref, wd_ref, o_ref, *, tm):
    del blk_ref
    i = pl.program_id(0)

    @pl.when(i < nv_ref[0])
    def _():
        g = grp_ref[i]
        x = x_ref[...]
        xb = x.astype(BF16)
        logits = _dot(xb, wr_ref[0]) + br_ref[0]
        lane = lax.broadcasted_iota(jnp.int32, (tm, LANES), 1).astype(F32)
        is_grp = (lane >= EXPERTS_PER_GROUP) & (lane < EXPERTS_PER_GROUP + N_GROUPS)
        gl = jnp.where(is_grp, logits, NEG_INF)
        gmax = jnp.max(gl, axis=1, keepdims=True)
        gexp = jnp.exp(gl - gmax)
        p_group = (jnp.sum(jnp.where(lane == (EXPERTS_PER_GROUP + g).astype(F32), gexp, 0.0), axis=1, keepdims=True)
                   / jnp.sum(gexp, axis=1, keepdims=True))
        el = jnp.where(lane < EXPERTS_PER_GROUP, logits, NEG_INF)
        v1 = jnp.max(el, axis=1, keepdims=True)
        i1 = jnp.min(jnp.where(el == v1, lane, float(LANES)), axis=1, keepdims=True)
        el2 = jnp.where(lane == i1, NEG_INF, el)
        v2 = jnp.max(el2, axis=1, keepdims=True)
        i2 = jnp.min(jnp.where(el2 == v2, lane, float(LANES)), axis=1, keepdims=True)
        e2 = jnp.exp(v2 - v1)
        w1 = p_group / (1.0 + e2)
        w2 = w1 * e2
        gates = jnp.where(lane == i1, w1, jnp.where(lane == i2, w2, 0.0))

        hs = []
        for e in range(EXPERTS_PER_GROUP):
            hg = _dot(xb, wg_ref[0, e])
            hu = _dot(xb, wu_ref[0, e])
            ge = jnp.sum(jnp.where(lane == float(e), gates, 0.0), axis=1, keepdims=True)
            hs.append((jax.nn.silu(hg) * hu * ge).astype(BF16))
        o_ref[...] = _dot(jnp.concatenate(hs, axis=1), wd_ref[0])

    @pl.when(i >= nv_ref[0])
    def _():
        o_ref[...] = jnp.zeros(o_ref.shape, F32)


def _moe(xs, tile_grp, tile_blk, n_valid, wts, *, tm):
    n_rows = xs.shape[0]
    nt = n_rows // tm
    grp3 = lambda i, grp, blk, nv: (grp[i], 0, 0)
    grp4 = lambda i, grp, blk, nv: (grp[i], 0, 0, 0)
    rows = lambda i, grp, blk, nv: (blk[i], 0)
    grid_spec = pltpu.PrefetchScalarGridSpec(
        num_scalar_prefetch=3, grid=(nt,),
        in_specs=[pl.BlockSpec((tm, D_MODEL), rows),
                  pl.BlockSpec((1, D_MODEL, LANES), grp3), pl.BlockSpec((1, 1, LANES), grp3),
                  pl.BlockSpec((1, EXPERTS_PER_GROUP, D_MODEL, D_EXPERT), grp4),
                  pl.BlockSpec((1, EXPERTS_PER_GROUP, D_MODEL, D_EXPERT), grp4),
                  pl.BlockSpec((1, EXPERTS_PER_GROUP * D_EXPERT, D_MODEL), grp3)],
        out_specs=pl.BlockSpec((tm, D_MODEL), lambda i, grp, blk, nv: (i, 0)))
    return pl.pallas_call(
        functools.partial(_moe_kernel, tm=tm),
        grid_spec=grid_spec,
        out_shape=jax.ShapeDtypeStruct((n_rows, D_MODEL), F32),
        compiler_params=_params(("arbitrary",)),
        name="moe_experts",
    )(tile_grp, tile_blk, n_valid, xs, wts["wroute"], wts["broute"],
      wts["w_e_gate"], wts["w_e_up"], wts["w_e_down"])


def _final_kernel(x1_ref, dest_ref, dest_next_ref, ys_ref, p_ref, g_ref, b_ref, wpg_ref, bpg_ref, wpp_ref, o_ref,
                  moe_ref, sem, *, tm):
    i = pl.program_id(0)
    slot = lax.rem(i, 2)

    def gather(d_ref, sl):
        def issue(r8, c):
            for k in range(ROW_DMA_UNROLL):
                r = r8 * ROW_DMA_UNROLL + k
                pltpu.make_async_copy(ys_ref.at[pl.ds(d_ref[0, 0, r], 1)], moe_ref.at[sl, pl.ds(r, 1)],
                                      sem.at[sl]).start(priority=k % 2)
            return c

        lax.fori_loop(0, tm // ROW_DMA_UNROLL, issue, 0)

    @pl.when(i == 0)
    def _():
        gather(dest_ref, 0)

    @pl.when(i + 1 < pl.num_programs(0))
    def _():
        gather(dest_next_ref, 1 - slot)

    pltpu.make_async_copy(ys_ref.at[pl.ds(0, tm)], moe_ref.at[slot], sem.at[slot]).wait()
    x2 = _layer_norm(ALPHA * x1_ref[...] + moe_ref[slot], g_ref[...], b_ref[...])
    gate = jax.nn.sigmoid(_dot(x2.astype(BF16), wpg_ref[...]) + bpg_ref[...])
    o_ref[...] = x2 + gate * _dot(p_ref[...].astype(BF16), wpp_ref[...])


def _final(x1, dest, ys, p, wts, *, tm):
    T = x1.shape[0]
    nt = T // tm
    row = lambda i: (i, 0)
    return pl.pallas_call(
        functools.partial(_final_kernel, tm=tm),
        grid=(nt,),
        in_specs=[pl.BlockSpec((tm, D_MODEL), row),
                  pl.BlockSpec((1, 1, tm), lambda i: (i, 0, 0), memory_space=pltpu.SMEM),
                  pl.BlockSpec((1, 1, tm), lambda i: (jnp.minimum(i + 1, nt - 1), 0, 0), memory_space=pltpu.SMEM),
                  pl.BlockSpec(memory_space=pl.ANY),
                  pl.BlockSpec((tm, PLE_DIM), row),
                  _const_spec((1, D_MODEL)), _const_spec((1, D_MODEL)),
                  _const_spec((D_MODEL, D_MODEL)), _const_spec((1, D_MODEL)), _const_spec((PLE_DIM, D_MODEL))],
        out_specs=pl.BlockSpec((tm, D_MODEL), row),
        out_shape=jax.ShapeDtypeStruct((T, D_MODEL), F32),
        scratch_shapes=[pltpu.VMEM((2, tm, D_MODEL), F32), pltpu.SemaphoreType.DMA((2,))],
        compiler_params=_params(("arbitrary",)),
        name="combine_ln2_ple",
    )(x1, dest.reshape(nt, 1, tm), dest.reshape(nt, 1, tm), ys, p, wts["ln2_g"], wts["ln2_b"], wts["w_ple_gate"], wts["b_ple_gate"],
      wts["w_ple_proj"])


def _prep_weights(w_in, a_re, a_im, log_dt, b_re, b_im, c_re, c_im, d_skip, w_glu, b_glu, w_br_ssm,
                  q_norm_g, w_uq, kv_norm_g, w_uk, w_uv, w_br_mla, w_out, ln1_g, ln1_b,
                  w_gr, b_gr, w_er, b_er, w_e_gate, w_e_up, w_e_down, ln2_g, ln2_b,
                  w_ple_gate, b_ple_gate, w_ple_proj):
    half = QK_ROPE // 2
    rot = lambda w: jnp.concatenate([-w[..., half:], w[..., :half]], axis=-1)
    w = {}

    u_w, cq_w, ckv_w, kr_w, gs_w, gm_w = jnp.split(
        w_in, (512, 768, 1024, 1024 + QK_ROPE, 1024 + QK_ROPE + D_MODEL), axis=1)
    slab = lambda c: jnp.pad(c, ((0, 0), (QK_NOPE, HEAD_PAD - QK_NOPE - QK_ROPE)))
    w["w_in"] = jnp.concatenate([u_w, cq_w, ckv_w, gs_w, gm_w, slab(kr_w), slab(rot(kr_w))], axis=1).astype(BF16)
    w["q_norm_g"] = q_norm_g.reshape(1, Q_RANK)
    w["kv_norm_g"] = kv_norm_g.reshape(1, KV_RANK)

    wq3 = w_uq.reshape(Q_RANK, N_HEADS, QK_NOPE + QK_ROPE)
    nope, rope = wq3[..., :QK_NOPE], wq3[..., QK_NOPE:]
    pad_tail = jnp.zeros((Q_RANK, N_HEADS, HEAD_PAD - QK_NOPE - QK_ROPE), F32)
    w["wq"] = jnp.concatenate([nope, rope, pad_tail], -1).reshape(Q_RANK, N_HEADS * HEAD_PAD).astype(BF16)
    w["wq_rot"] = jnp.concatenate([jnp.zeros_like(nope), rot(rope), pad_tail], -1).reshape(
        Q_RANK, N_HEADS * HEAD_PAD).astype(BF16)
    w["wq_rope"] = rope.reshape(Q_RANK, N_HEADS * QK_ROPE).astype(BF16)
    w["wq_rope_rot"] = rot(rope).reshape(Q_RANK, N_HEADS * QK_ROPE).astype(BF16)
    head_pad = lambda a: jnp.pad(a, ((0, 0), (0, 0), (0, HEAD_PAD - a.shape[-1]))).reshape(
        a.shape[0], N_HEADS * HEAD_PAD).astype(BF16)
    w["wk"] = head_pad(w_uk)
    w["wv"] = head_pad(w_uv)
    w["wuk_abs"] = jnp.pad(w_uk.transpose(1, 2, 0), ((0, 0), (0, HEAD_PAD - QK_NOPE), (0, 0))).astype(BF16)
    w["wuv_blockdiag"] = (jnp.eye(N_HEADS, dtype=F32)[:, None, :, None]
                          * w_uv.transpose(1, 0, 2)[:, :, None, :]).reshape(
                              N_HEADS * KV_RANK, N_HEADS * V_DIM).astype(BF16)

    dt = jnp.exp(log_dt)[:, None]
    mag = jnp.exp(dt * a_re)
    lr = mag * jnp.cos(dt * a_im)
    li = mag * jnp.sin(dt * a_im)
    den = a_re * a_re + a_im * a_im
    fr = ((lr - 1.0) * a_re + li * a_im) / den
    fi = (li * a_re - (lr - 1.0) * a_im) / den
    bbr = fr[..., None] * b_re - fi[..., None] * b_im
    bbi = fr[..., None] * b_im + fi[..., None] * b_re
    w["lam"] = jnp.stack([lr.reshape(-1), li.reshape(-1)])
    n_tiles = SSM_GROUPS // 2
    col_group = jnp.broadcast_to((2 * jnp.arange(n_tiles)[:, None, None, None]
                                  + jnp.arange(2)[None, None, :, None]), (n_tiles, 2, 2, SSM_STATE)).reshape(-1)
    ch_group = jnp.arange(SSM_WIDTH) // SSM_GROUP
    mask = ch_group[:, None] == col_group[None, :]
    bcols = jnp.stack([bbr, bbi]).reshape(2, n_tiles, 2, SSM_STATE, SSM_GROUP).transpose(
        1, 0, 2, 3, 4).reshape(STATE_COLS, SSM_GROUP)
    bmat = jnp.where(mask, jnp.tile(bcols.T, (SSM_GROUPS, 1)), 0.0)
    bmat4 = bmat.reshape(2, MXU_DIM, n_tiles, MXU_DIM)
    w["s5_wb"] = jnp.stack([bmat4[j // (n_tiles // 2), :, j, :] for j in range(n_tiles)]).astype(BF16)
    ccols = jnp.stack([c_re, -c_im]).reshape(2, n_tiles, 2, SSM_GROUP, SSM_STATE).transpose(
        1, 0, 2, 4, 3).reshape(STATE_COLS, SSM_GROUP)
    cmat = jnp.where(mask.T, jnp.tile(ccols, (1, SSM_GROUPS)), 0.0)
    hc = STATE_COLS // 2
    w["s5_wc"] = jnp.stack([cmat[:hc, :MXU_DIM], cmat[hc:, MXU_DIM:]]).astype(BF16)
    w["d_skip"] = d_skip.reshape(1, SSM_WIDTH)
    w["w_glu"] = w_glu.astype(BF16)
    w["b_glu"] = b_glu.reshape(1, SSM_WIDTH)

    w["w_br_ssm"] = w_br_ssm.astype(BF16)
    w["w_br_mla"] = w_br_mla.astype(BF16)
    w["w_out"] = w_out.astype(BF16)
    w["ln1_g"], w["ln1_b"] = ln1_g.reshape(1, D_MODEL), ln1_b.reshape(1, D_MODEL)
    w["ln2_g"], w["ln2_b"] = ln2_g.reshape(1, D_MODEL), ln2_b.reshape(1, D_MODEL)

    wrg = jnp.pad(w_gr, ((0, 0), (0, LANES - N_GROUPS)))
    w["wrg"] = wrg.astype(BF16)
    w["brg"] = jnp.pad(b_gr, (0, LANES - N_GROUPS)).reshape(1, LANES)
    er = w_er.reshape(D_MODEL, N_GROUPS, EXPERTS_PER_GROUP).transpose(1, 0, 2)
    gr = jnp.broadcast_to(w_gr[None], (N_GROUPS, D_MODEL, N_GROUPS))
    wroute = jnp.pad(jnp.concatenate([er, gr], -1), ((0, 0), (0, 0), (0, LANES - EXPERTS_PER_GROUP - N_GROUPS)))
    w["wroute"] = wroute.astype(BF16)
    broute = jnp.concatenate([b_er.reshape(N_GROUPS, EXPERTS_PER_GROUP),
                              jnp.broadcast_to(b_gr[None], (N_GROUPS, N_GROUPS))], -1)
    w["broute"] = jnp.pad(broute, ((0, 0), (0, LANES - EXPERTS_PER_GROUP - N_GROUPS))).reshape(N_GROUPS, 1, LANES)
    w["w_e_gate"] = w_e_gate.astype(BF16).reshape(N_GROUPS, EXPERTS_PER_GROUP, D_MODEL, D_EXPERT)
    w["w_e_up"] = w_e_up.astype(BF16).reshape(N_GROUPS, EXPERTS_PER_GROUP, D_MODEL, D_EXPERT)
    w["w_e_down"] = w_e_down.astype(BF16).reshape(N_GROUPS, EXPERTS_PER_GROUP * D_EXPERT, D_MODEL)
    w["w_ple_gate"] = w_ple_gate.astype(BF16)
    w["b_ple_gate"] = b_ple_gate.reshape(1, D_MODEL)
    w["w_ple_proj"] = w_ple_proj.astype(BF16)
    return w


def _rope_tables(pos, reps):
    half = QK_ROPE // 2
    inv = jnp.exp(-math.log(ROPE_THETA) * jnp.arange(half, dtype=F32) / half)
    ang = pos.astype(F32)[:, None] * inv[None, :]
    cos2 = jnp.tile(jnp.cos(ang), (1, 2))
    sin2 = jnp.tile(jnp.sin(ang), (1, 2))
    slab = lambda t: jnp.tile(jnp.pad(t, ((0, 0), (QK_NOPE, HEAD_PAD - QK_NOPE - QK_ROPE))), (reps, 1))
    heads = lambda t: jnp.tile(t, (reps, N_HEADS))
    return {"cos128": slab(cos2), "sin128": slab(sin2), "cos_heads": heads(cos2), "sin_heads": heads(sin2)}


def _pack_state(s_re, s_im):
    B = s_re.shape[0]
    st = jnp.stack([s_re.reshape(B, SSM_GROUPS // 2, 2, SSM_STATE), s_im.reshape(B, SSM_GROUPS // 2, 2, SSM_STATE)],
                   axis=2)
    return st.reshape(B, STATE_COLS)


def _unpack_state(s):
    B = s.shape[0]
    st = s.reshape(B, SSM_GROUPS // 2, 2, 2, SSM_STATE)
    return (st[:, :, 0].reshape(B, SSM_GROUPS, SSM_STATE), st[:, :, 1].reshape(B, SSM_GROUPS, SSM_STATE))


def _route_plan(route, counts, *, tm, n_tok):
    g_sel = route[:, 0].astype(jnp.int32)
    rank = route[:, 1].astype(jnp.int32)
    cnt = counts[0, :N_GROUPS].astype(jnp.int32)
    padded = ((cnt + tm - 1) // tm) * tm
    ends = jnp.cumsum(padded)
    offs = ends - padded
    dest = offs[g_sel] + rank
    n_tiles = n_tok // tm + N_GROUPS
    n_valid = ends[-1] // tm
    tile_start = jnp.arange(n_tiles, dtype=jnp.int32) * tm
    tile_grp = jnp.minimum(jnp.sum(tile_start[:, None] >= ends[None, :], axis=1), N_GROUPS - 1).astype(jnp.int32)
    last = jnp.maximum(n_valid - 1, 0)
    tile_blk = jnp.minimum(jnp.arange(n_tiles, dtype=jnp.int32), last)
    tile_grp = jnp.where(jnp.arange(n_tiles) < n_valid, tile_grp, tile_grp[last])
    n_rows = n_tiles * tm
    pad_info = jnp.concatenate([offs + cnt, ends[-1:], padded - cnt, n_rows - ends[-1:]]).astype(jnp.int32)
    return dest, tile_grp, tile_blk, n_valid.reshape(1).astype(jnp.int32), pad_info, n_rows


def _token_tail(ssm, mla, gs, gm, x, p, wts, *, tm, seq_tiles, tm_moe):
    T = x.shape[0]
    x1, route, counts = _merge(ssm, mla, gs, gm, x, wts, tm=tm, seq_tiles=seq_tiles)
    dest, tile_grp, tile_blk, n_valid, pad_info, n_rows = _route_plan(route, counts, tm=tm_moe, n_tok=T)
    xs = _dispatch(x1, dest, pad_info, tm=tm, n_rows=n_rows)
    ys = _moe(xs, tile_grp, tile_blk, n_valid, wts, tm=tm_moe)
    return _final(x1, dest, ys, p, wts, tm=tm)


def kernel(x_prompt, x_sample, p_prompt, p_sample, cache_kv, cache_k_rope, state_ssm_re, state_ssm_im, page_table, w_in, a_re, a_im, log_dt, b_re, b_im, c_re, c_im, d_skip, w_glu, b_glu, w_br_ssm, q_norm_g, w_uq, kv_norm_g, w_uk, w_uv, w_br_mla, w_out, ln1_g, ln1_b, w_gr, b_gr, w_er, b_er, w_e_gate, w_e_up, w_e_down, ln2_g, ln2_b, w_ple_gate, b_ple_gate, w_ple_proj):
    B, S, _ = x_prompt.shape
    Bs, Ss, _ = x_sample.shape
    Tp, Ts = B * S, Bs * Ss
    wts = _prep_weights(w_in[0], a_re[0], a_im[0], log_dt[0], b_re[0], b_im[0], c_re[0], c_im[0], d_skip[0],
                        w_glu[0], b_glu[0], w_br_ssm[0], q_norm_g[0], w_uq[0], kv_norm_g[0], w_uk[0], w_uv[0],
                        w_br_mla[0], w_out[0], ln1_g[0], ln1_b[0], w_gr[0], b_gr[0], w_er[0], b_er[0],
                        w_e_gate[0], w_e_up[0], w_e_down[0], ln2_g[0], ln2_b[0], w_ple_gate[0], b_ple_gate[0],
                        w_ple_proj[0])
    tm = 256
    tm_moe = 256

    tabs_p = _rope_tables(jnp.arange(S, dtype=jnp.int32), 1)
    seq_tiles = S // tm
    u_t, ckv_p, kr_p, gs_p, gm_p, q3, k3, v3 = _proj(
        x_prompt.reshape(Tp, D_MODEL), wts, tabs_p, tm=tm, seq_tiles=seq_tiles, absorbed=False)
    zero_state = jnp.zeros((B, STATE_COLS), F32)
    ssm_p, sfin_p = _s5(u_t.reshape(S, B, SSM_WIDTH), zero_state, wts, lt=32, bb=B, name="s5_prompt")
    mla_p = _attn_prompt(q3, k3, v3, batch=B, seq=S, tq=512, tk=512)
    y_p = _token_tail(ssm_p.reshape(S, B * SSM_WIDTH), mla_p, gs_p, gm_p, x_prompt.reshape(Tp, D_MODEL),
                      p_prompt.reshape(Tp, PLE_DIM), wts, tm=tm, seq_tiles=seq_tiles, tm_moe=tm_moe)

    tabs_s = _rope_tables(PAST_LEN + jnp.arange(Ss, dtype=jnp.int32), tm // Ss)
    u_s, ckv_s, kr_s, gs_s, gm_s, q_lat, q_rope = _proj(
        x_sample.reshape(Ts, D_MODEL), wts, tabs_s, tm=tm, seq_tiles=None, absorbed=True)
    s0 = _pack_state(state_ssm_re[0], state_ssm_im[0])
    u_s_t = u_s.reshape(Bs, Ss, SSM_WIDTH).transpose(1, 0, 2)
    ssm_s_t, sfin_s = _s5(u_s_t, s0, wts, lt=Ss, bb=64, name="s5_sample")
    ssm_s = ssm_s_t.transpose(1, 0, 2).reshape(Ts, SSM_WIDTH)
    kr_s32 = kr_s[:, QK_NOPE:QK_NOPE + QK_ROPE]
    n_pool = cache_kv.shape[1]
    o_lat = _attn_sample(page_table, q_lat.reshape(Ts * N_HEADS, KV_RANK), q_rope.reshape(Ts * N_HEADS, QK_ROPE),
                         ckv_s, kr_s32, cache_kv.reshape(n_pool, PAGE_SIZE, KV_RANK),
                         cache_k_rope.reshape(n_pool, PAGE_SIZE, QK_ROPE).transpose(0, 2, 1),
                         batch=Bs, n_new=Ss, cp=64)
    mla_s = _mm(o_lat.reshape(Ts, N_HEADS * KV_RANK), wts["wuv_blockdiag"], tm=tm, out_dtype=BF16, name="uv_sample")
    y_s = _token_tail(ssm_s, mla_s, gs_s, gm_s, x_sample.reshape(Ts, D_MODEL), p_sample.reshape(Ts, PLE_DIM), wts,
                      tm=tm, seq_tiles=None, tm_moe=tm_moe)

    sre_p, sim_p = _unpack_state(sfin_p)
    sre_s, sim_s = _unpack_state(sfin_s)
    kr_p32 = kr_p[:, QK_NOPE:QK_NOPE + QK_ROPE]
    return (y_p.reshape(B, S, D_MODEL), y_s.reshape(Bs, Ss, D_MODEL),
            ckv_p.reshape(1, B, S, KV_RANK), kr_p32.reshape(1, B, S, QK_ROPE),
            sre_p[None], sim_p[None],
            ckv_s.reshape(1, Bs, Ss, KV_RANK), kr_s32.reshape(1, Bs, Ss, QK_ROPE),
            sre_s[None], sim_s[None])
```

```python
import functools
import math

import jax
import jax.numpy as jnp
from jax import lax
from jax.experimental import pallas as pl
from jax.experimental.pallas import tpu as pltpu

F32 = jnp.float32
BF16 = jnp.bfloat16

D_MODEL = 1024
DEPTH = 1
PAST_LEN = 16384
PAGE_SIZE = 128
SSM_WIDTH = 512
SSM_GROUP = 16
SSM_GROUPS = 32
SSM_STATE = 64
STATE_COLS = 2 * SSM_GROUPS * SSM_STATE
N_HEADS = 8
QK_NOPE = 64
QK_ROPE = 32
V_DIM = 64
Q_RANK = 256
KV_RANK = 256
HEAD_PAD = 128
ROPE_THETA = 10000.0
ATTN_SCALE = (QK_NOPE + QK_ROPE) ** -0.5
Q_SCALE = ATTN_SCALE * math.log2(math.e)
N_GROUPS = 4
EXPERTS_PER_GROUP = 8
D_EXPERT = 256
PLE_DIM = 256
LN_EPS = 1e-5
RMS_EPS = 1e-6
ALPHA = (2 * DEPTH) ** 0.25

LANES = 128
MXU_DIM = 256
VMEM_LIMIT = 56 * 1024 * 1024

ATTN_HEADS_PER_BODY = 4
ROW_DMA_UNROLL = 256

NEG_INF = float("-inf")


def _params(sem, vmem=VMEM_LIMIT):
    return pltpu.CompilerParams(dimension_semantics=sem, vmem_limit_bytes=vmem)


def _const_spec(shape):
    zeros = (0,) * len(shape)
    return pl.BlockSpec(shape, lambda *_: zeros)


def _dot(a, b):
    return jnp.dot(a, b, preferred_element_type=F32)


def _dot_t(a, b):
    return lax.dot_general(a, b, (((1,), (1,)), ((), ())), preferred_element_type=F32)


def _layer_norm(x, g, b):
    mu = jnp.mean(x, axis=-1, keepdims=True)
    xc = x - mu
    var = jnp.mean(xc * xc, axis=-1, keepdims=True)
    return xc * lax.rsqrt(var + LN_EPS) * g + b


def _rms_norm(x, g):
    ms = jnp.mean(x * x, axis=-1, keepdims=True)
    return x * lax.rsqrt(ms + RMS_EPS) * g


def _proj_kernel(x_ref, w_ref, qg_ref, kg_ref, cos_ref, sin_ref, wq_ref, wqr_ref, *rest, absorbed):
    if absorbed:
        (wuk_ref, wqc_ref, wqcr_ref, cos8_ref, sin8_ref,
         u_ref, ckv_ref, kr_ref, gs_ref, gm_ref, ql_ref, qr_ref) = rest
    else:
        wk_ref, wv_ref, u_ref, ckv_ref, kr_ref, gs_ref, gm_ref, q_ref, k_ref, v_ref = rest
    xb = x_ref[...].astype(BF16)
    u_ref[...] = _dot(xb, w_ref[:, 0:512])
    cq = _rms_norm(_dot(xb, w_ref[:, 512:768]), qg_ref[...])
    ckv = _rms_norm(_dot(xb, w_ref[:, 768:1024]), kg_ref[...])
    ckv_ref[...] = ckv
    gs_ref[...] = jax.nn.sigmoid(_dot(xb, w_ref[:, 1024:2048])).astype(BF16)
    gm_ref[...] = jax.nn.sigmoid(_dot(xb, w_ref[:, 2048:3072])).astype(BF16)
    cos = cos_ref[...]
    sin = sin_ref[...]
    kr = _dot(xb, w_ref[:, 3072:3200]) * cos + _dot(xb, w_ref[:, 3200:3328]) * sin
    kr_ref[...] = kr

    cqb = cq.astype(BF16)
    qa = _dot(cqb, wq_ref[...])
    qb = _dot(cqb, wqr_ref[...])
    lane = lax.broadcasted_iota(jnp.int32, cos.shape, 1)
    cq_tab = (cos + jnp.where(lane < QK_NOPE, 1.0, 0.0)) * Q_SCALE
    sq_tab = sin * Q_SCALE
    ckvb = ckv.astype(BF16)
    if absorbed:
        for h in range(N_HEADS):
            sl = slice(h * HEAD_PAD, (h + 1) * HEAD_PAD)
            qh = (qa[:, sl] * cq_tab + qb[:, sl] * sq_tab).astype(BF16)
            ql_ref[:, h * KV_RANK:(h + 1) * KV_RANK] = _dot(qh, wuk_ref[h]).astype(BF16)
        qc = _dot(cqb, wqc_ref[...]) * cos8_ref[...] + _dot(cqb, wqcr_ref[...]) * sin8_ref[...]
        qr_ref[...] = (qc * Q_SCALE).astype(BF16)
    else:
        kn = _dot(ckvb, wk_ref[...])
        vv = _dot(ckvb, wv_ref[...])
        for h in range(N_HEADS):
            sl = slice(h * HEAD_PAD, (h + 1) * HEAD_PAD)
            q_ref[h] = (qa[:, sl] * cq_tab + qb[:, sl] * sq_tab).astype(BF16)
            k_ref[h] = (kn[:, sl] + kr).astype(BF16)
            v_ref[h] = jnp.where(lane == V_DIM, 1.0, vv[:, sl]).astype(BF16)


def _proj(x, wts, tabs, *, tm, seq_tiles, absorbed):
    T = x.shape[0]
    nt = T // tm
    row = lambda i: (i, 0)
    if seq_tiles is None:
        tab_map = lambda i: (0, 0)
    else:
        tab_map = lambda i: (i % seq_tiles, 0)
    u_shape, u_spec = (T, SSM_WIDTH), pl.BlockSpec((tm, SSM_WIDTH), row)
    in_specs = [
        pl.BlockSpec((tm, D_MODEL), row),
        _const_spec(wts["w_in"].shape),
        _const_spec((1, Q_RANK)), _const_spec((1, KV_RANK)),
        pl.BlockSpec((tm, HEAD_PAD), tab_map), pl.BlockSpec((tm, HEAD_PAD), tab_map),
        _const_spec(wts["wq"].shape), _const_spec(wts["wq_rot"].shape),
    ]
    args = [x, wts["w_in"], wts["q_norm_g"], wts["kv_norm_g"], tabs["cos128"], tabs["sin128"],
            wts["wq"], wts["wq_rot"]]
    out_shape = [jax.ShapeDtypeStruct(u_shape, F32),
                 jax.ShapeDtypeStruct((T, KV_RANK), F32),
                 jax.ShapeDtypeStruct((T, HEAD_PAD), F32),
                 jax.ShapeDtypeStruct((T, D_MODEL), BF16),
                 jax.ShapeDtypeStruct((T, D_MODEL), BF16)]
    out_specs = [u_spec, pl.BlockSpec((tm, KV_RANK), row), pl.BlockSpec((tm, HEAD_PAD), row),
                 pl.BlockSpec((tm, D_MODEL), row), pl.BlockSpec((tm, D_MODEL), row)]
    if absorbed:
        in_specs += [_const_spec(wts["wuk_abs"].shape), _const_spec(wts["wq_rope"].shape),
                     _const_spec(wts["wq_rope_rot"].shape),
                     pl.BlockSpec((tm, N_HEADS * QK_ROPE), tab_map),
                     pl.BlockSpec((tm, N_HEADS * QK_ROPE), tab_map)]
        args += [wts["wuk_abs"], wts["wq_rope"], wts["wq_rope_rot"], tabs["cos_heads"], tabs["sin_heads"]]
        out_shape += [jax.ShapeDtypeStruct((T, N_HEADS * KV_RANK), BF16),
                      jax.ShapeDtypeStruct((T, N_HEADS * QK_ROPE), BF16)]
        out_specs += [pl.BlockSpec((tm, N_HEADS * KV_RANK), row), pl.BlockSpec((tm, N_HEADS * QK_ROPE), row)]
    else:
        in_specs += [_const_spec(wts["wk"].shape), _const_spec(wts["wv"].shape)]
        args += [wts["wk"], wts["wv"]]
        head = jax.ShapeDtypeStruct((N_HEADS, T, HEAD_PAD), BF16)
        head_spec = pl.BlockSpec((N_HEADS, tm, HEAD_PAD), lambda i: (0, i, 0))
        out_shape += [head, head, head]
        out_specs += [head_spec, head_spec, head_spec]
    return pl.pallas_call(
        functools.partial(_proj_kernel, absorbed=absorbed),
        grid=(nt,), in_specs=in_specs, out_specs=out_specs, out_shape=out_shape,
        compiler_params=_params(("parallel",)),
        name="proj_sample" if absorbed else "proj_prompt",
    )(*args)


def _s5_kernel(u_ref, s0_ref, lam_ref, wb_ref, wc_ref, dskip_ref, wglu_ref, bglu_ref,
               y_ref, sfin_ref, bu_ref, st_ref, tr_ref, *, lt, bb):
    ti = pl.program_id(1)
    rows = lt * bb
    n_tiles = STATE_COLS // MXU_DIM
    n_slabs = SSM_WIDTH // LANES

    @pl.when(ti == 0)
    def _():
        st_ref[...] = s0_ref[...]

    for b in range(bb):
        for g in range(n_slabs):
            tr_ref[g, pl.ds(b, lt, stride=bb), :] = u_ref[b, :, g * LANES:(g + 1) * LANES]
    u = jnp.concatenate([tr_ref[g] for g in range(n_slabs)], axis=1)
    ub = u.astype(BF16)
    for j in range(n_tiles):
        k0 = MXU_DIM * (j // (n_tiles // 2))
        bu_ref[:, j * MXU_DIM:(j + 1) * MXU_DIM] = _dot(ub[:, k0:k0 + MXU_DIM], wb_ref[j])

    jg = max(1, 64 // bb)
    for j0 in range(0, n_tiles, jg):
        lrs = [jnp.broadcast_to(lam_ref[0:1, (j0 + q) * LANES:(j0 + q + 1) * LANES], (bb, LANES)) for q in range(jg)]
        lis = [jnp.broadcast_to(lam_ref[1:2, (j0 + q) * LANES:(j0 + q + 1) * LANES], (bb, LANES)) for q in range(jg)]

        def body(t, carry, j0=j0, lrs=lrs, lis=lis):
            r0 = pl.multiple_of(t * bb, bb)
            new = []
            for q in range(jg):
                c0 = (j0 + q) * MXU_DIM
                sr, si = carry[2 * q], carry[2 * q + 1]
                nr = lrs[q] * sr - lis[q] * si + bu_ref[pl.ds(r0, bb), c0:c0 + LANES]
                ni = lrs[q] * si + lis[q] * sr + bu_ref[pl.ds(r0, bb), c0 + LANES:c0 + 2 * LANES]
                bu_ref[pl.ds(r0, bb), c0:c0 + LANES] = nr
                bu_ref[pl.ds(r0, bb), c0 + LANES:c0 + 2 * LANES] = ni
                new += [nr, ni]
            return tuple(new)

        init = []
        for q in range(jg):
            c0 = (j0 + q) * MXU_DIM
            init += [st_ref[:, c0:c0 + LANES], st_ref[:, c0 + LANES:c0 + 2 * LANES]]
        fin = lax.fori_loop(0, lt, body, tuple(init), unroll=min(lt, 8))
        for q in range(jg):
            c0 = (j0 + q) * MXU_DIM
            st_ref[:, c0:c0 + LANES] = fin[2 * q]
            st_ref[:, c0 + LANES:c0 + 2 * LANES] = fin[2 * q + 1]

    half = STATE_COLS // 2
    y = jnp.concatenate([_dot(bu_ref[:, 0:half].astype(BF16), wc_ref[0]),
                         _dot(bu_ref[:, half:STATE_COLS].astype(BF16), wc_ref[1])], axis=1)
    y = y + dskip_ref[...] * u
    z = jax.nn.gelu(y, approximate=True)
    gate = jax.nn.sigmoid(_dot(z.astype(BF16), wglu_ref[...]) + bglu_ref[...])
    out = z * gate
    for g in range(n_slabs):
        tr_ref[g] = out[:, g * LANES:(g + 1) * LANES]
    for b in range(bb):
        y_ref[b] = jnp.concatenate([tr_ref[g, pl.ds(b, lt, stride=bb), :] for g in range(n_slabs)],
                                   axis=1).astype(BF16)

    @pl.when(ti == pl.num_programs(1) - 1)
    def _():
        sfin_ref[...] = st_ref[...]


def _s5(u3, s0, wts, *, lt, bb, name):
    B, S, _ = u3.shape
    return pl.pallas_call(
        functools.partial(_s5_kernel, lt=lt, bb=bb),
        grid=(B // bb, S // lt),
        in_specs=[pl.BlockSpec((bb, lt, SSM_WIDTH), lambda b, t: (b, t, 0)),
                  pl.BlockSpec((bb, STATE_COLS), lambda b, t: (b, 0)),
                  _const_spec(wts["lam"].shape), _const_spec(wts["s5_wb"].shape),
                  _const_spec(wts["s5_wc"].shape), _const_spec((1, SSM_WIDTH)),
                  _const_spec(wts["w_glu"].shape), _const_spec((1, SSM_WIDTH))],
        out_specs=[pl.BlockSpec((bb, lt, SSM_WIDTH), lambda b, t: (b, t, 0)),
                   pl.BlockSpec((bb, STATE_COLS), lambda b, t: (b, 0))],
        out_shape=[jax.ShapeDtypeStruct((B, S, SSM_WIDTH), BF16),
                   jax.ShapeDtypeStruct((B, STATE_COLS), F32)],
        scratch_shapes=[pltpu.VMEM((lt * bb, STATE_COLS), F32), pltpu.VMEM((bb, STATE_COLS), F32),
                        pltpu.VMEM((SSM_WIDTH // LANES, lt * bb, LANES), F32)],
        compiler_params=_params(("parallel", "arbitrary")),
        name=name,
    )(u3, s0, wts["lam"], wts["s5_wb"], wts["s5_wc"], wts["d_skip"], wts["w_glu"], wts["b_glu"])


def _attn_prompt_kernel(q_ref, k_ref, v_ref, o_ref, *, tq, tk):
    qi = pl.program_id(1)
    n_diag = tq // tk
    row = lax.broadcasted_iota(jnp.int32, (tq, tk), 0)
    col = lax.broadcasted_iota(jnp.int32, (tq, tk), 1)
    n_full = qi * n_diag
    lane = lax.broadcasted_iota(jnp.int32, (tq, HEAD_PAD), 1)
    pairs = []
    for h0 in range(0, N_HEADS, ATTN_HEADS_PER_BODY):
        heads = tuple(range(h0, h0 + ATTN_HEADS_PER_BODY))
        qs = [q_ref[h] for h in heads]

        def step(kt, carry, diag=None, heads=heads, qs=qs):
            k0 = pl.multiple_of(kt * tk, tk)
            new = []
            for j, h in enumerate(heads):
                m, acc = carry[2 * j], carry[2 * j + 1]
                s = _dot_t(qs[j], k_ref[h, pl.ds(k0, tk), :])
                if diag is not None:
                    s = jnp.where(col + diag * tk <= row, s, NEG_INF)
                m_new = jnp.maximum(m, jnp.max(s, axis=1, keepdims=True))
                p = jnp.exp2(s - m_new).astype(BF16)
                acc = jnp.exp2(m - m_new) * acc + _dot(p, v_ref[h, pl.ds(k0, tk), :])
                new += [m_new, acc]
            return tuple(new)

        carry = (jnp.full((tq, 1), NEG_INF, F32), jnp.zeros((tq, HEAD_PAD), F32)) * ATTN_HEADS_PER_BODY
        carry = lax.fori_loop(0, n_full, step, carry)
        for d in range(n_diag):
            carry = step(n_full + d, carry, diag=d)
        outs = [jnp.where(lane < V_DIM, acc / acc[:, V_DIM:V_DIM + 1], 0.0) for acc in carry[1::2]]
        pairs += [outs[j] + pltpu.roll(outs[j + 1], V_DIM, axis=1) for j in range(0, ATTN_HEADS_PER_BODY, 2)]
    o_ref[...] = jnp.concatenate(pairs, axis=1).astype(BF16)


def _attn_prompt(q3, k3, v3, *, batch, seq, tq, tk):
    nq = seq // tq
    T = batch * seq
    return pl.pallas_call(
        functools.partial(_attn_prompt_kernel, tq=tq, tk=tk),
        grid=(batch, nq),
        in_specs=[pl.BlockSpec((N_HEADS, tq, HEAD_PAD), lambda b, i: (0, b * nq + i, 0)),
                  pl.BlockSpec((N_HEADS, seq, HEAD_PAD), lambda b, i: (0, b, 0)),
                  pl.BlockSpec((N_HEADS, seq, HEAD_PAD), lambda b, i: (0, b, 0))],
        out_specs=pl.BlockSpec((tq, N_HEADS * V_DIM), lambda b, i: (b * nq + i, 0)),
        out_shape=jax.ShapeDtypeStruct((T, N_HEADS * V_DIM), BF16),
        compiler_params=_params(("parallel", "arbitrary")),
        name="attn_prompt",
    )(q3, k3, v3)


def _attn_sample_kernel(pt_ref, ql_ref, qr_ref, qlp_ref, qrp_ref, cn_ref, krn_ref, kv_hbm, kr_hbm, o_ref,
                        kvbuf, krbuf, kv_sem, kr_sem, kvb_ref, s_ref, m_ref, l_ref, acc_ref,
                        *, cp, n_chunks, n_steps, n_new):
    step = pl.program_id(0)
    slot = lax.rem(step, 2)

    def page_copies(chunk, sl):
        b = lax.div(chunk, n_chunks)
        c = lax.rem(chunk, n_chunks)
        out = []
        for k in range(cp):
            page = pt_ref[b, c * cp + k]
            out.append(pltpu.make_async_copy(kv_hbm.at[page], kvbuf.at[sl, k], kv_sem.at[sl]))
            out.append(pltpu.make_async_copy(kr_hbm.at[page], krbuf.at[sl, k], kr_sem.at[sl]))
        return out

    @pl.when(step == 0)
    def _():
        for d in page_copies(0, 0):
            d.start()

    @pl.when(step + 1 < n_steps)
    def _():
        for d in page_copies(step + 1, 1 - slot):
            d.start()

    @pl.when(step < n_steps)
    def _():
        for d in page_copies(step, slot):
            d.wait()

    def reset_state():
        m_ref[...] = jnp.full(m_ref.shape, NEG_INF, F32)
        l_ref[...] = jnp.zeros(l_ref.shape, F32)
        acc_ref[...] = jnp.zeros(acc_ref.shape, F32)

    @pl.when(step == 0)
    def _():
        reset_state()
        kvb_ref[1] = jnp.zeros(kvb_ref.shape[1:], BF16)
        s_ref[1] = jnp.zeros(s_ref.shape[1:], F32)

    ql = ql_ref[...]
    qr = qr_ref[...]

    def update(state, s, vals):
        m, l, acc = state
        m_new = jnp.maximum(m, jnp.max(s, axis=1, keepdims=True))
        alpha = jnp.exp2(m - m_new)
        p = jnp.exp2(s - m_new)
        return (m_new, alpha * l + jnp.sum(p, axis=1, keepdims=True),
                alpha * acc + _dot(p.astype(BF16), vals))

    def fold(sl):
        state = update((m_ref[...], l_ref[...], acc_ref[...]), s_ref[sl], kvb_ref[sl])
        m_ref[...], l_ref[...], acc_ref[...] = state
        return state

    def body(sl):
        kv = kvbuf[sl].reshape(cp * PAGE_SIZE, KV_RANK).astype(BF16)
        kr_t = jnp.concatenate([krbuf[sl, k] for k in range(cp)], axis=1).astype(BF16)
        kvb_ref[sl] = kv
        s_ref[sl] = _dot_t(ql, kv) + _dot(qr, kr_t)
        fold(1 - sl)

    for sl in range(2):
        pl.when(slot == sl)(functools.partial(body, sl))

    @pl.when(lax.rem(step, n_chunks) == 0)
    def _():
        @pl.when(step > 0)
        def _():
            qlp = qlp_ref[...]
            qrp = qrp_ref[...]
            cn = cn_ref[...].astype(BF16)
            krn = krn_ref[...].astype(BF16)
            s = _dot_t(qlp, cn) + _dot_t(qrp, krn)
            rows = qlp.shape[0]
            t_row = jnp.right_shift(lax.broadcasted_iota(jnp.int32, (rows, n_new), 0), 3)
            t_col = lax.broadcasted_iota(jnp.int32, (rows, n_new), 1)
            state = (m_ref[...], l_ref[...], acc_ref[...])
            _, l, acc = update(state, jnp.where(t_col <= t_row, s, NEG_INF), cn)
            o_ref[...] = (acc / l).astype(BF16)

        reset_state()


def _attn_sample(page_table, q_lat, q_rope, ckv_new, kr_new, cache_kv, cache_kr_t, *, batch, n_new, cp):
    rows = n_new * N_HEADS
    n_chunks = page_table.shape[1] // cp
    n_steps = batch * n_chunks
    cur = lambda t, pt: (jnp.minimum(t, n_steps - 1) // n_chunks, 0)
    prev = lambda t, pt: (jnp.maximum(t - 1, 0) // n_chunks, 0)
    in_specs = [pl.BlockSpec((rows, KV_RANK), cur), pl.BlockSpec((rows, QK_ROPE), cur),
                pl.BlockSpec((rows, KV_RANK), prev), pl.BlockSpec((rows, QK_ROPE), prev),
                pl.BlockSpec((n_new, KV_RANK), prev), pl.BlockSpec((n_new, QK_ROPE), prev),
                pl.BlockSpec(memory_space=pl.ANY), pl.BlockSpec(memory_space=pl.ANY)]
    grid_spec = pltpu.PrefetchScalarGridSpec(
        num_scalar_prefetch=1, grid=(n_steps + 1,), in_specs=in_specs,
        out_specs=pl.BlockSpec((rows, KV_RANK), prev),
        scratch_shapes=[pltpu.VMEM((2, cp, PAGE_SIZE, KV_RANK), F32), pltpu.VMEM((2, cp, QK_ROPE, PAGE_SIZE), F32),
                        pltpu.SemaphoreType.DMA((2,)), pltpu.SemaphoreType.DMA((2,)),
                        pltpu.VMEM((2, cp * PAGE_SIZE, KV_RANK), BF16), pltpu.VMEM((2, rows, cp * PAGE_SIZE), F32),
                        pltpu.VMEM((rows, 1), F32), pltpu.VMEM((rows, 1), F32), pltpu.VMEM((rows, KV_RANK), F32)])
    return pl.pallas_call(
        functools.partial(_attn_sample_kernel, cp=cp, n_chunks=n_chunks, n_steps=n_steps, n_new=n_new),
        grid_spec=grid_spec,
        out_shape=jax.ShapeDtypeStruct((batch * rows, KV_RANK), BF16),
        compiler_params=_params(("arbitrary",)),
        name="attn_sample",
    )(page_table, q_lat, q_rope, q_lat, q_rope, ckv_new, kr_new, cache_kv, cache_kr_t)


def _mm_kernel(x_ref, w_ref, o_ref):
    o_ref[...] = _dot(x_ref[...], w_ref[...]).astype(o_ref.dtype)


def _mm(x, w, *, tm, out_dtype, name):
    M, K = x.shape
    N = w.shape[1]
    return pl.pallas_call(
        _mm_kernel, grid=(M // tm,),
        in_specs=[pl.BlockSpec((tm, K), lambda i: (i, 0)), _const_spec((K, N))],
        out_specs=pl.BlockSpec((tm, N), lambda i: (i, 0)),
        out_shape=jax.ShapeDtypeStruct((M, N), out_dtype),
        compiler_params=_params(("parallel",)), name=name,
    )(x, w)


def _merge_kernel(ssm_ref, mla_ref, gs_ref, gm_ref, x_ref, wbs_ref, wbm_ref, wo_ref, g_ref, b_ref,
                  wr_ref, br_ref, x1_ref, route_ref, cnt_ref, run_ref, *, tm):
    i = pl.program_id(0)

    @pl.when(i == 0)
    def _():
        run_ref[...] = jnp.zeros(run_ref.shape, F32)

    a = _dot(ssm_ref[...], wbs_ref[...])
    m = _dot(mla_ref[...], wbm_ref[...])
    merged = gs_ref[...].astype(F32) * a + gm_ref[...].astype(F32) * m
    y = _dot(merged.astype(BF16), wo_ref[...])
    x1 = _layer_norm(ALPHA * x_ref[...] + y, g_ref[...], b_ref[...])
    x1_ref[...] = x1

    logits = _dot(x1.astype(BF16), wr_ref[...]) + br_ref[...]
    lane = lax.broadcasted_iota(jnp.int32, (tm, LANES), 1).astype(F32)
    gl = jnp.where(lane < N_GROUPS, logits, NEG_INF)
    gmax = jnp.max(gl, axis=1, keepdims=True)
    g_sel = jnp.min(jnp.where(gl == gmax, lane, float(LANES)), axis=1, keepdims=True)
    onehot = jnp.where(lane == g_sel, 1.0, 0.0)
    r_i = lax.broadcasted_iota(jnp.int32, (tm, tm), 0)
    c_i = lax.broadcasted_iota(jnp.int32, (tm, tm), 1)
    before = jnp.where(c_i < r_i, 1.0, 0.0).astype(BF16)
    rank_all = _dot(before, onehot.astype(BF16)) + run_ref[0:1, :]
    rank = jnp.sum(onehot * rank_all, axis=1, keepdims=True)
    route_ref[...] = jnp.where(lane == 0.0, g_sel, jnp.where(lane == 1.0, rank, 0.0))
    run_ref[0:1, :] = run_ref[0:1, :] + jnp.sum(onehot, axis=0, keepdims=True)

    @pl.when(i == pl.num_programs(0) - 1)
    def _():
        cnt_ref[...] = run_ref[...]


def _merge(ssm, mla, gs, gm, x, wts, *, tm):
    T = x.shape[0]
    row = lambda i: (i, 0)
    ssm_spec = pl.BlockSpec((tm, SSM_WIDTH), row)
    return pl.pallas_call(
        functools.partial(_merge_kernel, tm=tm),
        grid=(T // tm,),
        in_specs=[ssm_spec, pl.BlockSpec((tm, N_HEADS * V_DIM), row),
                  pl.BlockSpec((tm, D_MODEL), row), pl.BlockSpec((tm, D_MODEL), row),
                  pl.BlockSpec((tm, D_MODEL), row),
                  _const_spec(wts["w_br_ssm"].shape), _const_spec(wts["w_br_mla"].shape),
                  _const_spec(wts["w_out"].shape), _const_spec((1, D_MODEL)), _const_spec((1, D_MODEL)),
                  _const_spec((D_MODEL, LANES)), _const_spec((1, LANES))],
        out_specs=[pl.BlockSpec((tm, D_MODEL), row), pl.BlockSpec((tm, LANES), row),
                   _const_spec((8, LANES))],
        out_shape=[jax.ShapeDtypeStruct((T, D_MODEL), F32), jax.ShapeDtypeStruct((T, LANES), F32),
                   jax.ShapeDtypeStruct((8, LANES), F32)],
        scratch_shapes=[pltpu.VMEM((8, LANES), F32)],
        compiler_params=_params(("arbitrary",)),
        name="merge_ln1_route",
    )(ssm, mla, gs, gm, x, wts["w_br_ssm"], wts["w_br_mla"], wts["w_out"], wts["ln1_g"], wts["ln1_b"],
      wts["wrg"], wts["brg"])


def _dispatch_kernel(pad_ref, x_ref, dest_ref, xs_ref, zero_ref, sem, *, tm):
    i = pl.program_id(0)

    def row_copy(r, d):
        return pltpu.make_async_copy(x_ref.at[pl.ds(r, 1)], xs_ref.at[pl.ds(d, 1)], sem)

    def zero_copy(d):
        return pltpu.make_async_copy(zero_ref.at[pl.ds(0, 1)], xs_ref.at[pl.ds(d, 1)], sem)

    @pl.when(i == 0)
    def _():
        zero_ref[...] = jnp.zeros(zero_ref.shape, F32)
        for g in range(N_GROUPS + 1):
            start = pad_ref[g]
            n = pad_ref[N_GROUPS + 1 + g]

            def issue(r, c, start=start):
                zero_copy(start + r).start()
                return c

            def drain(r, c, start=start):
                zero_copy(start + r).wait()
                return c

            lax.fori_loop(0, n, issue, 0)
            lax.fori_loop(0, n, drain, 0)

    def issue(r8, c):
        for k in range(ROW_DMA_UNROLL):
            r = r8 * ROW_DMA_UNROLL + k
            row_copy(r, dest_ref[0, 0, r]).start(priority=k % 2)
        return c

    lax.fori_loop(0, tm // ROW_DMA_UNROLL, issue, 0)
    pltpu.make_async_copy(x_ref, xs_ref.at[pl.ds(0, tm)], sem).wait()


def _dispatch(x1, dest, pad_info, *, tm, n_rows):
    T = x1.shape[0]
    nt = T // tm
    grid_spec = pltpu.PrefetchScalarGridSpec(
        num_scalar_prefetch=1, grid=(nt,),
        in_specs=[pl.BlockSpec((tm, D_MODEL), lambda i, pad: (i, 0)),
                  pl.BlockSpec((1, 1, tm), lambda i, pad: (i, 0, 0), memory_space=pltpu.SMEM)],
        out_specs=pl.BlockSpec(memory_space=pl.ANY),
        scratch_shapes=[pltpu.VMEM((8, D_MODEL), F32), pltpu.SemaphoreType.DMA(())])
    return pl.pallas_call(
        functools.partial(_dispatch_kernel, tm=tm),
        grid_spec=grid_spec,
        out_shape=jax.ShapeDtypeStruct((n_rows, D_MODEL), F32),
        compiler_params=_params(("arbitrary",)),
        name="moe_dispatch",
    )(pad_info, x1, dest.reshape(nt, 1, tm))


def _moe_kernel(grp_ref, blk_ref, nv_ref, x_ref, wr_ref, br_ref, wg_ref, wu_ref, wd_ref, o_ref, *, tm):
    del blk_ref
    i = pl.program_id(0)

    @pl.when(i < nv_ref[0])
    def _():
        g = grp_ref[i]
        x = x_ref[...]
        xb = x.astype(BF16)
        logits = _dot(xb, wr_ref[0]) + br_ref[0]
        lane = lax.broadcasted_iota(jnp.int32, (tm, LANES), 1).astype(F32)
        is_grp = (lane >= EXPERTS_PER_GROUP) & (lane < EXPERTS_PER_GROUP + N_GROUPS)
        gl = jnp.where(is_grp, logits, NEG_INF)
        gmax = jnp.max(gl, axis=1, keepdims=True)
        gexp = jnp.exp(gl - gmax)
        p_group = (jnp.sum(jnp.where(lane == (EXPERTS_PER_GROUP + g).astype(F32), gexp, 0.0), axis=1, keepdims=True)
                   / jnp.sum(gexp, axis=1, keepdims=True))
        el = jnp.where(lane < EXPERTS_PER_GROUP, logits, NEG_INF)
        v1 = jnp.max(el, axis=1, keepdims=True)
        i1 = jnp.min(jnp.where(el == v1, lane, float(LANES)), axis=1, keepdims=True)
        el2 = jnp.where(lane == i1, NEG_INF, el)
        v2 = jnp.max(el2, axis=1, keepdims=True)
        i2 = jnp.min(jnp.where(el2 == v2, lane, float(LANES)), axis=1, keepdims=True)
        e2 = jnp.exp(v2 - v1)
        w1 = p_group / (1.0 + e2)
        w2 = w1 * e2
        gates = jnp.where(lane == i1, w1, jnp.where(lane == i2, w2, 0.0))

        hs = []
        for e in range(EXPERTS_PER_GROUP):
            hg = _dot(xb, wg_ref[0, e])
            hu = _dot(xb, wu_ref[0, e])
            ge = jnp.sum(jnp.where(lane == float(e), gates, 0.0), axis=1, keepdims=True)
            hs.append((jax.nn.silu(hg) * hu * ge).astype(BF16))
        o_ref[...] = _dot(jnp.concatenate(hs, axis=1), wd_ref[0])

    @pl.when(i >= nv_ref[0])
    def _():
        o_ref[...] = jnp.zeros(o_ref.shape, F32)


def _moe(xs, tile_grp, tile_blk, n_valid, wts, *, tm):
    n_rows = xs.shape[0]
    nt = n_rows // tm
    grp3 = lambda i, grp, blk, nv: (grp[i], 0, 0)
    grp4 = lambda i, grp, blk, nv: (grp[i], 0, 0, 0)
    rows = lambda i, grp, blk, nv: (blk[i], 0)
    grid_spec = pltpu.PrefetchScalarGridSpec(
        num_scalar_prefetch=3, grid=(nt,),
        in_specs=[pl.BlockSpec((tm, D_MODEL), rows),
                  pl.BlockSpec((1, D_MODEL, LANES), grp3), pl.BlockSpec((1, 1, LANES), grp3),
                  pl.BlockSpec((1, EXPERTS_PER_GROUP, D_MODEL, D_EXPERT), grp4),
                  pl.BlockSpec((1, EXPERTS_PER_GROUP, D_MODEL, D_EXPERT), grp4),
                  pl.BlockSpec((1, EXPERTS_PER_GROUP * D_EXPERT, D_MODEL), grp3)],
        out_specs=pl.BlockSpec((tm, D_MODEL), lambda i, grp, blk, nv: (i, 0)))
    return pl.pallas_call(
        functools.partial(_moe_kernel, tm=tm),
        grid_spec=grid_spec,
        out_shape=jax.ShapeDtypeStruct((n_rows, D_MODEL), F32),
        compiler_params=_params(("arbitrary",)),
        name="moe_experts",
    )(tile_grp, tile_blk, n_valid, xs, wts["wroute"], wts["broute"],
      wts["w_e_gate"], wts["w_e_up"], wts["w_e_down"])


def _final_kernel(x1_ref, dest_ref, dest_next_ref, ys_ref, p_ref, g_ref, b_ref, wpg_ref, bpg_ref, wpp_ref, o_ref,
                  moe_ref, sem, *, tm):
    i = pl.program_id(0)
    slot = lax.rem(i, 2)

    def gather(d_ref, sl):
        def issue(r8, c):
            for k in range(ROW_DMA_UNROLL):
                r = r8 * ROW_DMA_UNROLL + k
                pltpu.make_async_copy(ys_ref.at[pl.ds(d_ref[0, 0, r], 1)], moe_ref.at[sl, pl.ds(r, 1)],
                                      sem.at[sl]).start(priority=k % 2)
            return c

        lax.fori_loop(0, tm // ROW_DMA_UNROLL, issue, 0)

    @pl.when(i == 0)
    def _():
        gather(dest_ref, 0)

    @pl.when(i + 1 < pl.num_programs(0))
    def _():
        gather(dest_next_ref, 1 - slot)

    pltpu.make_async_copy(ys_ref.at[pl.ds(0, tm)], moe_ref.at[slot], sem.at[slot]).wait()
    x2 = _layer_norm(ALPHA * x1_ref[...] + moe_ref[slot], g_ref[...], b_ref[...])
    gate = jax.nn.sigmoid(_dot(x2.astype(BF16), wpg_ref[...]) + bpg_ref[...])
    o_ref[...] = x2 + gate * _dot(p_ref[...].astype(BF16), wpp_ref[...])


def _final(x1, dest, ys, p, wts, *, tm):
    T = x1.shape[0]
    nt = T // tm
    row = lambda i: (i, 0)
    return pl.pallas_call(
        functools.partial(_final_kernel, tm=tm),
        grid=(nt,),
        in_specs=[pl.BlockSpec((tm, D_MODEL), row),
                  pl.BlockSpec((1, 1, tm), lambda i: (i, 0, 0), memory_space=pltpu.SMEM),
                  pl.BlockSpec((1, 1, tm), lambda i: (jnp.minimum(i + 1, nt - 1), 0, 0), memory_space=pltpu.SMEM),
                  pl.BlockSpec(memory_space=pl.ANY),
                  pl.BlockSpec((tm, PLE_DIM), row),
                  _const_spec((1, D_MODEL)), _const_spec((1, D_MODEL)),
                  _const_spec((D_MODEL, D_MODEL)), _const_spec((1, D_MODEL)), _const_spec((PLE_DIM, D_MODEL))],
        out_specs=pl.BlockSpec((tm, D_MODEL), row),
        out_shape=jax.ShapeDtypeStruct((T, D_MODEL), F32),
        scratch_shapes=[pltpu.VMEM((2, tm, D_MODEL), F32), pltpu.SemaphoreType.DMA((2,))],
        compiler_params=_params(("arbitrary",)),
        name="combine_ln2_ple",
    )(x1, dest.reshape(nt, 1, tm), dest.reshape(nt, 1, tm), ys, p, wts["ln2_g"], wts["ln2_b"], wts["w_ple_gate"], wts["b_ple_gate"],
      wts["w_ple_proj"])


def _prep_weights(w_in, a_re, a_im, log_dt, b_re, b_im, c_re, c_im, d_skip, w_glu, b_glu, w_br_ssm,
                  q_norm_g, w_uq, kv_norm_g, w_uk, w_uv, w_br_mla, w_out, ln1_g, ln1_b,
                  w_gr, b_gr, w_er, b_er, w_e_gate, w_e_up, w_e_down, ln2_g, ln2_b,
                  w_ple_gate, b_ple_gate, w_ple_proj):
    half = QK_ROPE // 2
    rot = lambda w: jnp.concatenate([-w[..., half:], w[..., :half]], axis=-1)
    w = {}

    u_w, cq_w, ckv_w, kr_w, gs_w, gm_w = jnp.split(
        w_in, (512, 768, 1024, 1024 + QK_ROPE, 1024 + QK_ROPE + D_MODEL), axis=1)
    slab = lambda c: jnp.pad(c, ((0, 0), (QK_NOPE, HEAD_PAD - QK_NOPE - QK_ROPE)))
    w["w_in"] = jnp.concatenate([u_w, cq_w, ckv_w, gs_w, gm_w, slab(kr_w), slab(rot(kr_w))], axis=1).astype(BF16)
    w["q_norm_g"] = q_norm_g.reshape(1, Q_RANK)
    w["kv_norm_g"] = kv_norm_g.reshape(1, KV_RANK)

    wq3 = w_uq.reshape(Q_RANK, N_HEADS, QK_NOPE + QK_ROPE)
    nope, rope = wq3[..., :QK_NOPE], wq3[..., QK_NOPE:]
    pad_tail = jnp.zeros((Q_RANK, N_HEADS, HEAD_PAD - QK_NOPE - QK_ROPE), F32)
    w["wq"] = jnp.concatenate([nope, rope, pad_tail], -1).reshape(Q_RANK, N_HEADS * HEAD_PAD).astype(BF16)
    w["wq_rot"] = jnp.concatenate([jnp.zeros_like(nope), rot(rope), pad_tail], -1).reshape(
        Q_RANK, N_HEADS * HEAD_PAD).astype(BF16)
    w["wq_rope"] = rope.reshape(Q_RANK, N_HEADS * QK_ROPE).astype(BF16)
    w["wq_rope_rot"] = rot(rope).reshape(Q_RANK, N_HEADS * QK_ROPE).astype(BF16)
    head_pad = lambda a: jnp.pad(a, ((0, 0), (0, 0), (0, HEAD_PAD - a.shape[-1]))).reshape(
        a.shape[0], N_HEADS * HEAD_PAD).astype(BF16)
    w["wk"] = head_pad(w_uk)
    w["wv"] = head_pad(w_uv)
    w["wuk_abs"] = jnp.pad(w_uk.transpose(1, 2, 0), ((0, 0), (0, HEAD_PAD - QK_NOPE), (0, 0))).astype(BF16)
    w["wuv_blockdiag"] = (jnp.eye(N_HEADS, dtype=F32)[:, None, :, None]
                          * w_uv.transpose(1, 0, 2)[:, :, None, :]).reshape(
                              N_HEADS * KV_RANK, N_HEADS * V_DIM).astype(BF16)

    dt = jnp.exp(log_dt)[:, None]
    mag = jnp.exp(dt * a_re)
    lr = mag * jnp.cos(dt * a_im)
    li = mag * jnp.sin(dt * a_im)
    den = a_re * a_re + a_im * a_im
    fr = ((lr - 1.0) * a_re + li * a_im) / den
    fi = (li * a_re - (lr - 1.0) * a_im) / den
    bbr = fr[..., None] * b_re - fi[..., None] * b_im
    bbi = fr[..., None] * b_im + fi[..., None] * b_re
    w["lam"] = jnp.stack([lr.reshape(-1), li.reshape(-1)])
    n_tiles = SSM_GROUPS // 2
    col_group = jnp.broadcast_to((2 * jnp.arange(n_tiles)[:, None, None, None]
                                  + jnp.arange(2)[None, None, :, None]), (n_tiles, 2, 2, SSM_STATE)).reshape(-1)
    ch_group = jnp.arange(SSM_WIDTH) // SSM_GROUP
    mask = ch_group[:, None] == col_group[None, :]
    bcols = jnp.stack([bbr, bbi]).reshape(2, n_tiles, 2, SSM_STATE, SSM_GROUP).transpose(
        1, 0, 2, 3, 4).reshape(STATE_COLS, SSM_GROUP)
    bmat = jnp.where(mask, jnp.tile(bcols.T, (SSM_GROUPS, 1)), 0.0)
    bmat4 = bmat.reshape(2, MXU_DIM, n_tiles, MXU_DIM)
    w["s5_wb"] = jnp.stack([bmat4[j // (n_tiles // 2), :, j, :] for j in range(n_tiles)]).astype(BF16)
    ccols = jnp.stack([c_re, -c_im]).reshape(2, n_tiles, 2, SSM_GROUP, SSM_STATE).transpose(
        1, 0, 2, 4, 3).reshape(STATE_COLS, SSM_GROUP)
    cmat = jnp.where(mask.T, jnp.tile(ccols, (1, SSM_GROUPS)), 0.0)
    hc = STATE_COLS // 2
    w["s5_wc"] = jnp.stack([cmat[:hc, :MXU_DIM], cmat[hc:, MXU_DIM:]]).astype(BF16)
    w["d_skip"] = d_skip.reshape(1, SSM_WIDTH)
    w["w_glu"] = w_glu.astype(BF16)
    w["b_glu"] = b_glu.reshape(1, SSM_WIDTH)

    w["w_br_ssm"] = w_br_ssm.astype(BF16)
    w["w_br_mla"] = w_br_mla.astype(BF16)
    w["w_out"] = w_out.astype(BF16)
    w["ln1_g"], w["ln1_b"] = ln1_g.reshape(1, D_MODEL), ln1_b.reshape(1, D_MODEL)
    w["ln2_g"], w["ln2_b"] = ln2_g.reshape(1, D_MODEL), ln2_b.reshape(1, D_MODEL)

    wrg = jnp.pad(w_gr, ((0, 0), (0, LANES - N_GROUPS)))
    w["wrg"] = wrg.astype(BF16)
    w["brg"] = jnp.pad(b_gr, (0, LANES - N_GROUPS)).reshape(1, LANES)
    er = w_er.reshape(D_MODEL, N_GROUPS, EXPERTS_PER_GROUP).transpose(1, 0, 2)
    gr = jnp.broadcast_to(w_gr[None], (N_GROUPS, D_MODEL, N_GROUPS))
    wroute = jnp.pad(jnp.concatenate([er, gr], -1), ((0, 0), (0, 0), (0, LANES - EXPERTS_PER_GROUP - N_GROUPS)))
    w["wroute"] = wroute.astype(BF16)
    broute = jnp.concatenate([b_er.reshape(N_GROUPS, EXPERTS_PER_GROUP),
                              jnp.broadcast_to(b_gr[None], (N_GROUPS, N_GROUPS))], -1)
    w["broute"] = jnp.pad(broute, ((0, 0), (0, LANES - EXPERTS_PER_GROUP - N_GROUPS))).reshape(N_GROUPS, 1, LANES)
    w["w_e_gate"] = w_e_gate.astype(BF16).reshape(N_GROUPS, EXPERTS_PER_GROUP, D_MODEL, D_EXPERT)
    w["w_e_up"] = w_e_up.astype(BF16).reshape(N_GROUPS, EXPERTS_PER_GROUP, D_MODEL, D_EXPERT)
    w["w_e_down"] = w_e_down.astype(BF16).reshape(N_GROUPS, EXPERTS_PER_GROUP * D_EXPERT, D_MODEL)
    w["w_ple_gate"] = w_ple_gate.astype(BF16)
    w["b_ple_gate"] = b_ple_gate.reshape(1, D_MODEL)
    w["w_ple_proj"] = w_ple_proj.astype(BF16)
    return w


def _rope_tables(pos, reps):
    half = QK_ROPE // 2
    inv = jnp.exp(-math.log(ROPE_THETA) * jnp.arange(half, dtype=F32) / half)
    ang = pos.astype(F32)[:, None] * inv[None, :]
    cos2 = jnp.tile(jnp.cos(ang), (1, 2))
    sin2 = jnp.tile(jnp.sin(ang), (1, 2))
    slab = lambda t: jnp.tile(jnp.pad(t, ((0, 0), (QK_NOPE, HEAD_PAD - QK_NOPE - QK_ROPE))), (reps, 1))
    heads = lambda t: jnp.tile(t, (reps, N_HEADS))
    return {"cos128": slab(cos2), "sin128": slab(sin2), "cos_heads": heads(cos2), "sin_heads": heads(sin2)}


def _pack_state(s_re, s_im):
    B = s_re.shape[0]
    st = jnp.stack([s_re.reshape(B, SSM_GROUPS // 2, 2, SSM_STATE), s_im.reshape(B, SSM_GROUPS // 2, 2, SSM_STATE)],
                   axis=2)
    return st.reshape(B, STATE_COLS)


def _unpack_state(s):
    B = s.shape[0]
    st = s.reshape(B, SSM_GROUPS // 2, 2, 2, SSM_STATE)
    return (st[:, :, 0].reshape(B, SSM_GROUPS, SSM_STATE), st[:, :, 1].reshape(B, SSM_GROUPS, SSM_STATE))


def _route_plan(route, counts, *, tm, n_tok):
    g_sel = route[:, 0].astype(jnp.int32)
    rank = route[:, 1].astype(jnp.int32)
    cnt = counts[0, :N_GROUPS].astype(jnp.int32)
    padded = ((cnt + tm - 1) // tm) * tm
    ends = jnp.cumsum(padded)
    offs = ends - padded
    dest = offs[g_sel] + rank
    n_tiles = n_tok // tm + N_GROUPS
    n_valid = ends[-1] // tm
    tile_start = jnp.arange(n_tiles, dtype=jnp.int32) * tm
    tile_grp = jnp.minimum(jnp.sum(tile_start[:, None] >= ends[None, :], axis=1), N_GROUPS - 1).astype(jnp.int32)
    last = jnp.maximum(n_valid - 1, 0)
    tile_blk = jnp.minimum(jnp.arange(n_tiles, dtype=jnp.int32), last)
    tile_grp = jnp.where(jnp.arange(n_tiles) < n_valid, tile_grp, tile_grp[last])
    n_rows = n_tiles * tm
    pad_info = jnp.concatenate([offs + cnt, ends[-1:], padded - cnt, n_rows - ends[-1:]]).astype(jnp.int32)
    return dest, tile_grp, tile_blk, n_valid.reshape(1).astype(jnp.int32), pad_info, n_rows


def _token_tail(ssm, mla, gs, gm, x, p, wts, *, tm, tm_moe):
    T = x.shape[0]
    x1, route, counts = _merge(ssm, mla, gs, gm, x, wts, tm=tm)
    dest, tile_grp, tile_blk, n_valid, pad_info, n_rows = _route_plan(route, counts, tm=tm_moe, n_tok=T)
    xs = _dispatch(x1, dest, pad_info, tm=tm, n_rows=n_rows)
    ys = _moe(xs, tile_grp, tile_blk, n_valid, wts, tm=tm_moe)
    return _final(x1, dest, ys, p, wts, tm=tm)


def kernel(x_prompt, x_sample, p_prompt, p_sample, cache_kv, cache_k_rope, state_ssm_re, state_ssm_im, page_table, w_in, a_re, a_im, log_dt, b_re, b_im, c_re, c_im, d_skip, w_glu, b_glu, w_br_ssm, q_norm_g, w_uq, kv_norm_g, w_uk, w_uv, w_br_mla, w_out, ln1_g, ln1_b, w_gr, b_gr, w_er, b_er, w_e_gate, w_e_up, w_e_down, ln2_g, ln2_b, w_ple_gate, b_ple_gate, w_ple_proj):
    B, S, _ = x_prompt.shape
    Bs, Ss, _ = x_sample.shape
    Tp, Ts = B * S, Bs * Ss
    wts = _prep_weights(w_in[0], a_re[0], a_im[0], log_dt[0], b_re[0], b_im[0], c_re[0], c_im[0], d_skip[0],
                        w_glu[0], b_glu[0], w_br_ssm[0], q_norm_g[0], w_uq[0], kv_norm_g[0], w_uk[0], w_uv[0],
                        w_br_mla[0], w_out[0], ln1_g[0], ln1_b[0], w_gr[0], b_gr[0], w_er[0], b_er[0],
                        w_e_gate[0], w_e_up[0], w_e_down[0], ln2_g[0], ln2_b[0], w_ple_gate[0], b_ple_gate[0],
                        w_ple_proj[0])
    tm = 256
    tm_moe = 256

    tabs_p = _rope_tables(jnp.arange(S, dtype=jnp.int32), 1)
    seq_tiles = S // tm
    u_p, ckv_p, kr_p, gs_p, gm_p, q3, k3, v3 = _proj(
        x_prompt.reshape(Tp, D_MODEL), wts, tabs_p, tm=tm, seq_tiles=seq_tiles, absorbed=False)
    zero_state = jnp.zeros((B, STATE_COLS), F32)
    ssm_p, sfin_p = _s5(u_p.reshape(B, S, SSM_WIDTH), zero_state, wts, lt=32, bb=B, name="s5_prompt")
    mla_p = _attn_prompt(q3, k3, v3, batch=B, seq=S, tq=512, tk=512)
    y_p = _token_tail(ssm_p.reshape(Tp, SSM_WIDTH), mla_p, gs_p, gm_p, x_prompt.reshape(Tp, D_MODEL),
                      p_prompt.reshape(Tp, PLE_DIM), wts, tm=tm, tm_moe=tm_moe)

    tabs_s = _rope_tables(PAST_LEN + jnp.arange(Ss, dtype=jnp.int32), tm // Ss)
    u_s, ckv_s, kr_s, gs_s, gm_s, q_lat, q_rope = _proj(
        x_sample.reshape(Ts, D_MODEL), wts, tabs_s, tm=tm, seq_tiles=None, absorbed=True)
    s0 = _pack_state(state_ssm_re[0], state_ssm_im[0])
    ssm_s, sfin_s = _s5(u_s.reshape(Bs, Ss, SSM_WIDTH), s0, wts, lt=Ss, bb=64, name="s5_sample")
    ssm_s = ssm_s.reshape(Ts, SSM_WIDTH)
    kr_s32 = kr_s[:, QK_NOPE:QK_NOPE + QK_ROPE]
    n_pool = cache_kv.shape[1]
    o_lat = _attn_sample(page_table, q_lat.reshape(Ts * N_HEADS, KV_RANK), q_rope.reshape(Ts * N_HEADS, QK_ROPE),
                         ckv_s, kr_s32, cache_kv.reshape(n_pool, PAGE_SIZE, KV_RANK),
                         cache_k_rope.reshape(n_pool, PAGE_SIZE, QK_ROPE).transpose(0, 2, 1),
                         batch=Bs, n_new=Ss, cp=64)
    mla_s = _mm(o_lat.reshape(Ts, N_HEADS * KV_RANK), wts["wuv_blockdiag"], tm=tm, out_dtype=BF16, name="uv_sample")
    y_s = _token_tail(ssm_s, mla_s, gs_s, gm_s, x_sample.reshape(Ts, D_MODEL), p_sample.reshape(Ts, PLE_DIM), wts,
                      tm=tm, tm_moe=tm_moe)

    sre_p, sim_p = _unpack_state(sfin_p)
    sre_s, sim_s = _unpack_state(sfin_s)
    kr_p32 = kr_p[:, QK_NOPE:QK_NOPE + QK_ROPE]
    return (y_p.reshape(B, S, D_MODEL), y_s.reshape(Bs, Ss, D_MODEL),
            ckv_p.reshape(1, B, S, KV_RANK), kr_p32.reshape(1, B, S, QK_ROPE),
            sre_p[None], sim_p[None],
            ckv_s.reshape(1, Bs, Ss, KV_RANK), kr_s32.reshape(1, Bs, Ss, QK_ROPE),
            sre_s[None], sim_s[None])
```

```python
import functools
import math

import jax
import jax.numpy as jnp
from jax import lax
from jax.experimental import pallas as pl
from jax.experimental.pallas import tpu as pltpu

F32 = jnp.float32
BF16 = jnp.bfloat16

D_MODEL = 1024
DEPTH = 1
PAST_LEN = 16384
PAGE_SIZE = 128
SSM_WIDTH = 512
SSM_GROUP = 16
SSM_GROUPS = 32
SSM_STATE = 64
STATE_COLS = 2 * SSM_GROUPS * SSM_STATE
N_HEADS = 8
QK_NOPE = 64
QK_ROPE = 32
V_DIM = 64
Q_RANK = 256
KV_RANK = 256
HEAD_PAD = 128
ROPE_THETA = 10000.0
ATTN_SCALE = (QK_NOPE + QK_ROPE) ** -0.5
Q_SCALE = ATTN_SCALE * math.log2(math.e)
N_GROUPS = 4
EXPERTS_PER_GROUP = 8
D_EXPERT = 256
PLE_DIM = 256
LN_EPS = 1e-5
RMS_EPS = 1e-6
ALPHA = (2 * DEPTH) ** 0.25

LANES = 128
MXU_DIM = 256
VMEM_LIMIT = 56 * 1024 * 1024

ATTN_HEADS_PER_BODY = 4
ROW_CHAINS = 2

NEG_INF = float("-inf")


def _params(sem, vmem=VMEM_LIMIT):
    return pltpu.CompilerParams(dimension_semantics=sem, vmem_limit_bytes=vmem)


def _const_spec(shape):
    zeros = (0,) * len(shape)
    return pl.BlockSpec(shape, lambda *_: zeros)


def _dot(a, b):
    return jnp.dot(a, b, preferred_element_type=F32)


def _dot_t(a, b):
    return lax.dot_general(a, b, (((1,), (1,)), ((), ())), preferred_element_type=F32)


def _layer_norm(x, g, b):
    mu = jnp.mean(x, axis=-1, keepdims=True)
    xc = x - mu
    var = jnp.mean(xc * xc, axis=-1, keepdims=True)
    return xc * lax.rsqrt(var + LN_EPS) * g + b


def _rms_norm(x, g):
    ms = jnp.mean(x * x, axis=-1, keepdims=True)
    return x * lax.rsqrt(ms + RMS_EPS) * g


def _proj_kernel(x_ref, w_ref, qg_ref, kg_ref, cos_ref, sin_ref, wq_ref, wqr_ref, *rest, absorbed):
    if absorbed:
        (wuk_ref, wqc_ref, wqcr_ref, cos8_ref, sin8_ref,
         u_ref, ckv_ref, kr_ref, gs_ref, gm_ref, ql_ref, qr_ref) = rest
    else:
        wk_ref, wv_ref, u_ref, ckv_ref, kr_ref, gs_ref, gm_ref, q_ref, k_ref, v_ref = rest
    xb = x_ref[...].astype(BF16)
    u_ref[...] = _dot(xb, w_ref[:, 0:512])
    cq = _rms_norm(_dot(xb, w_ref[:, 512:768]), qg_ref[...])
    ckv = _rms_norm(_dot(xb, w_ref[:, 768:1024]), kg_ref[...])
    ckv_ref[...] = ckv
    gs_ref[...] = jax.nn.sigmoid(_dot(xb, w_ref[:, 1024:2048])).astype(BF16)
    gm_ref[...] = jax.nn.sigmoid(_dot(xb, w_ref[:, 2048:3072])).astype(BF16)
    cos = cos_ref[...]
    sin = sin_ref[...]
    kr = _dot(xb, w_ref[:, 3072:3200]) * cos + _dot(xb, w_ref[:, 3200:3328]) * sin
    kr_ref[...] = kr

    cqb = cq.astype(BF16)
    qa = _dot(cqb, wq_ref[...])
    qb = _dot(cqb, wqr_ref[...])
    lane = lax.broadcasted_iota(jnp.int32, cos.shape, 1)
    cq_tab = (cos + jnp.where(lane < QK_NOPE, 1.0, 0.0)) * Q_SCALE
    sq_tab = sin * Q_SCALE
    ckvb = ckv.astype(BF16)
    if absorbed:
        for h in range(N_HEADS):
            sl = slice(h * HEAD_PAD, (h + 1) * HEAD_PAD)
            qh = (qa[:, sl] * cq_tab + qb[:, sl] * sq_tab).astype(BF16)
            ql_ref[:, h * KV_RANK:(h + 1) * KV_RANK] = _dot(qh, wuk_ref[h]).astype(BF16)
        qc = _dot(cqb, wqc_ref[...]) * cos8_ref[...] + _dot(cqb, wqcr_ref[...]) * sin8_ref[...]
        qr_ref[...] = (qc * Q_SCALE).astype(BF16)
    else:
        kn = _dot(ckvb, wk_ref[...])
        vv = _dot(ckvb, wv_ref[...])
        for h in range(N_HEADS):
            sl = slice(h * HEAD_PAD, (h + 1) * HEAD_PAD)
            q_ref[h] = (qa[:, sl] * cq_tab + qb[:, sl] * sq_tab).astype(BF16)
            k_ref[h] = (kn[:, sl] + kr).astype(BF16)
            v_ref[h] = jnp.where(lane == V_DIM, 1.0, vv[:, sl]).astype(BF16)


def _proj(x, wts, tabs, *, tm, seq_tiles, absorbed):
    T = x.shape[0]
    nt = T // tm
    row = lambda i: (i, 0)
    if seq_tiles is None:
        tab_map = lambda i: (0, 0)
    else:
        tab_map = lambda i: (i % seq_tiles, 0)
    u_shape, u_spec = (T, SSM_WIDTH), pl.BlockSpec((tm, SSM_WIDTH), row)
    in_specs = [
        pl.BlockSpec((tm, D_MODEL), row),
        _const_spec(wts["w_in"].shape),
        _const_spec((1, Q_RANK)), _const_spec((1, KV_RANK)),
        pl.BlockSpec((tm, HEAD_PAD), tab_map), pl.BlockSpec((tm, HEAD_PAD), tab_map),
        _const_spec(wts["wq"].shape), _const_spec(wts["wq_rot"].shape),
    ]
    args = [x, wts["w_in"], wts["q_norm_g"], wts["kv_norm_g"], tabs["cos128"], tabs["sin128"],
            wts["wq"], wts["wq_rot"]]
    out_shape = [jax.ShapeDtypeStruct(u_shape, F32),
                 jax.ShapeDtypeStruct((T, KV_RANK), F32),
                 jax.ShapeDtypeStruct((T, HEAD_PAD), F32),
                 jax.ShapeDtypeStruct((T, D_MODEL), BF16),
                 jax.ShapeDtypeStruct((T, D_MODEL), BF16)]
    out_specs = [u_spec, pl.BlockSpec((tm, KV_RANK), row), pl.BlockSpec((tm, HEAD_PAD), row),
                 pl.BlockSpec((tm, D_MODEL), row), pl.BlockSpec((tm, D_MODEL), row)]
    if absorbed:
        in_specs += [_const_spec(wts["wuk_abs"].shape), _const_spec(wts["wq_rope"].shape),
                     _const_spec(wts["wq_rope_rot"].shape),
                     pl.BlockSpec((tm, N_HEADS * QK_ROPE), tab_map),
                     pl.BlockSpec((tm, N_HEADS * QK_ROPE), tab_map)]
        args += [wts["wuk_abs"], wts["wq_rope"], wts["wq_rope_rot"], tabs["cos_heads"], tabs["sin_heads"]]
        out_shape += [jax.ShapeDtypeStruct((T, N_HEADS * KV_RANK), BF16),
                      jax.ShapeDtypeStruct((T, N_HEADS * QK_ROPE), BF16)]
        out_specs += [pl.BlockSpec((tm, N_HEADS * KV_RANK), row), pl.BlockSpec((tm, N_HEADS * QK_ROPE), row)]
    else:
        in_specs += [_const_spec(wts["wk"].shape), _const_spec(wts["wv"].shape)]
        args += [wts["wk"], wts["wv"]]
        head = jax.ShapeDtypeStruct((N_HEADS, T, HEAD_PAD), BF16)
        head_spec = pl.BlockSpec((N_HEADS, tm, HEAD_PAD), lambda i: (0, i, 0))
        out_shape += [head, head, head]
        out_specs += [head_spec, head_spec, head_spec]
    return pl.pallas_call(
        functools.partial(_proj_kernel, absorbed=absorbed),
        grid=(nt,), in_specs=in_specs, out_specs=out_specs, out_shape=out_shape,
        compiler_params=_params(("parallel",)),
        name="proj_sample" if absorbed else "proj_prompt",
    )(*args)


def _s5_kernel(u_ref, s0_ref, lam_ref, wb_ref, wc_ref, dskip_ref, wglu_ref, bglu_ref,
               y_ref, sfin_ref, bu_ref, st_ref, tr_ref, *, lt, bb):
    ti = pl.program_id(1)
    rows = lt * bb
    n_tiles = STATE_COLS // MXU_DIM
    n_slabs = SSM_WIDTH // LANES

    @pl.when(ti == 0)
    def _():
        st_ref[...] = s0_ref[...]

    for b in range(bb):
        for g in range(n_slabs):
            tr_ref[g, pl.ds(b, lt, stride=bb), :] = u_ref[b, :, g * LANES:(g + 1) * LANES]
    u = jnp.concatenate([tr_ref[g] for g in range(n_slabs)], axis=1)
    ub = u.astype(BF16)
    for j in range(n_tiles):
        k0 = MXU_DIM * (j // (n_tiles // 2))
        bu_ref[:, j * MXU_DIM:(j + 1) * MXU_DIM] = _dot(ub[:, k0:k0 + MXU_DIM], wb_ref[j])

    jg = max(1, 64 // bb)
    for j0 in range(0, n_tiles, jg):
        lrs = [jnp.broadcast_to(lam_ref[0:1, (j0 + q) * LANES:(j0 + q + 1) * LANES], (bb, LANES)) for q in range(jg)]
        lis = [jnp.broadcast_to(lam_ref[1:2, (j0 + q) * LANES:(j0 + q + 1) * LANES], (bb, LANES)) for q in range(jg)]

        def body(t, carry, j0=j0, lrs=lrs, lis=lis):
            r0 = pl.multiple_of(t * bb, bb)
            new = []
            for q in range(jg):
                c0 = (j0 + q) * MXU_DIM
                sr, si = carry[2 * q], carry[2 * q + 1]
                nr = lrs[q] * sr - lis[q] * si + bu_ref[pl.ds(r0, bb), c0:c0 + LANES]
                ni = lrs[q] * si + lis[q] * sr + bu_ref[pl.ds(r0, bb), c0 + LANES:c0 + 2 * LANES]
                bu_ref[pl.ds(r0, bb), c0:c0 + LANES] = nr
                bu_ref[pl.ds(r0, bb), c0 + LANES:c0 + 2 * LANES] = ni
                new += [nr, ni]
            return tuple(new)

        init = []
        for q in range(jg):
            c0 = (j0 + q) * MXU_DIM
            init += [st_ref[:, c0:c0 + LANES], st_ref[:, c0 + LANES:c0 + 2 * LANES]]
        fin = lax.fori_loop(0, lt, body, tuple(init), unroll=min(lt, 8))
        for q in range(jg):
            c0 = (j0 + q) * MXU_DIM
            st_ref[:, c0:c0 + LANES] = fin[2 * q]
            st_ref[:, c0 + LANES:c0 + 2 * LANES] = fin[2 * q + 1]

    half = STATE_COLS // 2
    y = jnp.concatenate([_dot(bu_ref[:, 0:half].astype(BF16), wc_ref[0]),
                         _dot(bu_ref[:, half:STATE_COLS].astype(BF16), wc_ref[1])], axis=1)
    y = y + dskip_ref[...] * u
    z = jax.nn.gelu(y, approximate=True)
    gate = jax.nn.sigmoid(_dot(z.astype(BF16), wglu_ref[...]) + bglu_ref[...])
    out = z * gate
    for g in range(n_slabs):
        tr_ref[g] = out[:, g * LANES:(g + 1) * LANES]
    for b in range(bb):
        y_ref[b] = jnp.concatenate([tr_ref[g, pl.ds(b, lt, stride=bb), :] for g in range(n_slabs)],
                                   axis=1).astype(BF16)

    @pl.when(ti == pl.num_programs(1) - 1)
    def _():
        sfin_ref[...] = st_ref[...]


def _s5(u3, s0, wts, *, lt, bb, name):
    B, S, _ = u3.shape
    return pl.pallas_call(
        functools.partial(_s5_kernel, lt=lt, bb=bb),
        grid=(B // bb, S // lt),
        in_specs=[pl.BlockSpec((bb, lt, SSM_WIDTH), lambda b, t: (b, t, 0)),
                  pl.BlockSpec((bb, STATE_COLS), lambda b, t: (b, 0)),
                  _const_spec(wts["lam"].shape), _const_spec(wts["s5_wb"].shape),
                  _const_spec(wts["s5_wc"].shape), _const_spec((1, SSM_WIDTH)),
                  _const_spec(wts["w_glu"].shape), _const_spec((1, SSM_WIDTH))],
        out_specs=[pl.BlockSpec((bb, lt, SSM_WIDTH), lambda b, t: (b, t, 0)),
                   pl.BlockSpec((bb, STATE_COLS), lambda b, t: (b, 0))],
        out_shape=[jax.ShapeDtypeStruct((B, S, SSM_WIDTH), BF16),
                   jax.ShapeDtypeStruct((B, STATE_COLS), F32)],
        scratch_shapes=[pltpu.VMEM((lt * bb, STATE_COLS), F32), pltpu.VMEM((bb, STATE_COLS), F32),
                        pltpu.VMEM((SSM_WIDTH // LANES, lt * bb, LANES), F32)],
        compiler_params=_params(("parallel", "arbitrary")),
        name=name,
    )(u3, s0, wts["lam"], wts["s5_wb"], wts["s5_wc"], wts["d_skip"], wts["w_glu"], wts["b_glu"])


def _attn_prompt_kernel(q_ref, k_ref, v_ref, o_ref, *, tq, tk):
    qi = pl.program_id(1)
    n_diag = tq // tk
    row = lax.broadcasted_iota(jnp.int32, (tq, tk), 0)
    col = lax.broadcasted_iota(jnp.int32, (tq, tk), 1)
    n_full = qi * n_diag
    lane = lax.broadcasted_iota(jnp.int32, (tq, HEAD_PAD), 1)
    pairs = []
    for h0 in range(0, N_HEADS, ATTN_HEADS_PER_BODY):
        heads = tuple(range(h0, h0 + ATTN_HEADS_PER_BODY))
        qs = [q_ref[h] for h in heads]

        def step(kt, carry, diag=None, heads=heads, qs=qs):
            k0 = pl.multiple_of(kt * tk, tk)
            new = []
            for j, h in enumerate(heads):
                m, acc = carry[2 * j], carry[2 * j + 1]
                s = _dot_t(qs[j], k_ref[h, pl.ds(k0, tk), :])
                if diag is not None:
                    s = jnp.where(col + diag * tk <= row, s, NEG_INF)
                m_new = jnp.maximum(m, jnp.max(s, axis=1, keepdims=True))
                p = jnp.exp2(s - m_new).astype(BF16)
                acc = jnp.exp2(m - m_new) * acc + _dot(p, v_ref[h, pl.ds(k0, tk), :])
                new += [m_new, acc]
            return tuple(new)

        carry = (jnp.full((tq, 1), NEG_INF, F32), jnp.zeros((tq, HEAD_PAD), F32)) * ATTN_HEADS_PER_BODY
        carry = lax.fori_loop(0, n_full, step, carry)
        for d in range(n_diag):
            carry = step(n_full + d, carry, diag=d)
        outs = [jnp.where(lane < V_DIM, acc / acc[:, V_DIM:V_DIM + 1], 0.0) for acc in carry[1::2]]
        pairs += [outs[j] + pltpu.roll(outs[j + 1], V_DIM, axis=1) for j in range(0, ATTN_HEADS_PER_BODY, 2)]
    o_ref[...] = jnp.concatenate(pairs, axis=1).astype(BF16)


def _attn_prompt(q3, k3, v3, *, batch, seq, tq, tk):
    nq = seq // tq
    T = batch * seq
    return pl.pallas_call(
        functools.partial(_attn_prompt_kernel, tq=tq, tk=tk),
        grid=(batch, nq),
        in_specs=[pl.BlockSpec((N_HEADS, tq, HEAD_PAD), lambda b, i: (0, b * nq + i, 0)),
                  pl.BlockSpec((N_HEADS, seq, HEAD_PAD), lambda b, i: (0, b, 0)),
                  pl.BlockSpec((N_HEADS, seq, HEAD_PAD), lambda b, i: (0, b, 0))],
        out_specs=pl.BlockSpec((tq, N_HEADS * V_DIM), lambda b, i: (b * nq + i, 0)),
        out_shape=jax.ShapeDtypeStruct((T, N_HEADS * V_DIM), BF16),
        compiler_params=_params(("parallel", "arbitrary")),
        name="attn_prompt",
    )(q3, k3, v3)


def _attn_sample_kernel(pt_ref, ql_ref, qr_ref, qlp_ref, qrp_ref, cn_ref, krn_ref, kv_hbm, kr_hbm, o_ref,
                        kvbuf, krbuf, kv_sem, kr_sem, kvb_ref, s_ref, m_ref, l_ref, acc_ref,
                        *, cp, n_chunks, n_steps, n_new):
    step = pl.program_id(0)
    slot = lax.rem(step, 2)

    def page_copies(chunk, sl):
        b = lax.div(chunk, n_chunks)
        c = lax.rem(chunk, n_chunks)
        out = []
        for k in range(cp):
            page = pt_ref[b, c * cp + k]
            out.append(pltpu.make_async_copy(kv_hbm.at[page], kvbuf.at[sl, k], kv_sem.at[sl]))
            out.append(pltpu.make_async_copy(kr_hbm.at[page], krbuf.at[sl, k], kr_sem.at[sl]))
        return out

    @pl.when(step == 0)
    def _():
        for d in page_copies(0, 0):
            d.start()

    @pl.when(step + 1 < n_steps)
    def _():
        for d in page_copies(step + 1, 1 - slot):
            d.start()

    @pl.when(step < n_steps)
    def _():
        for d in page_copies(step, slot):
            d.wait()

    def reset_state():
        m_ref[...] = jnp.full(m_ref.shape, NEG_INF, F32)
        l_ref[...] = jnp.zeros(l_ref.shape, F32)
        acc_ref[...] = jnp.zeros(acc_ref.shape, F32)

    @pl.when(step == 0)
    def _():
        reset_state()
        kvb_ref[1] = jnp.zeros(kvb_ref.shape[1:], BF16)
        s_ref[1] = jnp.zeros(s_ref.shape[1:], F32)

    ql = ql_ref[...]
    qr = qr_ref[...]

    def update(state, s, vals):
        m, l, acc = state
        m_new = jnp.maximum(m, jnp.max(s, axis=1, keepdims=True))
        alpha = jnp.exp2(m - m_new)
        p = jnp.exp2(s - m_new)
        return (m_new, alpha * l + jnp.sum(p, axis=1, keepdims=True),
                alpha * acc + _dot(p.astype(BF16), vals))

    def fold(sl):
        state = update((m_ref[...], l_ref[...], acc_ref[...]), s_ref[sl], kvb_ref[sl])
        m_ref[...], l_ref[...], acc_ref[...] = state
        return state

    def body(sl):
        kv = kvbuf[sl].reshape(cp * PAGE_SIZE, KV_RANK).astype(BF16)
        kr_t = jnp.concatenate([krbuf[sl, k] for k in range(cp)], axis=1).astype(BF16)
        kvb_ref[sl] = kv
        s_ref[sl] = _dot_t(ql, kv) + _dot(qr, kr_t)
        fold(1 - sl)

    for sl in range(2):
        pl.when(slot == sl)(functools.partial(body, sl))

    @pl.when(lax.rem(step, n_chunks) == 0)
    def _():
        @pl.when(step > 0)
        def _():
            qlp = qlp_ref[...]
            qrp = qrp_ref[...]
            cn = cn_ref[...].astype(BF16)
            krn = krn_ref[...].astype(BF16)
            s = _dot_t(qlp, cn) + _dot_t(qrp, krn)
            rows = qlp.shape[0]
            t_row = jnp.right_shift(lax.broadcasted_iota(jnp.int32, (rows, n_new), 0), 3)
            t_col = lax.broadcasted_iota(jnp.int32, (rows, n_new), 1)
            state = (m_ref[...], l_ref[...], acc_ref[...])
            _, l, acc = update(state, jnp.where(t_col <= t_row, s, NEG_INF), cn)
            o_ref[...] = (acc / l).astype(BF16)

        reset_state()


def _attn_sample(page_table, q_lat, q_rope, ckv_new, kr_new, cache_kv, cache_kr_t, *, batch, n_new, cp):
    rows = n_new * N_HEADS
    n_chunks = page_table.shape[1] // cp
    n_steps = batch * n_chunks
    cur = lambda t, pt: (jnp.minimum(t, n_steps - 1) // n_chunks, 0)
    prev = lambda t, pt: (jnp.maximum(t - 1, 0) // n_chunks, 0)
    in_specs = [pl.BlockSpec((rows, KV_RANK), cur), pl.BlockSpec((rows, QK_ROPE), cur),
                pl.BlockSpec((rows, KV_RANK), prev), pl.BlockSpec((rows, QK_ROPE), prev),
                pl.BlockSpec((n_new, KV_RANK), prev), pl.BlockSpec((n_new, QK_ROPE), prev),
                pl.BlockSpec(memory_space=pl.ANY), pl.BlockSpec(memory_space=pl.ANY)]
    grid_spec = pltpu.PrefetchScalarGridSpec(
        num_scalar_prefetch=1, grid=(n_steps + 1,), in_specs=in_specs,
        out_specs=pl.BlockSpec((rows, KV_RANK), prev),
        scratch_shapes=[pltpu.VMEM((2, cp, PAGE_SIZE, KV_RANK), F32), pltpu.VMEM((2, cp, QK_ROPE, PAGE_SIZE), F32),
                        pltpu.SemaphoreType.DMA((2,)), pltpu.SemaphoreType.DMA((2,)),
                        pltpu.VMEM((2, cp * PAGE_SIZE, KV_RANK), BF16), pltpu.VMEM((2, rows, cp * PAGE_SIZE), F32),
                        pltpu.VMEM((rows, 1), F32), pltpu.VMEM((rows, 1), F32), pltpu.VMEM((rows, KV_RANK), F32)])
    return pl.pallas_call(
        functools.partial(_attn_sample_kernel, cp=cp, n_chunks=n_chunks, n_steps=n_steps, n_new=n_new),
        grid_spec=grid_spec,
        out_shape=jax.ShapeDtypeStruct((batch * rows, KV_RANK), BF16),
        compiler_params=_params(("arbitrary",)),
        name="attn_sample",
    )(page_table, q_lat, q_rope, q_lat, q_rope, ckv_new, kr_new, cache_kv, cache_kr_t)


def _mm_kernel(x_ref, w_ref, o_ref):
    o_ref[...] = _dot(x_ref[...], w_ref[...]).astype(o_ref.dtype)


def _mm(x, w, *, tm, out_dtype, name):
    M, K = x.shape
    N = w.shape[1]
    return pl.pallas_call(
        _mm_kernel, grid=(M // tm,),
        in_specs=[pl.BlockSpec((tm, K), lambda i: (i, 0)), _const_spec((K, N))],
        out_specs=pl.BlockSpec((tm, N), lambda i: (i, 0)),
        out_shape=jax.ShapeDtypeStruct((M, N), out_dtype),
        compiler_params=_params(("parallel",)), name=name,
    )(x, w)


def _merge_kernel(ssm_ref, mla_ref, gs_ref, gm_ref, x_ref, wbs_ref, wbm_ref, wo_ref, g_ref, b_ref,
                  wr_ref, br_ref, x1_ref, route_ref, cnt_ref, run_ref, *, tm):
    i = pl.program_id(0)

    @pl.when(i == 0)
    def _():
        run_ref[...] = jnp.zeros(run_ref.shape, F32)

    hm = tm // ROW_CHAINS
    lane = lax.broadcasted_iota(jnp.int32, (hm, LANES), 1).astype(F32)
    r_i = lax.broadcasted_iota(jnp.int32, (hm, hm), 0)
    c_i = lax.broadcasted_iota(jnp.int32, (hm, hm), 1)
    before = jnp.where(c_i < r_i, 1.0, 0.0).astype(BF16)
    run = run_ref[0:1, :]
    for c in range(ROW_CHAINS):
        rows = slice(c * hm, (c + 1) * hm)
        a = _dot(ssm_ref[rows, :], wbs_ref[...])
        m = _dot(mla_ref[rows, :], wbm_ref[...])
        merged = gs_ref[rows, :].astype(F32) * a + gm_ref[rows, :].astype(F32) * m
        y = _dot(merged.astype(BF16), wo_ref[...])
        x1 = _layer_norm(ALPHA * x_ref[rows, :] + y, g_ref[...], b_ref[...])
        x1_ref[rows, :] = x1

        logits = _dot(x1.astype(BF16), wr_ref[...]) + br_ref[...]
        gl = jnp.where(lane < N_GROUPS, logits, NEG_INF)
        gmax = jnp.max(gl, axis=1, keepdims=True)
        g_sel = jnp.min(jnp.where(gl == gmax, lane, float(LANES)), axis=1, keepdims=True)
        onehot = jnp.where(lane == g_sel, 1.0, 0.0)
        rank_all = _dot(before, onehot.astype(BF16)) + run
        rank = jnp.sum(onehot * rank_all, axis=1, keepdims=True)
        route_ref[rows, :] = jnp.where(lane == 0.0, g_sel, jnp.where(lane == 1.0, rank, 0.0))
        run = run + jnp.sum(onehot, axis=0, keepdims=True)
    run_ref[0:1, :] = run

    @pl.when(i == pl.num_programs(0) - 1)
    def _():
        cnt_ref[...] = run_ref[...]


def _merge(ssm, mla, gs, gm, x, wts, *, tm):
    T = x.shape[0]
    row = lambda i: (i, 0)
    ssm_spec = pl.BlockSpec((tm, SSM_WIDTH), row)
    return pl.pallas_call(
        functools.partial(_merge_kernel, tm=tm),
        grid=(T // tm,),
        in_specs=[ssm_spec, pl.BlockSpec((tm, N_HEADS * V_DIM), row),
                  pl.BlockSpec((tm, D_MODEL), row), pl.BlockSpec((tm, D_MODEL), row),
                  pl.BlockSpec((tm, D_MODEL), row),
                  _const_spec(wts["w_br_ssm"].shape), _const_spec(wts["w_br_mla"].shape),
                  _const_spec(wts["w_out"].shape), _const_spec((1, D_MODEL)), _const_spec((1, D_MODEL)),
                  _const_spec((D_MODEL, LANES)), _const_spec((1, LANES))],
        out_specs=[pl.BlockSpec((tm, D_MODEL), row), pl.BlockSpec((tm, LANES), row),
                   _const_spec((8, LANES))],
        out_shape=[jax.ShapeDtypeStruct((T, D_MODEL), F32), jax.ShapeDtypeStruct((T, LANES), F32),
                   jax.ShapeDtypeStruct((8, LANES), F32)],
        scratch_shapes=[pltpu.VMEM((8, LANES), F32)],
        compiler_params=_params(("arbitrary",)),
        name="merge_ln1_route",
    )(ssm, mla, gs, gm, x, wts["w_br_ssm"], wts["w_br_mla"], wts["w_out"], wts["ln1_g"], wts["ln1_b"],
      wts["wrg"], wts["brg"])


def _dispatch_kernel(pad_ref, x_ref, dest_ref, xs_ref, zero_ref, sem, *, tm):
    i = pl.program_id(0)

    def row_copy(r, d):
        return pltpu.make_async_copy(x_ref.at[pl.ds(r, 1)], xs_ref.at[pl.ds(d, 1)], sem)

    def zero_copy(d):
        return pltpu.make_async_copy(zero_ref.at[pl.ds(0, 1)], xs_ref.at[pl.ds(d, 1)], sem)

    @pl.when(i == 0)
    def _():
        zero_ref[...] = jnp.zeros(zero_ref.shape, F32)
        for g in range(N_GROUPS + 1):
            start = pad_ref[g]
            n = pad_ref[N_GROUPS + 1 + g]

            def issue(r, c, start=start):
                zero_copy(start + r).start()
                return c

            def drain(r, c, start=start):
                zero_copy(start + r).wait()
                return c

            lax.fori_loop(0, n, issue, 0)
            lax.fori_loop(0, n, drain, 0)

    for r in range(tm):
        row_copy(r, dest_ref[0, 0, r]).start(priority=r % 2)
    pltpu.make_async_copy(x_ref, xs_ref.at[pl.ds(0, tm)], sem).wait()


def _dispatch(x1, dest, pad_info, *, tm, n_rows):
    T = x1.shape[0]
    nt = T // tm
    grid_spec = pltpu.PrefetchScalarGridSpec(
        num_scalar_prefetch=1, grid=(nt,),
        in_specs=[pl.BlockSpec((tm, D_MODEL), lambda i, pad: (i, 0)),
                  pl.BlockSpec((1, 1, tm), lambda i, pad: (i, 0, 0), memory_space=pltpu.SMEM)],
        out_specs=pl.BlockSpec(memory_space=pl.ANY),
        scratch_shapes=[pltpu.VMEM((8, D_MODEL), F32), pltpu.SemaphoreType.DMA(())])
    return pl.pallas_call(
        functools.partial(_dispatch_kernel, tm=tm),
        grid_spec=grid_spec,
        out_shape=jax.ShapeDtypeStruct((n_rows, D_MODEL), F32),
        compiler_params=_params(("arbitrary",)),
        name="moe_dispatch",
    )(pad_info, x1, dest.reshape(nt, 1, tm))


def _moe_kernel(grp_ref, blk_ref, nv_ref, x_ref, wr_ref, br_ref, wg_ref, wu_ref, wd_ref, o_ref, *, tm):
    del blk_ref
    i = pl.program_id(0)

    @pl.when(i < nv_ref[0])
    def _():
        g = grp_ref[i]
        x = x_ref[...]
        xb = x.astype(BF16)
        logits = _dot(xb, wr_ref[0]) + br_ref[0]
        lane = lax.broadcasted_iota(jnp.int32, (tm, LANES), 1).astype(F32)
        is_grp = (lane >= EXPERTS_PER_GROUP) & (lane < EXPERTS_PER_GROUP + N_GROUPS)
        gl = jnp.where(is_grp, logits, NEG_INF)
        gmax = jnp.max(gl, axis=1, keepdims=True)
        gexp = jnp.exp(gl - gmax)
        p_group = (jnp.sum(jnp.where(lane == (EXPERTS_PER_GROUP + g).astype(F32), gexp, 0.0), axis=1, keepdims=True)
                   / jnp.sum(gexp, axis=1, keepdims=True))
        el = jnp.where(lane < EXPERTS_PER_GROUP, logits, NEG_INF)
        v1 = jnp.max(el, axis=1, keepdims=True)
        i1 = jnp.min(jnp.where(el == v1, lane, float(LANES)), axis=1, keepdims=True)
        el2 = jnp.where(lane == i1, NEG_INF, el)
        v2 = jnp.max(el2, axis=1, keepdims=True)
        i2 = jnp.min(jnp.where(el2 == v2, lane, float(LANES)), axis=1, keepdims=True)
        e2 = jnp.exp(v2 - v1)
        w1 = p_group / (1.0 + e2)
        w2 = w1 * e2
        gates = jnp.where(lane == i1, w1, jnp.where(lane == i2, w2, 0.0))

        hs = []
        for e in range(EXPERTS_PER_GROUP):
            hg = _dot(xb, wg_ref[0, e])
            hu = _dot(xb, wu_ref[0, e])
            ge = jnp.sum(jnp.where(lane == float(e), gates, 0.0), axis=1, keepdims=True)
            hs.append((jax.nn.silu(hg) * hu * ge).astype(BF16))
        o_ref[...] = _dot(jnp.concatenate(hs, axis=1), wd_ref[0])

    @pl.when(i >= nv_ref[0])
    def _():
        o_ref[...] = jnp.zeros(o_ref.shape, F32)


def _moe(xs, tile_grp, tile_blk, n_valid, wts, *, tm):
    n_rows = xs.shape[0]
    nt = n_rows // tm
    grp3 = lambda i, grp, blk, nv: (grp[i], 0, 0)
    grp4 = lambda i, grp, blk, nv: (grp[i], 0, 0, 0)
    rows = lambda i, grp, blk, nv: (blk[i], 0)
    grid_spec = pltpu.PrefetchScalarGridSpec(
        num_scalar_prefetch=3, grid=(nt,),
        in_specs=[pl.BlockSpec((tm, D_MODEL), rows),
                  pl.BlockSpec((1, D_MODEL, LANES), grp3), pl.BlockSpec((1, 1, LANES), grp3),
                  pl.BlockSpec((1, EXPERTS_PER_GROUP, D_MODEL, D_EXPERT), grp4),
                  pl.BlockSpec((1, EXPERTS_PER_GROUP, D_MODEL, D_EXPERT), grp4),
                  pl.BlockSpec((1, EXPERTS_PER_GROUP * D_EXPERT, D_MODEL), grp3)],
        out_specs=pl.BlockSpec((tm, D_MODEL), lambda i, grp, blk, nv: (i, 0)))
    return pl.pallas_call(
        functools.partial(_moe_kernel, tm=tm),
        grid_spec=grid_spec,
        out_shape=jax.ShapeDtypeStruct((n_rows, D_MODEL), F32),
        compiler_params=_params(("arbitrary",)),
        name="moe_experts",
    )(tile_grp, tile_blk, n_valid, xs, wts["wroute"], wts["broute"],
      wts["w_e_gate"], wts["w_e_up"], wts["w_e_down"])


def _final_kernel(x1_ref, dest_ref, dest_next_ref, ys_ref, p_ref, g_ref, b_ref, wpg_ref, bpg_ref, wpp_ref, o_ref,
                  moe_ref, sem, *, tm):
    i = pl.program_id(0)
    slot = lax.rem(i, 2)

    def gather(d_ref, sl):
        for r in range(tm):
            pltpu.make_async_copy(ys_ref.at[pl.ds(d_ref[0, 0, r], 1)], moe_ref.at[sl, pl.ds(r, 1)],
                                  sem.at[sl]).start(priority=r % 2)

    @pl.when(i == 0)
    def _():
        gather(dest_ref, 0)

    @pl.when(i + 1 < pl.num_programs(0))
    def _():
        gather(dest_next_ref, 1 - slot)

    pltpu.make_async_copy(ys_ref.at[pl.ds(0, tm)], moe_ref.at[slot], sem.at[slot]).wait()
    hm = tm // ROW_CHAINS
    for c in range(ROW_CHAINS):
        rows = slice(c * hm, (c + 1) * hm)
        x2 = _layer_norm(ALPHA * x1_ref[rows, :] + moe_ref[slot, rows, :], g_ref[...], b_ref[...])
        gate = jax.nn.sigmoid(_dot(x2.astype(BF16), wpg_ref[...]) + bpg_ref[...])
        o_ref[rows, :] = x2 + gate * _dot(p_ref[rows, :].astype(BF16), wpp_ref[...])


def _final(x1, dest, ys, p, wts, *, tm):
    T = x1.shape[0]
    nt = T // tm
    row = lambda i: (i, 0)
    return pl.pallas_call(
        functools.partial(_final_kernel, tm=tm),
        grid=(nt,),
        in_specs=[pl.BlockSpec((tm, D_MODEL), row),
                  pl.BlockSpec((1, 1, tm), lambda i: (i, 0, 0), memory_space=pltpu.SMEM),
                  pl.BlockSpec((1, 1, tm), lambda i: (jnp.minimum(i + 1, nt - 1), 0, 0), memory_space=pltpu.SMEM),
                  pl.BlockSpec(memory_space=pl.ANY),
                  pl.BlockSpec((tm, PLE_DIM), row),
                  _const_spec((1, D_MODEL)), _const_spec((1, D_MODEL)),
                  _const_spec((D_MODEL, D_MODEL)), _const_spec((1, D_MODEL)), _const_spec((PLE_DIM, D_MODEL))],
        out_specs=pl.BlockSpec((tm, D_MODEL), row),
        out_shape=jax.ShapeDtypeStruct((T, D_MODEL), F32),
        scratch_shapes=[pltpu.VMEM((2, tm, D_MODEL), F32), pltpu.SemaphoreType.DMA((2,))],
        compiler_params=_params(("arbitrary",)),
        name="combine_ln2_ple",
    )(x1, dest.reshape(nt, 1, tm), dest.reshape(nt, 1, tm), ys, p, wts["ln2_g"], wts["ln2_b"], wts["w_ple_gate"], wts["b_ple_gate"],
      wts["w_ple_proj"])


def _prep_weights(w_in, a_re, a_im, log_dt, b_re, b_im, c_re, c_im, d_skip, w_glu, b_glu, w_br_ssm,
                  q_norm_g, w_uq, kv_norm_g, w_uk, w_uv, w_br_mla, w_out, ln1_g, ln1_b,
                  w_gr, b_gr, w_er, b_er, w_e_gate, w_e_up, w_e_down, ln2_g, ln2_b,
                  w_ple_gate, b_ple_gate, w_ple_proj):
    half = QK_ROPE // 2
    rot = lambda w: jnp.concatenate([-w[..., half:], w[..., :half]], axis=-1)
    w = {}

    u_w, cq_w, ckv_w, kr_w, gs_w, gm_w = jnp.split(
        w_in, (512, 768, 1024, 1024 + QK_ROPE, 1024 + QK_ROPE + D_MODEL), axis=1)
    slab = lambda c: jnp.pad(c, ((0, 0), (QK_NOPE, HEAD_PAD - QK_NOPE - QK_ROPE)))
    w["w_in"] = jnp.concatenate([u_w, cq_w, ckv_w, gs_w, gm_w, slab(kr_w), slab(rot(kr_w))], axis=1).astype(BF16)
    w["q_norm_g"] = q_norm_g.reshape(1, Q_RANK)
    w["kv_norm_g"] = kv_norm_g.reshape(1, KV_RANK)

    wq3 = w_uq.reshape(Q_RANK, N_HEADS, QK_NOPE + QK_ROPE)
    nope, rope = wq3[..., :QK_NOPE], wq3[..., QK_NOPE:]
    pad_tail = jnp.zeros((Q_RANK, N_HEADS, HEAD_PAD - QK_NOPE - QK_ROPE), F32)
    w["wq"] = jnp.concatenate([nope, rope, pad_tail], -1).reshape(Q_RANK, N_HEADS * HEAD_PAD).astype(BF16)
    w["wq_rot"] = jnp.concatenate([jnp.zeros_like(nope), rot(rope), pad_tail], -1).reshape(
        Q_RANK, N_HEADS * HEAD_PAD).astype(BF16)
    w["wq_rope"] = rope.reshape(Q_RANK, N_HEADS * QK_ROPE).astype(BF16)
    w["wq_rope_rot"] = rot(rope).reshape(Q_RANK, N_HEADS * QK_ROPE).astype(BF16)
    head_pad = lambda a: jnp.pad(a, ((0, 0), (0, 0), (0, HEAD_PAD - a.shape[-1]))).reshape(
        a.shape[0], N_HEADS * HEAD_PAD).astype(BF16)
    w["wk"] = head_pad(w_uk)
    w["wv"] = head_pad(w_uv)
    w["wuk_abs"] = jnp.pad(w_uk.transpose(1, 2, 0), ((0, 0), (0, HEAD_PAD - QK_NOPE), (0, 0))).astype(BF16)
    w["wuv_blockdiag"] = (jnp.eye(N_HEADS, dtype=F32)[:, None, :, None]
                          * w_uv.transpose(1, 0, 2)[:, :, None, :]).reshape(
                              N_HEADS * KV_RANK, N_HEADS * V_DIM).astype(BF16)

    dt = jnp.exp(log_dt)[:, None]
    mag = jnp.exp(dt * a_re)
    lr = mag * jnp.cos(dt * a_im)
    li = mag * jnp.sin(dt * a_im)
    den = a_re * a_re + a_im * a_im
    fr = ((lr - 1.0) * a_re + li * a_im) / den
    fi = (li * a_re - (lr - 1.0) * a_im) / den
    bbr = fr[..., None] * b_re - fi[..., None] * b_im
    bbi = fr[..., None] * b_im + fi[..., None] * b_re
    w["lam"] = jnp.stack([lr.reshape(-1), li.reshape(-1)])
    n_tiles = SSM_GROUPS // 2
    col_group = jnp.broadcast_to((2 * jnp.arange(n_tiles)[:, None, None, None]
                                  + jnp.arange(2)[None, None, :, None]), (n_tiles, 2, 2, SSM_STATE)).reshape(-1)
    ch_group = jnp.arange(SSM_WIDTH) // SSM_GROUP
    mask = ch_group[:, None] == col_group[None, :]
    bcols = jnp.stack([bbr, bbi]).reshape(2, n_tiles, 2, SSM_STATE, SSM_GROUP).transpose(
        1, 0, 2, 3, 4).reshape(STATE_COLS, SSM_GROUP)
    bmat = jnp.where(mask, jnp.tile(bcols.T, (SSM_GROUPS, 1)), 0.0)
    bmat4 = bmat.reshape(2, MXU_DIM, n_tiles, MXU_DIM)
    w["s5_wb"] = jnp.stack([bmat4[j // (n_tiles // 2), :, j, :] for j in range(n_tiles)]).astype(BF16)
    ccols = jnp.stack([c_re, -c_im]).reshape(2, n_tiles, 2, SSM_GROUP, SSM_STATE).transpose(
        1, 0, 2, 4, 3).reshape(STATE_COLS, SSM_GROUP)
    cmat = jnp.where(mask.T, jnp.tile(ccols, (1, SSM_GROUPS)), 0.0)
    hc = STATE_COLS // 2
    w["s5_wc"] = jnp.stack([cmat[:hc, :MXU_DIM], cmat[hc:, MXU_DIM:]]).astype(BF16)
    w["d_skip"] = d_skip.reshape(1, SSM_WIDTH)
    w["w_glu"] = w_glu.astype(BF16)
    w["b_glu"] = b_glu.reshape(1, SSM_WIDTH)

    w["w_br_ssm"] = w_br_ssm.astype(BF16)
    w["w_br_mla"] = w_br_mla.astype(BF16)
    w["w_out"] = w_out.astype(BF16)
    w["ln1_g"], w["ln1_b"] = ln1_g.reshape(1, D_MODEL), ln1_b.reshape(1, D_MODEL)
    w["ln2_g"], w["ln2_b"] = ln2_g.reshape(1, D_MODEL), ln2_b.reshape(1, D_MODEL)

    wrg = jnp.pad(w_gr, ((0, 0), (0, LANES - N_GROUPS)))
    w["wrg"] = wrg.astype(BF16)
    w["brg"] = jnp.pad(b_gr, (0, LANES - N_GROUPS)).reshape(1, LANES)
    er = w_er.reshape(D_MODEL, N_GROUPS, EXPERTS_PER_GROUP).transpose(1, 0, 2)
    gr = jnp.broadcast_to(w_gr[None], (N_GROUPS, D_MODEL, N_GROUPS))
    wroute = jnp.pad(jnp.concatenate([er, gr], -1), ((0, 0), (0, 0), (0, LANES - EXPERTS_PER_GROUP - N_GROUPS)))
    w["wroute"] = wroute.astype(BF16)
    broute = jnp.concatenate([b_er.reshape(N_GROUPS, EXPERTS_PER_GROUP),
                              jnp.broadcast_to(b_gr[None], (N_GROUPS, N_GROUPS))], -1)
    w["broute"] = jnp.pad(broute, ((0, 0), (0, LANES - EXPERTS_PER_GROUP - N_GROUPS))).reshape(N_GROUPS, 1, LANES)
    w["w_e_gate"] = w_e_gate.astype(BF16).reshape(N_GROUPS, EXPERTS_PER_GROUP, D_MODEL, D_EXPERT)
    w["w_e_up"] = w_e_up.astype(BF16).reshape(N_GROUPS, EXPERTS_PER_GROUP, D_MODEL, D_EXPERT)
    w["w_e_down"] = w_e_down.astype(BF16).reshape(N_GROUPS, EXPERTS_PER_GROUP * D_EXPERT, D_MODEL)
    w["w_ple_gate"] = w_ple_gate.astype(BF16)
    w["b_ple_gate"] = b_ple_gate.reshape(1, D_MODEL)
    w["w_ple_proj"] = w_ple_proj.astype(BF16)
    return w


def _rope_tables(pos, reps):
    half = QK_ROPE // 2
    inv = jnp.exp(-math.log(ROPE_THETA) * jnp.arange(half, dtype=F32) / half)
    ang = pos.astype(F32)[:, None] * inv[None, :]
    cos2 = jnp.tile(jnp.cos(ang), (1, 2))
    sin2 = jnp.tile(jnp.sin(ang), (1, 2))
    slab = lambda t: jnp.tile(jnp.pad(t, ((0, 0), (QK_NOPE, HEAD_PAD - QK_NOPE - QK_ROPE))), (reps, 1))
    heads = lambda t: jnp.tile(t, (reps, N_HEADS))
    return {"cos128": slab(cos2), "sin128": slab(sin2), "cos_heads": heads(cos2), "sin_heads": heads(sin2)}


def _pack_state(s_re, s_im):
    B = s_re.shape[0]
    st = jnp.stack([s_re.reshape(B, SSM_GROUPS // 2, 2, SSM_STATE), s_im.reshape(B, SSM_GROUPS // 2, 2, SSM_STATE)],
                   axis=2)
    return st.reshape(B, STATE_COLS)


def _unpack_state(s):
    B = s.shape[0]
    st = s.reshape(B, SSM_GROUPS // 2, 2, 2, SSM_STATE)
    return (st[:, :, 0].reshape(B, SSM_GROUPS, SSM_STATE), st[:, :, 1].reshape(B, SSM_GROUPS, SSM_STATE))


def _route_plan(route, counts, *, tm, n_tok):
    g_sel = route[:, 0].astype(jnp.int32)
    rank = route[:, 1].astype(jnp.int32)
    cnt = counts[0, :N_GROUPS].astype(jnp.int32)
    padded = ((cnt + tm - 1) // tm) * tm
    ends = jnp.cumsum(padded)
    offs = ends - padded
    dest = offs[g_sel] + rank
    n_tiles = n_tok // tm + N_GROUPS
    n_valid = ends[-1] // tm
    tile_start = jnp.arange(n_tiles, dtype=jnp.int32) * tm
    tile_grp = jnp.minimum(jnp.sum(tile_start[:, None] >= ends[None, :], axis=1), N_GROUPS - 1).astype(jnp.int32)
    last = jnp.maximum(n_valid - 1, 0)
    tile_blk = jnp.minimum(jnp.arange(n_tiles, dtype=jnp.int32), last)
    tile_grp = jnp.where(jnp.arange(n_tiles) < n_valid, tile_grp, tile_grp[last])
    n_rows = n_tiles * tm
    pad_info = jnp.concatenate([offs + cnt, ends[-1:], padded - cnt, n_rows - ends[-1:]]).astype(jnp.int32)
    return dest, tile_grp, tile_blk, n_valid.reshape(1).astype(jnp.int32), pad_info, n_rows


def _token_tail(ssm, mla, gs, gm, x, p, wts, *, tm, tm_moe):
    T = x.shape[0]
    x1, route, counts = _merge(ssm, mla, gs, gm, x, wts, tm=tm)
    dest, tile_grp, tile_blk, n_valid, pad_info, n_rows = _route_plan(route, counts, tm=tm_moe, n_tok=T)
    xs = _dispatch(x1, dest, pad_info, tm=tm, n_rows=n_rows)
    ys = _moe(xs, tile_grp, tile_blk, n_valid, wts, tm=tm_moe)
    return _final(x1, dest, ys, p, wts, tm=tm)


def kernel(x_prompt, x_sample, p_prompt, p_sample, cache_kv, cache_k_rope, state_ssm_re, state_ssm_im, page_table, w_in, a_re, a_im, log_dt, b_re, b_im, c_re, c_im, d_skip, w_glu, b_glu, w_br_ssm, q_norm_g, w_uq, kv_norm_g, w_uk, w_uv, w_br_mla, w_out, ln1_g, ln1_b, w_gr, b_gr, w_er, b_er, w_e_gate, w_e_up, w_e_down, ln2_g, ln2_b, w_ple_gate, b_ple_gate, w_ple_proj):
    B, S, _ = x_prompt.shape
    Bs, Ss, _ = x_sample.shape
    Tp, Ts = B * S, Bs * Ss
    wts = _prep_weights(w_in[0], a_re[0], a_im[0], log_dt[0], b_re[0], b_im[0], c_re[0], c_im[0], d_skip[0],
                        w_glu[0], b_glu[0], w_br_ssm[0], q_norm_g[0], w_uq[0], kv_norm_g[0], w_uk[0], w_uv[0],
                        w_br_mla[0], w_out[0], ln1_g[0], ln1_b[0], w_gr[0], b_gr[0], w_er[0], b_er[0],
                        w_e_gate[0], w_e_up[0], w_e_down[0], ln2_g[0], ln2_b[0], w_ple_gate[0], b_ple_gate[0],
                        w_ple_proj[0])
    tm = 256
    tm_moe = 256
    tm_tail = 512

    tabs_p = _rope_tables(jnp.arange(S, dtype=jnp.int32), 1)
    seq_tiles = S // tm
    u_p, ckv_p, kr_p, gs_p, gm_p, q3, k3, v3 = _proj(
        x_prompt.reshape(Tp, D_MODEL), wts, tabs_p, tm=tm, seq_tiles=seq_tiles, absorbed=False)
    zero_state = jnp.zeros((B, STATE_COLS), F32)
    ssm_p, sfin_p = _s5(u_p.reshape(B, S, SSM_WIDTH), zero_state, wts, lt=32, bb=B, name="s5_prompt")
    mla_p = _attn_prompt(q3, k3, v3, batch=B, seq=S, tq=512, tk=512)
    y_p = _token_tail(ssm_p.reshape(Tp, SSM_WIDTH), mla_p, gs_p, gm_p, x_prompt.reshape(Tp, D_MODEL),
                      p_prompt.reshape(Tp, PLE_DIM), wts, tm=tm_tail, tm_moe=tm_moe)

    tabs_s = _rope_tables(PAST_LEN + jnp.arange(Ss, dtype=jnp.int32), tm // Ss)
    u_s, ckv_s, kr_s, gs_s, gm_s, q_lat, q_rope = _proj(
        x_sample.reshape(Ts, D_MODEL), wts, tabs_s, tm=tm, seq_tiles=None, absorbed=True)
    s0 = _pack_state(state_ssm_re[0], state_ssm_im[0])
    ssm_s, sfin_s = _s5(u_s.reshape(Bs, Ss, SSM_WIDTH), s0, wts, lt=Ss, bb=64, name="s5_sample")
    ssm_s = ssm_s.reshape(Ts, SSM_WIDTH)
    kr_s32 = kr_s[:, QK_NOPE:QK_NOPE + QK_ROPE]
    n_pool = cache_kv.shape[1]
    o_lat = _attn_sample(page_table, q_lat.reshape(Ts * N_HEADS, KV_RANK), q_rope.reshape(Ts * N_HEADS, QK_ROPE),
                         ckv_s, kr_s32, cache_kv.reshape(n_pool, PAGE_SIZE, KV_RANK),
                         cache_k_rope.reshape(n_pool, PAGE_SIZE, QK_ROPE).transpose(0, 2, 1),
                         batch=Bs, n_new=Ss, cp=64)
    mla_s = _mm(o_lat.reshape(Ts, N_HEADS * KV_RANK), wts["wuv_blockdiag"], tm=tm, out_dtype=BF16, name="uv_sample")
    y_s = _token_tail(ssm_s, mla_s, gs_s, gm_s, x_sample.reshape(Ts, D_MODEL), p_sample.reshape(Ts, PLE_DIM), wts,
                      tm=tm_tail, tm_moe=tm_moe)

    sre_p, sim_p = _unpack_state(sfin_p)
    sre_s, sim_s = _unpack_state(sfin_s)
    kr_p32 = kr_p[:, QK_NOPE:QK_NOPE + QK_ROPE]
    return (y_p.reshape(B, S, D_MODEL), y_s.reshape(Bs, Ss, D_MODEL),
            ckv_p.reshape(1, B, S, KV_RANK), kr_p32.reshape(1, B, S, QK_ROPE),
            sre_p[None], sim_p[None],
            ckv_s.reshape(1, Bs, Ss, KV_RANK), kr_s32.reshape(1, Bs, Ss, QK_ROPE),
            sre_s[None], sim_s[None])
```

```python
import functools
import math

import jax
import jax.numpy as jnp
from jax import lax
from jax.experimental import pallas as pl
from jax.experimental.pallas import tpu as pltpu

F32 = jnp.float32
BF16 = jnp.bfloat16

D_MODEL = 1024
DEPTH = 1
PAST_LEN = 16384
PAGE_SIZE = 128
SSM_WIDTH = 512
SSM_GROUP = 16
SSM_GROUPS = 32
SSM_STATE = 64
STATE_COLS = 2 * SSM_GROUPS * SSM_STATE
N_HEADS = 8
QK_NOPE = 64
QK_ROPE = 32
V_DIM = 64
Q_RANK = 256
KV_RANK = 256
HEAD_PAD = 128
ROPE_THETA = 10000.0
ATTN_SCALE = (QK_NOPE + QK_ROPE) ** -0.5
Q_SCALE = ATTN_SCALE * math.log2(math.e)
N_GROUPS = 4
EXPERTS_PER_GROUP = 8
D_EXPERT = 256
PLE_DIM = 256
LN_EPS = 1e-5
RMS_EPS = 1e-6
ALPHA = (2 * DEPTH) ** 0.25

LANES = 128
MXU_DIM = 256
VMEM_LIMIT = 56 * 1024 * 1024

ATTN_HEADS_PER_BODY = 4
ROW_CHAINS = 4

NEG_INF = float("-inf")


def _params(sem, vmem=VMEM_LIMIT):
    return pltpu.CompilerParams(dimension_semantics=sem, vmem_limit_bytes=vmem)


def _const_spec(shape):
    zeros = (0,) * len(shape)
    return pl.BlockSpec(shape, lambda *_: zeros)


def _dot(a, b):
    return jnp.dot(a, b, preferred_element_type=F32)


def _dot_t(a, b):
    return lax.dot_general(a, b, (((1,), (1,)), ((), ())), preferred_element_type=F32)


def _layer_norm(x, g, b):
    mu = jnp.mean(x, axis=-1, keepdims=True)
    xc = x - mu
    var = jnp.mean(xc * xc, axis=-1, keepdims=True)
    return xc * lax.rsqrt(var + LN_EPS) * g + b


def _rms_norm(x, g):
    ms = jnp.mean(x * x, axis=-1, keepdims=True)
    return x * lax.rsqrt(ms + RMS_EPS) * g


def _proj_kernel(x_ref, w_ref, qg_ref, kg_ref, cos_ref, sin_ref, wq_ref, wqr_ref, *rest, absorbed):
    if absorbed:
        (wuk_ref, wqc_ref, wqcr_ref, cos8_ref, sin8_ref,
         u_ref, ckv_ref, kr_ref, gs_ref, gm_ref, ql_ref, qr_ref) = rest
    else:
        wk_ref, wv_ref, u_ref, ckv_ref, kr_ref, gs_ref, gm_ref, q_ref, k_ref, v_ref = rest
    xb = x_ref[...].astype(BF16)
    u_ref[...] = _dot(xb, w_ref[:, 0:512])
    cq = _rms_norm(_dot(xb, w_ref[:, 512:768]), qg_ref[...])
    ckv = _rms_norm(_dot(xb, w_ref[:, 768:1024]), kg_ref[...])
    ckv_ref[...] = ckv
    gs_ref[...] = jax.nn.sigmoid(_dot(xb, w_ref[:, 1024:2048])).astype(BF16)
    gm_ref[...] = jax.nn.sigmoid(_dot(xb, w_ref[:, 2048:3072])).astype(BF16)
    cos = cos_ref[...]
    sin = sin_ref[...]
    kr = _dot(xb, w_ref[:, 3072:3200]) * cos + _dot(xb, w_ref[:, 3200:3328]) * sin
    kr_ref[...] = kr[:, QK_NOPE:QK_NOPE + QK_ROPE]

    cqb = cq.astype(BF16)
    qa = _dot(cqb, wq_ref[...])
    qb = _dot(cqb, wqr_ref[...])
    lane = lax.broadcasted_iota(jnp.int32, cos.shape, 1)
    cq_tab = (cos + jnp.where(lane < QK_NOPE, 1.0, 0.0)) * Q_SCALE
    sq_tab = sin * Q_SCALE
    ckvb = ckv.astype(BF16)
    if absorbed:
        for h in range(N_HEADS):
            sl = slice(h * HEAD_PAD, (h + 1) * HEAD_PAD)
            qh = (qa[:, sl] * cq_tab + qb[:, sl] * sq_tab).astype(BF16)
            ql_ref[:, h * KV_RANK:(h + 1) * KV_RANK] = _dot(qh, wuk_ref[h]).astype(BF16)
        qc = _dot(cqb, wqc_ref[...]) * cos8_ref[...] + _dot(cqb, wqcr_ref[...]) * sin8_ref[...]
        qr_ref[...] = (qc * Q_SCALE).astype(BF16)
    else:
        kn = _dot(ckvb, wk_ref[...])
        vv = _dot(ckvb, wv_ref[...])
        for h in range(N_HEADS):
            sl = slice(h * HEAD_PAD, (h + 1) * HEAD_PAD)
            q_ref[h] = (qa[:, sl] * cq_tab + qb[:, sl] * sq_tab).astype(BF16)
            k_ref[h] = (kn[:, sl] + kr).astype(BF16)
            v_ref[h] = jnp.where(lane == V_DIM, 1.0, vv[:, sl]).astype(BF16)


def _proj(x, wts, tabs, *, tm, seq_tiles, absorbed):
    T = x.shape[0]
    nt = T // tm
    row = lambda i: (i, 0)
    if seq_tiles is None:
        tab_map = lambda i: (0, 0)
    else:
        tab_map = lambda i: (i % seq_tiles, 0)
    u_shape, u_spec = (T, SSM_WIDTH), pl.BlockSpec((tm, SSM_WIDTH), row)
    in_specs = [
        pl.BlockSpec((tm, D_MODEL), row),
        _const_spec(wts["w_in"].shape),
        _const_spec((1, Q_RANK)), _const_spec((1, KV_RANK)),
        pl.BlockSpec((tm, HEAD_PAD), tab_map), pl.BlockSpec((tm, HEAD_PAD), tab_map),
        _const_spec(wts["wq"].shape), _const_spec(wts["wq_rot"].shape),
    ]
    args = [x, wts["w_in"], wts["q_norm_g"], wts["kv_norm_g"], tabs["cos128"], tabs["sin128"],
            wts["wq"], wts["wq_rot"]]
    out_shape = [jax.ShapeDtypeStruct(u_shape, F32),
                 jax.ShapeDtypeStruct((T, KV_RANK), F32),
                 jax.ShapeDtypeStruct((T, QK_ROPE), F32),
                 jax.ShapeDtypeStruct((T, D_MODEL), BF16),
                 jax.ShapeDtypeStruct((T, D_MODEL), BF16)]
    out_specs = [u_spec, pl.BlockSpec((tm, KV_RANK), row), pl.BlockSpec((tm, QK_ROPE), row),
                 pl.BlockSpec((tm, D_MODEL), row), pl.BlockSpec((tm, D_MODEL), row)]
    if absorbed:
        in_specs += [_const_spec(wts["wuk_abs"].shape), _const_spec(wts["wq_rope"].shape),
                     _const_spec(wts["wq_rope_rot"].shape),
                     pl.BlockSpec((tm, N_HEADS * QK_ROPE), tab_map),
                     pl.BlockSpec((tm, N_HEADS * QK_ROPE), tab_map)]
        args += [wts["wuk_abs"], wts["wq_rope"], wts["wq_rope_rot"], tabs["cos_heads"], tabs["sin_heads"]]
        out_shape += [jax.ShapeDtypeStruct((T, N_HEADS * KV_RANK), BF16),
                      jax.ShapeDtypeStruct((T, N_HEADS * QK_ROPE), BF16)]
        out_specs += [pl.BlockSpec((tm, N_HEADS * KV_RANK), row), pl.BlockSpec((tm, N_HEADS * QK_ROPE), row)]
    else:
        in_specs += [_const_spec(wts["wk"].shape), _const_spec(wts["wv"].shape)]
        args += [wts["wk"], wts["wv"]]
        head = jax.ShapeDtypeStruct((N_HEADS, T, HEAD_PAD), BF16)
        head_spec = pl.BlockSpec((N_HEADS, tm, HEAD_PAD), lambda i: (0, i, 0))
        out_shape += [head, head, head]
        out_specs += [head_spec, head_spec, head_spec]
    return pl.pallas_call(
        functools.partial(_proj_kernel, absorbed=absorbed),
        grid=(nt,), in_specs=in_specs, out_specs=out_specs, out_shape=out_shape,
        compiler_params=_params(("parallel",)),
        name="proj_sample" if absorbed else "proj_prompt",
    )(*args)


def _s5_kernel(u_ref, s0_ref, lam_ref, wb_ref, wc_ref, dskip_ref, wglu_ref, bglu_ref,
               y_ref, sfin_ref, bu_ref, st_ref, tr_ref, *, lt, bb):
    ti = pl.program_id(1)
    rows = lt * bb
    n_tiles = STATE_COLS // MXU_DIM
    n_slabs = SSM_WIDTH // LANES

    @pl.when(ti == 0)
    def _():
        st_ref[...] = s0_ref[...]

    for b in range(bb):
        for g in range(n_slabs):
            tr_ref[g, pl.ds(b, lt, stride=bb), :] = u_ref[b, :, g * LANES:(g + 1) * LANES]
    u = jnp.concatenate([tr_ref[g] for g in range(n_slabs)], axis=1)
    ub = u.astype(BF16)
    for j in range(n_tiles):
        k0 = MXU_DIM * (j // (n_tiles // 2))
        bu_ref[:, j * MXU_DIM:(j + 1) * MXU_DIM] = _dot(ub[:, k0:k0 + MXU_DIM], wb_ref[j])

    jg = max(1, 64 // bb)
    for j0 in range(0, n_tiles, jg):
        lrs = [jnp.broadcast_to(lam_ref[0:1, (j0 + q) * LANES:(j0 + q + 1) * LANES], (bb, LANES)) for q in range(jg)]
        lis = [jnp.broadcast_to(lam_ref[1:2, (j0 + q) * LANES:(j0 + q + 1) * LANES], (bb, LANES)) for q in range(jg)]

        def body(t, carry, j0=j0, lrs=lrs, lis=lis):
            r0 = pl.multiple_of(t * bb, bb)
            new = []
            for q in range(jg):
                c0 = (j0 + q) * MXU_DIM
                sr, si = carry[2 * q], carry[2 * q + 1]
                nr = lrs[q] * sr - lis[q] * si + bu_ref[pl.ds(r0, bb), c0:c0 + LANES]
                ni = lrs[q] * si + lis[q] * sr + bu_ref[pl.ds(r0, bb), c0 + LANES:c0 + 2 * LANES]
                bu_ref[pl.ds(r0, bb), c0:c0 + LANES] = nr
                bu_ref[pl.ds(r0, bb), c0 + LANES:c0 + 2 * LANES] = ni
                new += [nr, ni]
            return tuple(new)

        init = []
        for q in range(jg):
            c0 = (j0 + q) * MXU_DIM
            init += [st_ref[:, c0:c0 + LANES], st_ref[:, c0 + LANES:c0 + 2 * LANES]]
        fin = lax.fori_loop(0, lt, body, tuple(init), unroll=min(lt, 8))
        for q in range(jg):
            c0 = (j0 + q) * MXU_DIM
            st_ref[:, c0:c0 + LANES] = fin[2 * q]
            st_ref[:, c0 + LANES:c0 + 2 * LANES] = fin[2 * q + 1]

    half = STATE_COLS // 2
    y = jnp.concatenate([_dot(bu_ref[:, 0:half].astype(BF16), wc_ref[0]),
                         _dot(bu_ref[:, half:STATE_COLS].astype(BF16), wc_ref[1])], axis=1)
    y = y + dskip_ref[...] * u
    z = jax.nn.gelu(y, approximate=True)
    gate = jax.nn.sigmoid(_dot(z.astype(BF16), wglu_ref[...]) + bglu_ref[...])
    out = z * gate
    for g in range(n_slabs):
        tr_ref[g] = out[:, g * LANES:(g + 1) * LANES]
    for b in range(bb):
        y_ref[b] = jnp.concatenate([tr_ref[g, pl.ds(b, lt, stride=bb), :] for g in range(n_slabs)],
                                   axis=1).astype(BF16)

    @pl.when(ti == pl.num_programs(1) - 1)
    def _():
        sfin_ref[...] = st_ref[...]


def _s5(u3, s0, wts, *, lt, bb, name):
    B, S, _ = u3.shape
    return pl.pallas_call(
        functools.partial(_s5_kernel, lt=lt, bb=bb),
        grid=(B // bb, S // lt),
        in_specs=[pl.BlockSpec((bb, lt, SSM_WIDTH), lambda b, t: (b, t, 0)),
                  pl.BlockSpec((bb, STATE_COLS), lambda b, t: (b, 0)),
                  _const_spec(wts["lam"].shape), _const_spec(wts["s5_wb"].shape),
                  _const_spec(wts["s5_wc"].shape), _const_spec((1, SSM_WIDTH)),
                  _const_spec(wts["w_glu"].shape), _const_spec((1, SSM_WIDTH))],
        out_specs=[pl.BlockSpec((bb, lt, SSM_WIDTH), lambda b, t: (b, t, 0)),
                   pl.BlockSpec((bb, STATE_COLS), lambda b, t: (b, 0))],
        out_shape=[jax.ShapeDtypeStruct((B, S, SSM_WIDTH), BF16),
                   jax.ShapeDtypeStruct((B, STATE_COLS), F32)],
        scratch_shapes=[pltpu.VMEM((lt * bb, STATE_COLS), F32), pltpu.VMEM((bb, STATE_COLS), F32),
                        pltpu.VMEM((SSM_WIDTH // LANES, lt * bb, LANES), F32)],
        compiler_params=_params(("parallel", "arbitrary")),
        name=name,
    )(u3, s0, wts["lam"], wts["s5_wb"], wts["s5_wc"], wts["d_skip"], wts["w_glu"], wts["b_glu"])


def _attn_prompt_kernel(q_ref, k_ref, v_ref, o_ref, *, tq, tk):
    qi = pl.program_id(1)
    n_diag = tq // tk
    row = lax.broadcasted_iota(jnp.int32, (tq, tk), 0)
    col = lax.broadcasted_iota(jnp.int32, (tq, tk), 1)
    n_full = qi * n_diag
    lane = lax.broadcasted_iota(jnp.int32, (tq, HEAD_PAD), 1)
    pairs = []
    for h0 in range(0, N_HEADS, ATTN_HEADS_PER_BODY):
        heads = tuple(range(h0, h0 + ATTN_HEADS_PER_BODY))
        qs = [q_ref[h] for h in heads]

        def step(kt, carry, diag=None, heads=heads, qs=qs):
            k0 = pl.multiple_of(kt * tk, tk)
            new = []
            for j, h in enumerate(heads):
                m, acc = carry[2 * j], carry[2 * j + 1]
                s = _dot_t(qs[j], k_ref[h, pl.ds(k0, tk), :])
                if diag is not None:
                    s = jnp.where(col + diag * tk <= row, s, NEG_INF)
                m_new = jnp.maximum(m, jnp.max(s, axis=1, keepdims=True))
                p = jnp.exp2(s - m_new).astype(BF16)
                acc = jnp.exp2(m - m_new) * acc + _dot(p, v_ref[h, pl.ds(k0, tk), :])
                new += [m_new, acc]
            return tuple(new)

        carry = (jnp.full((tq, 1), NEG_INF, F32), jnp.zeros((tq, HEAD_PAD), F32)) * ATTN_HEADS_PER_BODY
        carry = lax.fori_loop(0, n_full, step, carry)
        for d in range(n_diag):
            carry = step(n_full + d, carry, diag=d)
        outs = [jnp.where(lane < V_DIM, acc / acc[:, V_DIM:V_DIM + 1], 0.0) for acc in carry[1::2]]
        pairs += [outs[j] + pltpu.roll(outs[j + 1], V_DIM, axis=1) for j in range(0, ATTN_HEADS_PER_BODY, 2)]
    o_ref[...] = jnp.concatenate(pairs, axis=1).astype(BF16)


def _attn_prompt(q3, k3, v3, *, batch, seq, tq, tk):
    nq = seq // tq
    T = batch * seq
    return pl.pallas_call(
        functools.partial(_attn_prompt_kernel, tq=tq, tk=tk),
        grid=(batch, nq),
        in_specs=[pl.BlockSpec((N_HEADS, tq, HEAD_PAD), lambda b, i: (0, b * nq + i, 0)),
                  pl.BlockSpec((N_HEADS, seq, HEAD_PAD), lambda b, i: (0, b, 0)),
                  pl.BlockSpec((N_HEADS, seq, HEAD_PAD), lambda b, i: (0, b, 0))],
        out_specs=pl.BlockSpec((tq, N_HEADS * V_DIM), lambda b, i: (b * nq + i, 0)),
        out_shape=jax.ShapeDtypeStruct((T, N_HEADS * V_DIM), BF16),
        compiler_params=_params(("parallel", "arbitrary")),
        name="attn_prompt",
    )(q3, k3, v3)


def _attn_sample_kernel(pt_ref, ql_ref, qr_ref, qlp_ref, qrp_ref, cn_ref, krn_ref, kv_hbm, kr_hbm, o_ref,
                        kvbuf, krbuf, kv_sem, kr_sem, kvb_ref, s_ref, m_ref, l_ref, acc_ref,
                        *, cp, n_chunks, n_steps, n_new):
    step = pl.program_id(0)
    slot = lax.rem(step, 2)

    def page_copies(chunk, sl):
        b = lax.div(chunk, n_chunks)
        c = lax.rem(chunk, n_chunks)
        out = []
        for k in range(cp):
            page = pt_ref[b, c * cp + k]
            out.append(pltpu.make_async_copy(kv_hbm.at[page], kvbuf.at[sl, k], kv_sem.at[sl]))
            out.append(pltpu.make_async_copy(kr_hbm.at[page], krbuf.at[sl, k], kr_sem.at[sl]))
        return out

    @pl.when(step == 0)
    def _():
        for d in page_copies(0, 0):
            d.start()

    @pl.when(step + 1 < n_steps)
    def _():
        for d in page_copies(step + 1, 1 - slot):
            d.start()

    @pl.when(step < n_steps)
    def _():
        for d in page_copies(step, slot):
            d.wait()

    def reset_state():
        m_ref[...] = jnp.full(m_ref.shape, NEG_INF, F32)
        l_ref[...] = jnp.zeros(l_ref.shape, F32)
        acc_ref[...] = jnp.zeros(acc_ref.shape, F32)

    @pl.when(step == 0)
    def _():
        reset_state()
        kvb_ref[1] = jnp.zeros(kvb_ref.shape[1:], BF16)
        s_ref[1] = jnp.zeros(s_ref.shape[1:], F32)

    ql = ql_ref[...]
    qr = qr_ref[...]

    def update(state, s, vals):
        m, l, acc = state
        m_new = jnp.maximum(m, jnp.max(s, axis=1, keepdims=True))
        alpha = jnp.exp2(m - m_new)
        p = jnp.exp2(s - m_new)
        return (m_new, alpha * l + jnp.sum(p, axis=1, keepdims=True),
                alpha * acc + _dot(p.astype(BF16), vals))

    def fold(sl):
        state = update((m_ref[...], l_ref[...], acc_ref[...]), s_ref[sl], kvb_ref[sl])
        m_ref[...], l_ref[...], acc_ref[...] = state
        return state

    def body(sl):
        kv = kvbuf[sl].reshape(cp * PAGE_SIZE, KV_RANK).astype(BF16)
        kr_t = jnp.concatenate([krbuf[sl, k] for k in range(cp)], axis=1).astype(BF16)
        kvb_ref[sl] = kv
        s_ref[sl] = _dot_t(ql, kv) + _dot(qr, kr_t)
        fold(1 - sl)

    for sl in range(2):
        pl.when(slot == sl)(functools.partial(body, sl))

    @pl.when(lax.rem(step, n_chunks) == 0)
    def _():
        @pl.when(step > 0)
        def _():
            qlp = qlp_ref[...]
            qrp = qrp_ref[...]
            cn = cn_ref[...].astype(BF16)
            krn = krn_ref[...].astype(BF16)
            s = _dot_t(qlp, cn) + _dot_t(qrp, krn)
            rows = qlp.shape[0]
            t_row = jnp.right_shift(lax.broadcasted_iota(jnp.int32, (rows, n_new), 0), 3)
            t_col = lax.broadcasted_iota(jnp.int32, (rows, n_new), 1)
            state = (m_ref[...], l_ref[...], acc_ref[...])
            _, l, acc = update(state, jnp.where(t_col <= t_row, s, NEG_INF), cn)
            o_ref[...] = (acc / l).astype(BF16)

        reset_state()


def _attn_sample(page_table, q_lat, q_rope, ckv_new, kr_new, cache_kv, cache_kr_t, *, batch, n_new, cp):
    rows = n_new * N_HEADS
    n_chunks = page_table.shape[1] // cp
    n_steps = batch * n_chunks
    cur = lambda t, pt: (jnp.minimum(t, n_steps - 1) // n_chunks, 0)
    prev = lambda t, pt: (jnp.maximum(t - 1, 0) // n_chunks, 0)
    in_specs = [pl.BlockSpec((rows, KV_RANK), cur), pl.BlockSpec((rows, QK_ROPE), cur),
                pl.BlockSpec((rows, KV_RANK), prev), pl.BlockSpec((rows, QK_ROPE), prev),
                pl.BlockSpec((n_new, KV_RANK), prev), pl.BlockSpec((n_new, QK_ROPE), prev),
                pl.BlockSpec(memory_space=pl.ANY), pl.BlockSpec(memory_space=pl.ANY)]
    grid_spec = pltpu.PrefetchScalarGridSpec(
        num_scalar_prefetch=1, grid=(n_steps + 1,), in_specs=in_specs,
        out_specs=pl.BlockSpec((rows, KV_RANK), prev),
        scratch_shapes=[pltpu.VMEM((2, cp, PAGE_SIZE, KV_RANK), F32), pltpu.VMEM((2, cp, QK_ROPE, PAGE_SIZE), F32),
                        pltpu.SemaphoreType.DMA((2,)), pltpu.SemaphoreType.DMA((2,)),
                        pltpu.VMEM((2, cp * PAGE_SIZE, KV_RANK), BF16), pltpu.VMEM((2, rows, cp * PAGE_SIZE), F32),
                        pltpu.VMEM((rows, 1), F32), pltpu.VMEM((rows, 1), F32), pltpu.VMEM((rows, KV_RANK), F32)])
    return pl.pallas_call(
        functools.partial(_attn_sample_kernel, cp=cp, n_chunks=n_chunks, n_steps=n_steps, n_new=n_new),
        grid_spec=grid_spec,
        out_shape=jax.ShapeDtypeStruct((batch * rows, KV_RANK), BF16),
        compiler_params=_params(("arbitrary",)),
        name="attn_sample",
    )(page_table, q_lat, q_rope, q_lat, q_rope, ckv_new, kr_new, cache_kv, cache_kr_t)


def _mm_kernel(x_ref, w_ref, o_ref):
    o_ref[...] = _dot(x_ref[...], w_ref[...]).astype(o_ref.dtype)


def _mm(x, w, *, tm, out_dtype, name):
    M, K = x.shape
    N = w.shape[1]
    return pl.pallas_call(
        _mm_kernel, grid=(M // tm,),
        in_specs=[pl.BlockSpec((tm, K), lambda i: (i, 0)), _const_spec((K, N))],
        out_specs=pl.BlockSpec((tm, N), lambda i: (i, 0)),
        out_shape=jax.ShapeDtypeStruct((M, N), out_dtype),
        compiler_params=_params(("parallel",)), name=name,
    )(x, w)


def _merge_kernel(ssm_ref, mla_ref, gs_ref, gm_ref, x_ref, wbs_ref, wbm_ref, wo_ref, g_ref, b_ref,
                  wr_ref, br_ref, x1_ref, route_ref, cnt_ref, run_ref, *, tm):
    i = pl.program_id(0)

    @pl.when(i == 0)
    def _():
        run_ref[...] = jnp.zeros(run_ref.shape, F32)

    hm = tm // ROW_CHAINS
    lane = lax.broadcasted_iota(jnp.int32, (hm, LANES), 1).astype(F32)
    r_i = lax.broadcasted_iota(jnp.int32, (hm, hm), 0)
    c_i = lax.broadcasted_iota(jnp.int32, (hm, hm), 1)
    before = jnp.where(c_i < r_i, 1.0, 0.0).astype(BF16)
    run = run_ref[0:1, :]
    for c in range(ROW_CHAINS):
        rows = slice(c * hm, (c + 1) * hm)
        a = _dot(ssm_ref[rows, :], wbs_ref[...])
        m = _dot(mla_ref[rows, :], wbm_ref[...])
        merged = gs_ref[rows, :].astype(F32) * a + gm_ref[rows, :].astype(F32) * m
        y = _dot(merged.astype(BF16), wo_ref[...])
        x1 = _layer_norm(ALPHA * x_ref[rows, :] + y, g_ref[...], b_ref[...])
        x1_ref[rows, :] = x1

        logits = _dot(x1.astype(BF16), wr_ref[...]) + br_ref[...]
        gl = jnp.where(lane < N_GROUPS, logits, NEG_INF)
        gmax = jnp.max(gl, axis=1, keepdims=True)
        g_sel = jnp.min(jnp.where(gl == gmax, lane, float(LANES)), axis=1, keepdims=True)
        onehot = jnp.where(lane == g_sel, 1.0, 0.0)
        rank_all = _dot(before, onehot.astype(BF16)) + run
        rank = jnp.sum(onehot * rank_all, axis=1, keepdims=True)
        route_ref[rows, :] = jnp.where(lane == 0.0, g_sel, jnp.where(lane == 1.0, rank, 0.0))
        run = run + jnp.sum(onehot, axis=0, keepdims=True)
    run_ref[0:1, :] = run

    @pl.when(i == pl.num_programs(0) - 1)
    def _():
        cnt_ref[...] = run_ref[...]


def _merge(ssm, mla, gs, gm, x, wts, *, tm):
    T = x.shape[0]
    row = lambda i: (i, 0)
    ssm_spec = pl.BlockSpec((tm, SSM_WIDTH), row)
    return pl.pallas_call(
        functools.partial(_merge_kernel, tm=tm),
        grid=(T // tm,),
        in_specs=[ssm_spec, pl.BlockSpec((tm, N_HEADS * V_DIM), row),
                  pl.BlockSpec((tm, D_MODEL), row), pl.BlockSpec((tm, D_MODEL), row),
                  pl.BlockSpec((tm, D_MODEL), row),
                  _const_spec(wts["w_br_ssm"].shape), _const_spec(wts["w_br_mla"].shape),
                  _const_spec(wts["w_out"].shape), _const_spec((1, D_MODEL)), _const_spec((1, D_MODEL)),
                  _const_spec((D_MODEL, LANES)), _const_spec((1, LANES))],
        out_specs=[pl.BlockSpec((tm, D_MODEL), row), pl.BlockSpec((tm, LANES), row),
                   _const_spec((8, LANES))],
        out_shape=[jax.ShapeDtypeStruct((T, D_MODEL), F32), jax.ShapeDtypeStruct((T, LANES), F32),
                   jax.ShapeDtypeStruct((8, LANES), F32)],
        scratch_shapes=[pltpu.VMEM((8, LANES), F32)],
        compiler_params=_params(("arbitrary",)),
        name="merge_ln1_route",
    )(ssm, mla, gs, gm, x, wts["w_br_ssm"], wts["w_br_mla"], wts["w_out"], wts["ln1_g"], wts["ln1_b"],
      wts["wrg"], wts["brg"])


def _dispatch_kernel(pad_ref, x_ref, dest_ref, xs_ref, zero_ref, sem, *, tm):
    i = pl.program_id(0)

    def row_copy(r, d):
        return pltpu.make_async_copy(x_ref.at[pl.ds(r, 1)], xs_ref.at[pl.ds(d, 1)], sem)

    def zero_copy(d):
        return pltpu.make_async_copy(zero_ref.at[pl.ds(0, 1)], xs_ref.at[pl.ds(d, 1)], sem)

    @pl.when(i == 0)
    def _():
        zero_ref[...] = jnp.zeros(zero_ref.shape, F32)
        for g in range(N_GROUPS + 1):
            start = pad_ref[g]
            n = pad_ref[N_GROUPS + 1 + g]

            def issue(r, c, start=start):
                zero_copy(start + r).start()
                return c

            def drain(r, c, start=start):
                zero_copy(start + r).wait()
                return c

            lax.fori_loop(0, n, issue, 0)
            lax.fori_loop(0, n, drain, 0)

    for r in range(tm):
        row_copy(r, dest_ref[0, 0, r]).start(priority=r % 2)
    pltpu.make_async_copy(x_ref, xs_ref.at[pl.ds(0, tm)], sem).wait()


def _dispatch(x1, dest, pad_info, *, tm, n_rows):
    T = x1.shape[0]
    nt = T // tm
    grid_spec = pltpu.PrefetchScalarGridSpec(
        num_scalar_prefetch=1, grid=(nt,),
        in_specs=[pl.BlockSpec((tm, D_MODEL), lambda i, pad: (i, 0)),
                  pl.BlockSpec((1, 1, tm), lambda i, pad: (i, 0, 0), memory_space=pltpu.SMEM)],
        out_specs=pl.BlockSpec(memory_space=pl.ANY),
        scratch_shapes=[pltpu.VMEM((8, D_MODEL), F32), pltpu.SemaphoreType.DMA(())])
    return pl.pallas_call(
        functools.partial(_dispatch_kernel, tm=tm),
        grid_spec=grid_spec,
        out_shape=jax.ShapeDtypeStruct((n_rows, D_MODEL), F32),
        compiler_params=_params(("arbitrary",)),
        name="moe_dispatch",
    )(pad_info, x1, dest.reshape(nt, 1, tm))


def _moe_kernel(grp_ref, blk_ref, nv_ref, x_ref, wr_ref, br_ref, wg_ref, wu_ref, wd_ref, o_ref, *, tm):
    del blk_ref
    i = pl.program_id(0)

    @pl.when(i < nv_ref[0])
    def _():
        g = grp_ref[i]
        x = x_ref[...]
        xb = x.astype(BF16)
        logits = _dot(xb, wr_ref[0]) + br_ref[0]
        lane = lax.broadcasted_iota(jnp.int32, (tm, LANES), 1).astype(F32)
        is_grp = (lane >= EXPERTS_PER_GROUP) & (lane < EXPERTS_PER_GROUP + N_GROUPS)
        gl = jnp.where(is_grp, logits, NEG_INF)
        gmax = jnp.max(gl, axis=1, keepdims=True)
        gexp = jnp.exp(gl - gmax)
        p_group = (jnp.sum(jnp.where(lane == (EXPERTS_PER_GROUP + g).astype(F32), gexp, 0.0), axis=1, keepdims=True)
                   / jnp.sum(gexp, axis=1, keepdims=True))
        el = jnp.where(lane < EXPERTS_PER_GROUP, logits, NEG_INF)
        v1 = jnp.max(el, axis=1, keepdims=True)
        i1 = jnp.min(jnp.where(el == v1, lane, float(LANES)), axis=1, keepdims=True)
        el2 = jnp.where(lane == i1, NEG_INF, el)
        v2 = jnp.max(el2, axis=1, keepdims=True)
        i2 = jnp.min(jnp.where(el2 == v2, lane, float(LANES)), axis=1, keepdims=True)
        e2 = jnp.exp(v2 - v1)
        w1 = p_group / (1.0 + e2)
        w2 = w1 * e2
        gates = jnp.where(lane == i1, w1, jnp.where(lane == i2, w2, 0.0))

        hs = []
        for e in range(EXPERTS_PER_GROUP):
            hg = _dot(xb, wg_ref[0, e])
            hu = _dot(xb, wu_ref[0, e])
            ge = jnp.sum(jnp.where(lane == float(e), gates, 0.0), axis=1, keepdims=True)
            hs.append((jax.nn.silu(hg) * hu * ge).astype(BF16))
        o_ref[...] = _dot(jnp.concatenate(hs, axis=1), wd_ref[0])

    @pl.when(i >= nv_ref[0])
    def _():
        o_ref[...] = jnp.zeros(o_ref.shape, F32)


def _moe(xs, tile_grp, tile_blk, n_valid, wts, *, tm):
    n_rows = xs.shape[0]
    nt = n_rows // tm
    grp3 = lambda i, grp, blk, nv: (grp[i], 0, 0)
    grp4 = lambda i, grp, blk, nv: (grp[i], 0, 0, 0)
    rows = lambda i, grp, blk, nv: (blk[i], 0)
    grid_spec = pltpu.PrefetchScalarGridSpec(
        num_scalar_prefetch=3, grid=(nt,),
        in_specs=[pl.BlockSpec((tm, D_MODEL), rows),
                  pl.BlockSpec((1, D_MODEL, LANES), grp3), pl.BlockSpec((1, 1, LANES), grp3),
                  pl.BlockSpec((1, EXPERTS_PER_GROUP, D_MODEL, D_EXPERT), grp4),
                  pl.BlockSpec((1, EXPERTS_PER_GROUP, D_MODEL, D_EXPERT), grp4),
                  pl.BlockSpec((1, EXPERTS_PER_GROUP * D_EXPERT, D_MODEL), grp3)],
        out_specs=pl.BlockSpec((tm, D_MODEL), lambda i, grp, blk, nv: (i, 0)))
    return pl.pallas_call(
        functools.partial(_moe_kernel, tm=tm),
        grid_spec=grid_spec,
        out_shape=jax.ShapeDtypeStruct((n_rows, D_MODEL), F32),
        compiler_params=_params(("arbitrary",)),
        name="moe_experts",
    )(tile_grp, tile_blk, n_valid, xs, wts["wroute"], wts["broute"],
      wts["w_e_gate"], wts["w_e_up"], wts["w_e_down"])


def _final_kernel(x1_ref, dest_ref, dest_next_ref, ys_ref, p_ref, g_ref, b_ref, wpg_ref, bpg_ref, wpp_ref, o_ref,
                  moe_ref, sem, *, tm):
    i = pl.program_id(0)
    slot = lax.rem(i, 2)

    def gather(d_ref, sl):
        for r in range(tm):
            pltpu.make_async_copy(ys_ref.at[pl.ds(d_ref[0, 0, r], 1)], moe_ref.at[sl, pl.ds(r, 1)],
                                  sem.at[sl]).start(priority=r % 2)

    @pl.when(i == 0)
    def _():
        gather(dest_ref, 0)

    @pl.when(i + 1 < pl.num_programs(0))
    def _():
        gather(dest_next_ref, 1 - slot)

    pltpu.make_async_copy(ys_ref.at[pl.ds(0, tm)], moe_ref.at[slot], sem.at[slot]).wait()
    hm = tm // ROW_CHAINS
    for c in range(ROW_CHAINS):
        rows = slice(c * hm, (c + 1) * hm)
        x2 = _layer_norm(ALPHA * x1_ref[rows, :] + moe_ref[slot, rows, :], g_ref[...], b_ref[...])
        gate = jax.nn.sigmoid(_dot(x2.astype(BF16), wpg_ref[...]) + bpg_ref[...])
        o_ref[rows, :] = x2 + gate * _dot(p_ref[rows, :].astype(BF16), wpp_ref[...])


def _final(x1, dest, ys, p, wts, *, tm):
    T = x1.shape[0]
    nt = T // tm
    row = lambda i: (i, 0)
    return pl.pallas_call(
        functools.partial(_final_kernel, tm=tm),
        grid=(nt,),
        in_specs=[pl.BlockSpec((tm, D_MODEL), row),
                  pl.BlockSpec((1, 1, tm), lambda i: (i, 0, 0), memory_space=pltpu.SMEM),
                  pl.BlockSpec((1, 1, tm), lambda i: (jnp.minimum(i + 1, nt - 1), 0, 0), memory_space=pltpu.SMEM),
                  pl.BlockSpec(memory_space=pl.ANY),
                  pl.BlockSpec((tm, PLE_DIM), row),
                  _const_spec((1, D_MODEL)), _const_spec((1, D_MODEL)),
                  _const_spec((D_MODEL, D_MODEL)), _const_spec((1, D_MODEL)), _const_spec((PLE_DIM, D_MODEL))],
        out_specs=pl.BlockSpec((tm, D_MODEL), row),
        out_shape=jax.ShapeDtypeStruct((T, D_MODEL), F32),
        scratch_shapes=[pltpu.VMEM((2, tm, D_MODEL), F32), pltpu.SemaphoreType.DMA((2,))],
        compiler_params=_params(("arbitrary",)),
        name="combine_ln2_ple",
    )(x1, dest.reshape(nt, 1, tm), dest.reshape(nt, 1, tm), ys, p, wts["ln2_g"], wts["ln2_b"], wts["w_ple_gate"], wts["b_ple_gate"],
      wts["w_ple_proj"])


def _prep_weights(w_in, a_re, a_im, log_dt, b_re, b_im, c_re, c_im, d_skip, w_glu, b_glu, w_br_ssm,
                  q_norm_g, w_uq, kv_norm_g, w_uk, w_uv, w_br_mla, w_out, ln1_g, ln1_b,
                  w_gr, b_gr, w_er, b_er, w_e_gate, w_e_up, w_e_down, ln2_g, ln2_b,
                  w_ple_gate, b_ple_gate, w_ple_proj):
    half = QK_ROPE // 2
    rot = lambda w: jnp.concatenate([-w[..., half:], w[..., :half]], axis=-1)
    w = {}

    u_w, cq_w, ckv_w, kr_w, gs_w, gm_w = jnp.split(
        w_in, (512, 768, 1024, 1024 + QK_ROPE, 1024 + QK_ROPE + D_MODEL), axis=1)
    slab = lambda c: jnp.pad(c, ((0, 0), (QK_NOPE, HEAD_PAD - QK_NOPE - QK_ROPE)))
    w["w_in"] = jnp.concatenate([u_w, cq_w, ckv_w, gs_w, gm_w, slab(kr_w), slab(rot(kr_w))], axis=1).astype(BF16)
    w["q_norm_g"] = q_norm_g.reshape(1, Q_RANK)
    w["kv_norm_g"] = kv_norm_g.reshape(1, KV_RANK)

    wq3 = w_uq.reshape(Q_RANK, N_HEADS, QK_NOPE + QK_ROPE)
    nope, rope = wq3[..., :QK_NOPE], wq3[..., QK_NOPE:]
    pad_tail = jnp.zeros((Q_RANK, N_HEADS, HEAD_PAD - QK_NOPE - QK_ROPE), F32)
    w["wq"] = jnp.concatenate([nope, rope, pad_tail], -1).reshape(Q_RANK, N_HEADS * HEAD_PAD).astype(BF16)
    w["wq_rot"] = jnp.concatenate([jnp.zeros_like(nope), rot(rope), pad_tail], -1).reshape(
        Q_RANK, N_HEADS * HEAD_PAD).astype(BF16)
    w["wq_rope"] = rope.reshape(Q_RANK, N_HEADS * QK_ROPE).astype(BF16)
    w["wq_rope_rot"] = rot(rope).reshape(Q_RANK, N_HEADS * QK_ROPE).astype(BF16)
    head_pad = lambda a: jnp.pad(a, ((0, 0), (0, 0), (0, HEAD_PAD - a.shape[-1]))).reshape(
        a.shape[0], N_HEADS * HEAD_PAD).astype(BF16)
    w["wk"] = head_pad(w_uk)
    w["wv"] = head_pad(w_uv)
    w["wuk_abs"] = jnp.pad(w_uk.transpose(1, 2, 0), ((0, 0), (0, HEAD_PAD - QK_NOPE), (0, 0))).astype(BF16)
    w["wuv_blockdiag"] = (jnp.eye(N_HEADS, dtype=F32)[:, None, :, None]
                          * w_uv.transpose(1, 0, 2)[:, :, None, :]).reshape(
                              N_HEADS * KV_RANK, N_HEADS * V_DIM).astype(BF16)

    dt = jnp.exp(log_dt)[:, None]
    mag = jnp.exp(dt * a_re)
    lr = mag * jnp.cos(dt * a_im)
    li = mag * jnp.sin(dt * a_im)
    den = a_re * a_re + a_im * a_im
    fr = ((lr - 1.0) * a_re + li * a_im) / den
    fi = (li * a_re - (lr - 1.0) * a_im) / den
    bbr = fr[..., None] * b_re - fi[..., None] * b_im
    bbi = fr[..., None] * b_im + fi[..., None] * b_re
    w["lam"] = jnp.stack([lr.reshape(-1), li.reshape(-1)])
    n_tiles = SSM_GROUPS // 2
    col_group = jnp.broadcast_to((2 * jnp.arange(n_tiles)[:, None, None, None]
                                  + jnp.arange(2)[None, None, :, None]), (n_tiles, 2, 2, SSM_STATE)).reshape(-1)
    ch_group = jnp.arange(SSM_WIDTH) // SSM_GROUP
    mask = ch_group[:, None] == col_group[None, :]
    bcols = jnp.stack([bbr, bbi]).reshape(2, n_tiles, 2, SSM_STATE, SSM_GROUP).transpose(
        1, 0, 2, 3, 4).reshape(STATE_COLS, SSM_GROUP)
    bmat = jnp.where(mask, jnp.tile(bcols.T, (SSM_GROUPS, 1)), 0.0)
    bmat4 = bmat.reshape(2, MXU_DIM, n_tiles, MXU_DIM)
    w["s5_wb"] = jnp.stack([bmat4[j // (n_tiles // 2), :, j, :] for j in range(n_tiles)]).astype(BF16)
    ccols = jnp.stack([c_re, -c_im]).reshape(2, n_tiles, 2, SSM_GROUP, SSM_STATE).transpose(
        1, 0, 2, 4, 3).reshape(STATE_COLS, SSM_GROUP)
    cmat = jnp.where(mask.T, jnp.tile(ccols, (1, SSM_GROUPS)), 0.0)
    hc = STATE_COLS // 2
    w["s5_wc"] = jnp.stack([cmat[:hc, :MXU_DIM], cmat[hc:, MXU_DIM:]]).astype(BF16)
    w["d_skip"] = d_skip.reshape(1, SSM_WIDTH)
    w["w_glu"] = w_glu.astype(BF16)
    w["b_glu"] = b_glu.reshape(1, SSM_WIDTH)

    w["w_br_ssm"] = w_br_ssm.astype(BF16)
    w["w_br_mla"] = w_br_mla.astype(BF16)
    w["w_out"] = w_out.astype(BF16)
    w["ln1_g"], w["ln1_b"] = ln1_g.reshape(1, D_MODEL), ln1_b.reshape(1, D_MODEL)
    w["ln2_g"], w["ln2_b"] = ln2_g.reshape(1, D_MODEL), ln2_b.reshape(1, D_MODEL)

    wrg = jnp.pad(w_gr, ((0, 0), (0, LANES - N_GROUPS)))
    w["wrg"] = wrg.astype(BF16)
    w["brg"] = jnp.pad(b_gr, (0, LANES - N_GROUPS)).reshape(1, LANES)
    er = w_er.reshape(D_MODEL, N_GROUPS, EXPERTS_PER_GROUP).transpose(1, 0, 2)
    gr = jnp.broadcast_to(w_gr[None], (N_GROUPS, D_MODEL, N_GROUPS))
    wroute = jnp.pad(jnp.concatenate([er, gr], -1), ((0, 0), (0, 0), (0, LANES - EXPERTS_PER_GROUP - N_GROUPS)))
    w["wroute"] = wroute.astype(BF16)
    broute = jnp.concatenate([b_er.reshape(N_GROUPS, EXPERTS_PER_GROUP),
                              jnp.broadcast_to(b_gr[None], (N_GROUPS, N_GROUPS))], -1)
    w["broute"] = jnp.pad(broute, ((0, 0), (0, LANES - EXPERTS_PER_GROUP - N_GROUPS))).reshape(N_GROUPS, 1, LANES)
    w["w_e_gate"] = w_e_gate.astype(BF16).reshape(N_GROUPS, EXPERTS_PER_GROUP, D_MODEL, D_EXPERT)
    w["w_e_up"] = w_e_up.astype(BF16).reshape(N_GROUPS, EXPERTS_PER_GROUP, D_MODEL, D_EXPERT)
    w["w_e_down"] = w_e_down.astype(BF16).reshape(N_GROUPS, EXPERTS_PER_GROUP * D_EXPERT, D_MODEL)
    w["w_ple_gate"] = w_ple_gate.astype(BF16)
    w["b_ple_gate"] = b_ple_gate.reshape(1, D_MODEL)
    w["w_ple_proj"] = w_ple_proj.astype(BF16)
    return w


def _rope_tables(pos, reps):
    half = QK_ROPE // 2
    inv = jnp.exp(-math.log(ROPE_THETA) * jnp.arange(half, dtype=F32) / half)
    ang = pos.astype(F32)[:, None] * inv[None, :]
    cos2 = jnp.tile(jnp.cos(ang), (1, 2))
    sin2 = jnp.tile(jnp.sin(ang), (1, 2))
    slab = lambda t: jnp.tile(jnp.pad(t, ((0, 0), (QK_NOPE, HEAD_PAD - QK_NOPE - QK_ROPE))), (reps, 1))
    heads = lambda t: jnp.tile(t, (reps, N_HEADS))
    return {"cos128": slab(cos2), "sin128": slab(sin2), "cos_heads": heads(cos2), "sin_heads": heads(sin2)}


def _pack_state(s_re, s_im):
    B = s_re.shape[0]
    st = jnp.stack([s_re.reshape(B, SSM_GROUPS // 2, 2, SSM_STATE), s_im.reshape(B, SSM_GROUPS // 2, 2, SSM_STATE)],
                   axis=2)
    return st.reshape(B, STATE_COLS)


def _unpack_state(s):
    B = s.shape[0]
    st = s.reshape(B, SSM_GROUPS // 2, 2, 2, SSM_STATE)
    return (st[:, :, 0].reshape(B, SSM_GROUPS, SSM_STATE), st[:, :, 1].reshape(B, SSM_GROUPS, SSM_STATE))


def _route_plan(route, counts, *, tm, n_tok):
    g_sel = route[:, 0].astype(jnp.int32)
    rank = route[:, 1].astype(jnp.int32)
    cnt = counts[0, :N_GROUPS].astype(jnp.int32)
    padded = ((cnt + tm - 1) // tm) * tm
    ends = jnp.cumsum(padded)
    offs = ends - padded
    dest = offs[g_sel] + rank
    n_tiles = n_tok // tm + N_GROUPS
    n_valid = ends[-1] // tm
    tile_start = jnp.arange(n_tiles, dtype=jnp.int32) * tm
    tile_grp = jnp.minimum(jnp.sum(tile_start[:, None] >= ends[None, :], axis=1), N_GROUPS - 1).astype(jnp.int32)
    last = jnp.maximum(n_valid - 1, 0)
    tile_blk = jnp.minimum(jnp.arange(n_tiles, dtype=jnp.int32), last)
    tile_grp = jnp.where(jnp.arange(n_tiles) < n_valid, tile_grp, tile_grp[last])
    n_rows = n_tiles * tm
    pad_info = jnp.concatenate([offs + cnt, ends[-1:], padded - cnt, n_rows - ends[-1:]]).astype(jnp.int32)
    return dest, tile_grp, tile_blk, n_valid.reshape(1).astype(jnp.int32), pad_info, n_rows


def _token_tail(ssm, mla, gs, gm, x, p, wts, *, tm, tm_moe):
    T = x.shape[0]
    x1, route, counts = _merge(ssm, mla, gs, gm, x, wts, tm=tm)
    dest, tile_grp, tile_blk, n_valid, pad_info, n_rows = _route_plan(route, counts, tm=tm_moe, n_tok=T)
    xs = _dispatch(x1, dest, pad_info, tm=tm, n_rows=n_rows)
    ys = _moe(xs, tile_grp, tile_blk, n_valid, wts, tm=tm_moe)
    return _final(x1, dest, ys, p, wts, tm=tm)


def kernel(x_prompt, x_sample, p_prompt, p_sample, cache_kv, cache_k_rope, state_ssm_re, state_ssm_im, page_table, w_in, a_re, a_im, log_dt, b_re, b_im, c_re, c_im, d_skip, w_glu, b_glu, w_br_ssm, q_norm_g, w_uq, kv_norm_g, w_uk, w_uv, w_br_mla, w_out, ln1_g, ln1_b, w_gr, b_gr, w_er, b_er, w_e_gate, w_e_up, w_e_down, ln2_g, ln2_b, w_ple_gate, b_ple_gate, w_ple_proj):
    B, S, _ = x_prompt.shape
    Bs, Ss, _ = x_sample.shape
    Tp, Ts = B * S, Bs * Ss
    wts = _prep_weights(w_in[0], a_re[0], a_im[0], log_dt[0], b_re[0], b_im[0], c_re[0], c_im[0], d_skip[0],
                        w_glu[0], b_glu[0], w_br_ssm[0], q_norm_g[0], w_uq[0], kv_norm_g[0], w_uk[0], w_uv[0],
                        w_br_mla[0], w_out[0], ln1_g[0], ln1_b[0], w_gr[0], b_gr[0], w_er[0], b_er[0],
                        w_e_gate[0], w_e_up[0], w_e_down[0], ln2_g[0], ln2_b[0], w_ple_gate[0], b_ple_gate[0],
                        w_ple_proj[0])
    tm = 256
    tm_moe = 256
    tm_tail = 1024

    tabs_p = _rope_tables(jnp.arange(S, dtype=jnp.int32), 1)
    seq_tiles = S // tm
    u_p, ckv_p, kr_p, gs_p, gm_p, q3, k3, v3 = _proj(
        x_prompt.reshape(Tp, D_MODEL), wts, tabs_p, tm=tm, seq_tiles=seq_tiles, absorbed=False)
    zero_state = jnp.zeros((B, STATE_COLS), F32)
    ssm_p, sfin_p = _s5(u_p.reshape(B, S, SSM_WIDTH), zero_state, wts, lt=32, bb=B, name="s5_prompt")
    mla_p = _attn_prompt(q3, k3, v3, batch=B, seq=S, tq=512, tk=512)
    y_p = _token_tail(ssm_p.reshape(Tp, SSM_WIDTH), mla_p, gs_p, gm_p, x_prompt.reshape(Tp, D_MODEL),
                      p_prompt.reshape(Tp, PLE_DIM), wts, tm=tm_tail, tm_moe=tm_moe)

    tabs_s = _rope_tables(PAST_LEN + jnp.arange(Ss, dtype=jnp.int32), tm // Ss)
    u_s, ckv_s, kr_s, gs_s, gm_s, q_lat, q_rope = _proj(
        x_sample.reshape(Ts, D_MODEL), wts, tabs_s, tm=tm, seq_tiles=None, absorbed=True)
    s0 = _pack_state(state_ssm_re[0], state_ssm_im[0])
    ssm_s, sfin_s = _s5(u_s.reshape(Bs, Ss, SSM_WIDTH), s0, wts, lt=Ss, bb=64, name="s5_sample")
    ssm_s = ssm_s.reshape(Ts, SSM_WIDTH)
    n_pool = cache_kv.shape[1]
    o_lat = _attn_sample(page_table, q_lat.reshape(Ts * N_HEADS, KV_RANK), q_rope.reshape(Ts * N_HEADS, QK_ROPE),
                         ckv_s, kr_s, cache_kv.reshape(n_pool, PAGE_SIZE, KV_RANK),
                         cache_k_rope.reshape(n_pool, PAGE_SIZE, QK_ROPE).transpose(0, 2, 1),
                         batch=Bs, n_new=Ss, cp=64)
    mla_s = _mm(o_lat.reshape(Ts, N_HEADS * KV_RANK), wts["wuv_blockdiag"], tm=tm, out_dtype=BF16, name="uv_sample")
    y_s = _token_tail(ssm_s, mla_s, gs_s, gm_s, x_sample.reshape(Ts, D_MODEL), p_sample.reshape(Ts, PLE_DIM), wts,
                      tm=tm_tail, tm_moe=tm_moe)

    sre_p, sim_p = _unpack_state(sfin_p)
    sre_s, sim_s = _unpack_state(sfin_s)
    return (y_p.reshape(B, S, D_MODEL), y_s.reshape(Bs, Ss, D_MODEL),
            ckv_p.reshape(1, B, S, KV_RANK), kr_p.reshape(1, B, S, QK_ROPE),
            sre_p[None], sim_p[None],
            ckv_s.reshape(1, Bs, Ss, KV_RANK), kr_s.reshape(1, Bs, Ss, QK_ROPE),
            sre_s[None], sim_s[None])
```

```python
import functools
import math

import jax
import jax.numpy as jnp
from jax import lax
from jax.experimental import pallas as pl
from jax.experimental.pallas import tpu as pltpu

F32 = jnp.float32
BF16 = jnp.bfloat16

D_MODEL = 1024
DEPTH = 1
PAST_LEN = 16384
PAGE_SIZE = 128
SSM_WIDTH = 512
SSM_GROUP = 16
SSM_GROUPS = 32
SSM_STATE = 64
STATE_COLS = 2 * SSM_GROUPS * SSM_STATE
N_HEADS = 8
QK_NOPE = 64
QK_ROPE = 32
V_DIM = 64
Q_RANK = 256
KV_RANK = 256
HEAD_PAD = 128
ROPE_THETA = 10000.0
ATTN_SCALE = (QK_NOPE + QK_ROPE) ** -0.5
Q_SCALE = ATTN_SCALE * math.log2(math.e)
N_GROUPS = 4
EXPERTS_PER_GROUP = 8
D_EXPERT = 256
PLE_DIM = 256
LN_EPS = 1e-5
RMS_EPS = 1e-6
ALPHA = (2 * DEPTH) ** 0.25

LANES = 128
MXU_DIM = 256
VMEM_LIMIT = 56 * 1024 * 1024

ATTN_HEADS_PER_BODY = 4
CHAIN_ROWS = 256

NEG_INF = float("-inf")


def _params(sem, vmem=VMEM_LIMIT):
    return pltpu.CompilerParams(dimension_semantics=sem, vmem_limit_bytes=vmem)


def _const_spec(shape):
    zeros = (0,) * len(shape)
    return pl.BlockSpec(shape, lambda *_: zeros)


def _dot(a, b):
    return jnp.dot(a, b, preferred_element_type=F32)


def _dot_t(a, b):
    return lax.dot_general(a, b, (((1,), (1,)), ((), ())), preferred_element_type=F32)


def _layer_norm(x, g, b):
    mu = jnp.mean(x, axis=-1, keepdims=True)
    xc = x - mu
    var = jnp.mean(xc * xc, axis=-1, keepdims=True)
    return xc * lax.rsqrt(var + LN_EPS) * g + b


def _rms_norm(x, g):
    ms = jnp.mean(x * x, axis=-1, keepdims=True)
    return x * lax.rsqrt(ms + RMS_EPS) * g


def _proj_kernel(x_ref, w_ref, qg_ref, kg_ref, cos_ref, sin_ref, wq_ref, wqr_ref, *rest, absorbed):
    if absorbed:
        (wuk_ref, wqc_ref, wqcr_ref, cos8_ref, sin8_ref,
         u_ref, ckv_ref, kr_ref, gs_ref, gm_ref, ql_ref, qr_ref) = rest
    else:
        wk_ref, wv_ref, u_ref, ckv_ref, kr_ref, gs_ref, gm_ref, q_ref, k_ref, v_ref = rest
    xb = x_ref[...].astype(BF16)
    u_ref[...] = _dot(xb, w_ref[:, 0:512])
    cq = _rms_norm(_dot(xb, w_ref[:, 512:768]), qg_ref[...])
    ckv = _rms_norm(_dot(xb, w_ref[:, 768:1024]), kg_ref[...])
    ckv_ref[...] = ckv
    gs_ref[...] = jax.nn.sigmoid(_dot(xb, w_ref[:, 1024:2048])).astype(BF16)
    gm_ref[...] = jax.nn.sigmoid(_dot(xb, w_ref[:, 2048:3072])).astype(BF16)
    cos = cos_ref[...]
    sin = sin_ref[...]
    kr = _dot(xb, w_ref[:, 3072:3200]) * cos + _dot(xb, w_ref[:, 3200:3328]) * sin
    kr_ref[...] = kr[:, QK_NOPE:QK_NOPE + QK_ROPE]

    cqb = cq.astype(BF16)
    qa = _dot(cqb, wq_ref[...])
    qb = _dot(cqb, wqr_ref[...])
    lane = lax.broadcasted_iota(jnp.int32, cos.shape, 1)
    cq_tab = (cos + jnp.where(lane < QK_NOPE, 1.0, 0.0)) * Q_SCALE
    sq_tab = sin * Q_SCALE
    ckvb = ckv.astype(BF16)
    if absorbed:
        for h in range(N_HEADS):
            sl = slice(h * HEAD_PAD, (h + 1) * HEAD_PAD)
            qh = (qa[:, sl] * cq_tab + qb[:, sl] * sq_tab).astype(BF16)
            ql_ref[:, h * KV_RANK:(h + 1) * KV_RANK] = _dot(qh, wuk_ref[h]).astype(BF16)
        qc = _dot(cqb, wqc_ref[...]) * cos8_ref[...] + _dot(cqb, wqcr_ref[...]) * sin8_ref[...]
        qr_ref[...] = (qc * Q_SCALE).astype(BF16)
    else:
        kn = _dot(ckvb, wk_ref[...])
        vv = _dot(ckvb, wv_ref[...])
        for h in range(N_HEADS):
            sl = slice(h * HEAD_PAD, (h + 1) * HEAD_PAD)
            q_ref[h] = (qa[:, sl] * cq_tab + qb[:, sl] * sq_tab).astype(BF16)
            k_ref[h] = (kn[:, sl] + kr).astype(BF16)
            v_ref[h] = jnp.where(lane == V_DIM, 1.0, vv[:, sl]).astype(BF16)


def _proj(x, wts, tabs, *, tm, seq_tiles, absorbed):
    T = x.shape[0]
    nt = T // tm
    row = lambda i: (i, 0)
    if seq_tiles is None:
        tab_map = lambda i: (0, 0)
    else:
        tab_map = lambda i: (i % seq_tiles, 0)
    u_shape, u_spec = (T, SSM_WIDTH), pl.BlockSpec((tm, SSM_WIDTH), row)
    in_specs = [
        pl.BlockSpec((tm, D_MODEL), row),
        _const_spec(wts["w_in"].shape),
        _const_spec((1, Q_RANK)), _const_spec((1, KV_RANK)),
        pl.BlockSpec((tm, HEAD_PAD), tab_map), pl.BlockSpec((tm, HEAD_PAD), tab_map),
        _const_spec(wts["wq"].shape), _const_spec(wts["wq_rot"].shape),
    ]
    args = [x, wts["w_in"], wts["q_norm_g"], wts["kv_norm_g"], tabs["cos128"], tabs["sin128"],
            wts["wq"], wts["wq_rot"]]
    out_shape = [jax.ShapeDtypeStruct(u_shape, F32),
                 jax.ShapeDtypeStruct((T, KV_RANK), F32),
                 jax.ShapeDtypeStruct((T, QK_ROPE), F32),
                 jax.ShapeDtypeStruct((T, D_MODEL), BF16),
                 jax.ShapeDtypeStruct((T, D_MODEL), BF16)]
    out_specs = [u_spec, pl.BlockSpec((tm, KV_RANK), row), pl.BlockSpec((tm, QK_ROPE), row),
                 pl.BlockSpec((tm, D_MODEL), row), pl.BlockSpec((tm, D_MODEL), row)]
    if absorbed:
        in_specs += [_const_spec(wts["wuk_abs"].shape), _const_spec(wts["wq_rope"].shape),
                     _const_spec(wts["wq_rope_rot"].shape),
                     pl.BlockSpec((tm, N_HEADS * QK_ROPE), tab_map),
                     pl.BlockSpec((tm, N_HEADS * QK_ROPE), tab_map)]
        args += [wts["wuk_abs"], wts["wq_rope"], wts["wq_rope_rot"], tabs["cos_heads"], tabs["sin_heads"]]
        out_shape += [jax.ShapeDtypeStruct((T, N_HEADS * KV_RANK), BF16),
                      jax.ShapeDtypeStruct((T, N_HEADS * QK_ROPE), BF16)]
        out_specs += [pl.BlockSpec((tm, N_HEADS * KV_RANK), row), pl.BlockSpec((tm, N_HEADS * QK_ROPE), row)]
    else:
        in_specs += [_const_spec(wts["wk"].shape), _const_spec(wts["wv"].shape)]
        args += [wts["wk"], wts["wv"]]
        head = jax.ShapeDtypeStruct((N_HEADS, T, HEAD_PAD), BF16)
        head_spec = pl.BlockSpec((N_HEADS, tm, HEAD_PAD), lambda i: (0, i, 0))
        out_shape += [head, head, head]
        out_specs += [head_spec, head_spec, head_spec]
    return pl.pallas_call(
        functools.partial(_proj_kernel, absorbed=absorbed),
        grid=(nt,), in_specs=in_specs, out_specs=out_specs, out_shape=out_shape,
        compiler_params=_params(("parallel",)),
        name="proj_sample" if absorbed else "proj_prompt",
    )(*args)


def _s5_kernel(u_ref, s0_ref, lam_ref, wb_ref, wc_ref, dskip_ref, wglu_ref, bglu_ref,
               y_ref, sfin_ref, bu_ref, st_ref, tr_ref, *, lt, bb):
    ti = pl.program_id(1)
    rows = lt * bb
    n_tiles = STATE_COLS // MXU_DIM
    n_slabs = SSM_WIDTH // LANES

    @pl.when(ti == 0)
    def _():
        st_ref[...] = s0_ref[...]

    for b in range(bb):
        for g in range(n_slabs):
            tr_ref[g, pl.ds(b, lt, stride=bb), :] = u_ref[b, :, g * LANES:(g + 1) * LANES]
    u = jnp.concatenate([tr_ref[g] for g in range(n_slabs)], axis=1)
    ub = u.astype(BF16)
    for j in range(n_tiles):
        k0 = MXU_DIM * (j // (n_tiles // 2))
        bu_ref[:, j * MXU_DIM:(j + 1) * MXU_DIM] = _dot(ub[:, k0:k0 + MXU_DIM], wb_ref[j])

    jg = max(1, 64 // bb)
    for j0 in range(0, n_tiles, jg):
        lrs = [jnp.broadcast_to(lam_ref[0:1, (j0 + q) * LANES:(j0 + q + 1) * LANES], (bb, LANES)) for q in range(jg)]
        lis = [jnp.broadcast_to(lam_ref[1:2, (j0 + q) * LANES:(j0 + q + 1) * LANES], (bb, LANES)) for q in range(jg)]

        def body(t, carry, j0=j0, lrs=lrs, lis=lis):
            r0 = pl.multiple_of(t * bb, bb)
            new = []
            for q in range(jg):
                c0 = (j0 + q) * MXU_DIM
                sr, si = carry[2 * q], carry[2 * q + 1]
                nr = lrs[q] * sr - lis[q] * si + bu_ref[pl.ds(r0, bb), c0:c0 + LANES]
                ni = lrs[q] * si + lis[q] * sr + bu_ref[pl.ds(r0, bb), c0 + LANES:c0 + 2 * LANES]
                bu_ref[pl.ds(r0, bb), c0:c0 + LANES] = nr
                bu_ref[pl.ds(r0, bb), c0 + LANES:c0 + 2 * LANES] = ni
                new += [nr, ni]
            return tuple(new)

        init = []
        for q in range(jg):
            c0 = (j0 + q) * MXU_DIM
            init += [st_ref[:, c0:c0 + LANES], st_ref[:, c0 + LANES:c0 + 2 * LANES]]
        fin = lax.fori_loop(0, lt, body, tuple(init), unroll=min(lt, 8))
        for q in range(jg):
            c0 = (j0 + q) * MXU_DIM
            st_ref[:, c0:c0 + LANES] = fin[2 * q]
            st_ref[:, c0 + LANES:c0 + 2 * LANES] = fin[2 * q + 1]

    half = STATE_COLS // 2
    y = jnp.concatenate([_dot(bu_ref[:, 0:half].astype(BF16), wc_ref[0]),
                         _dot(bu_ref[:, half:STATE_COLS].astype(BF16), wc_ref[1])], axis=1)
    y = y + dskip_ref[...] * u
    z = jax.nn.gelu(y, approximate=True)
    gate = jax.nn.sigmoid(_dot(z.astype(BF16), wglu_ref[...]) + bglu_ref[...])
    out = z * gate
    for g in range(n_slabs):
        tr_ref[g] = out[:, g * LANES:(g + 1) * LANES]
    for b in range(bb):
        y_ref[b] = jnp.concatenate([tr_ref[g, pl.ds(b, lt, stride=bb), :] for g in range(n_slabs)],
                                   axis=1).astype(BF16)

    @pl.when(ti == pl.num_programs(1) - 1)
    def _():
        sfin_ref[...] = st_ref[...]


def _s5(u3, s0, wts, *, lt, bb, name):
    B, S, _ = u3.shape
    return pl.pallas_call(
        functools.partial(_s5_kernel, lt=lt, bb=bb),
        grid=(B // bb, S // lt),
        in_specs=[pl.BlockSpec((bb, lt, SSM_WIDTH), lambda b, t: (b, t, 0)),
                  pl.BlockSpec((bb, STATE_COLS), lambda b, t: (b, 0)),
                  _const_spec(wts["lam"].shape), _const_spec(wts["s5_wb"].shape),
                  _const_spec(wts["s5_wc"].shape), _const_spec((1, SSM_WIDTH)),
                  _const_spec(wts["w_glu"].shape), _const_spec((1, SSM_WIDTH))],
        out_specs=[pl.BlockSpec((bb, lt, SSM_WIDTH), lambda b, t: (b, t, 0)),
                   pl.BlockSpec((bb, STATE_COLS), lambda b, t: (b, 0))],
        out_shape=[jax.ShapeDtypeStruct((B, S, SSM_WIDTH), BF16),
                   jax.ShapeDtypeStruct((B, STATE_COLS), F32)],
        scratch_shapes=[pltpu.VMEM((lt * bb, STATE_COLS), F32), pltpu.VMEM((bb, STATE_COLS), F32),
                        pltpu.VMEM((SSM_WIDTH // LANES, lt * bb, LANES), F32)],
        compiler_params=_params(("parallel", "arbitrary")),
        name=name,
    )(u3, s0, wts["lam"], wts["s5_wb"], wts["s5_wc"], wts["d_skip"], wts["w_glu"], wts["b_glu"])


def _attn_prompt_kernel(q_ref, k_ref, v_ref, o_ref, *, tq, tk):
    qi = pl.program_id(1)
    n_diag = tq // tk
    row = lax.broadcasted_iota(jnp.int32, (tq, tk), 0)
    col = lax.broadcasted_iota(jnp.int32, (tq, tk), 1)
    n_full = qi * n_diag
    lane = lax.broadcasted_iota(jnp.int32, (tq, HEAD_PAD), 1)
    pairs = []
    for h0 in range(0, N_HEADS, ATTN_HEADS_PER_BODY):
        heads = tuple(range(h0, h0 + ATTN_HEADS_PER_BODY))
        qs = [q_ref[h] for h in heads]

        def step(kt, carry, diag=None, heads=heads, qs=qs):
            k0 = pl.multiple_of(kt * tk, tk)
            new = []
            for j, h in enumerate(heads):
                m, acc = carry[2 * j], carry[2 * j + 1]
                s = _dot_t(qs[j], k_ref[h, pl.ds(k0, tk), :])
                if diag is not None:
                    s = jnp.where(col + diag * tk <= row, s, NEG_INF)
                m_new = jnp.maximum(m, jnp.max(s, axis=1, keepdims=True))
                p = jnp.exp2(s - m_new).astype(BF16)
                acc = jnp.exp2(m - m_new) * acc + _dot(p, v_ref[h, pl.ds(k0, tk), :])
                new += [m_new, acc]
            return tuple(new)

        carry = (jnp.full((tq, 1), NEG_INF, F32), jnp.zeros((tq, HEAD_PAD), F32)) * ATTN_HEADS_PER_BODY
        carry = lax.fori_loop(0, n_full, step, carry)
        for d in range(n_diag):
            carry = step(n_full + d, carry, diag=d)
        outs = [jnp.where(lane < V_DIM, acc / acc[:, V_DIM:V_DIM + 1], 0.0) for acc in carry[1::2]]
        pairs += [outs[j] + pltpu.roll(outs[j + 1], V_DIM, axis=1) for j in range(0, ATTN_HEADS_PER_BODY, 2)]
    o_ref[...] = jnp.concatenate(pairs, axis=1).astype(BF16)


def _attn_prompt(q3, k3, v3, *, batch, seq, tq, tk):
    nq = seq // tq
    T = batch * seq
    return pl.pallas_call(
        functools.partial(_attn_prompt_kernel, tq=tq, tk=tk),
        grid=(batch, nq),
        in_specs=[pl.BlockSpec((N_HEADS, tq, HEAD_PAD), lambda b, i: (0, b * nq + i, 0)),
                  pl.BlockSpec((N_HEADS, seq, HEAD_PAD), lambda b, i: (0, b, 0)),
                  pl.BlockSpec((N_HEADS, seq, HEAD_PAD), lambda b, i: (0, b, 0))],
        out_specs=pl.BlockSpec((tq, N_HEADS * V_DIM), lambda b, i: (b * nq + i, 0)),
        out_shape=jax.ShapeDtypeStruct((T, N_HEADS * V_DIM), BF16),
        compiler_params=_params(("parallel", "arbitrary")),
        name="attn_prompt",
    )(q3, k3, v3)


def _attn_sample_kernel(pt_ref, ql_ref, qr_ref, qlp_ref, qrp_ref, cn_ref, krn_ref, kv_hbm, kr_hbm, o_ref,
                        kvbuf, krbuf, kv_sem, kr_sem, kvb_ref, s_ref, m_ref, l_ref, acc_ref,
                        *, cp, n_chunks, n_steps, n_new):
    step = pl.program_id(0)
    slot = lax.rem(step, 2)

    def page_copies(chunk, sl):
        b = lax.div(chunk, n_chunks)
        c = lax.rem(chunk, n_chunks)
        out = []
        for k in range(cp):
            page = pt_ref[b, c * cp + k]
            out.append(pltpu.make_async_copy(kv_hbm.at[page], kvbuf.at[sl, k], kv_sem.at[sl]))
            out.append(pltpu.make_async_copy(kr_hbm.at[page], krbuf.at[sl, k], kr_sem.at[sl]))
        return out

    @pl.when(step == 0)
    def _():
        for d in page_copies(0, 0):
            d.start()

    @pl.when(step + 1 < n_steps)
    def _():
        for d in page_copies(step + 1, 1 - slot):
            d.start()

    @pl.when(step < n_steps)
    def _():
        for d in page_copies(step, slot):
            d.wait()

    def reset_state():
        m_ref[...] = jnp.full(m_ref.shape, NEG_INF, F32)
        l_ref[...] = jnp.zeros(l_ref.shape, F32)
        acc_ref[...] = jnp.zeros(acc_ref.shape, F32)

    @pl.when(step == 0)
    def _():
        reset_state()
        kvb_ref[1] = jnp.zeros(kvb_ref.shape[1:], BF16)
        s_ref[1] = jnp.zeros(s_ref.shape[1:], F32)

    ql = ql_ref[...]
    qr = qr_ref[...]

    def update(state, s, vals):
        m, l, acc = state
        m_new = jnp.maximum(m, jnp.max(s, axis=1, keepdims=True))
        alpha = jnp.exp2(m - m_new)
        p = jnp.exp2(s - m_new)
        return (m_new, alpha * l + jnp.sum(p, axis=1, keepdims=True),
                alpha * acc + _dot(p.astype(BF16), vals))

    def fold(sl):
        state = update((m_ref[...], l_ref[...], acc_ref[...]), s_ref[sl], kvb_ref[sl])
        m_ref[...], l_ref[...], acc_ref[...] = state
        return state

    def body(sl):
        kv = kvbuf[sl].reshape(cp * PAGE_SIZE, KV_RANK).astype(BF16)
        kr_t = jnp.concatenate([krbuf[sl, k] for k in range(cp)], axis=1).astype(BF16)
        kvb_ref[sl] = kv
        s_ref[sl] = _dot_t(ql, kv) + _dot(qr, kr_t)
        fold(1 - sl)

    for sl in range(2):
        pl.when(slot == sl)(functools.partial(body, sl))

    @pl.when(lax.rem(step, n_chunks) == 0)
    def _():
        @pl.when(step > 0)
        def _():
            qlp = qlp_ref[...]
            qrp = qrp_ref[...]
            cn = cn_ref[...].astype(BF16)
            krn = krn_ref[...].astype(BF16)
            s = _dot_t(qlp, cn) + _dot_t(qrp, krn)
            rows = qlp.shape[0]
            t_row = jnp.right_shift(lax.broadcasted_iota(jnp.int32, (rows, n_new), 0), 3)
            t_col = lax.broadcasted_iota(jnp.int32, (rows, n_new), 1)
            state = (m_ref[...], l_ref[...], acc_ref[...])
            _, l, acc = update(state, jnp.where(t_col <= t_row, s, NEG_INF), cn)
            o_ref[...] = (acc / l).astype(BF16)

        reset_state()


def _attn_sample(page_table, q_lat, q_rope, ckv_new, kr_new, cache_kv, cache_kr_t, *, batch, n_new, cp):
    rows = n_new * N_HEADS
    n_chunks = page_table.shape[1] // cp
    n_steps = batch * n_chunks
    cur = lambda t, pt: (jnp.minimum(t, n_steps - 1) // n_chunks, 0)
    prev = lambda t, pt: (jnp.maximum(t - 1, 0) // n_chunks, 0)
    in_specs = [pl.BlockSpec((rows, KV_RANK), cur), pl.BlockSpec((rows, QK_ROPE), cur),
                pl.BlockSpec((rows, KV_RANK), prev), pl.BlockSpec((rows, QK_ROPE), prev),
                pl.BlockSpec((n_new, KV_RANK), prev), pl.BlockSpec((n_new, QK_ROPE), prev),
                pl.BlockSpec(memory_space=pl.ANY), pl.BlockSpec(memory_space=pl.ANY)]
    grid_spec = pltpu.PrefetchScalarGridSpec(
        num_scalar_prefetch=1, grid=(n_steps + 1,), in_specs=in_specs,
        out_specs=pl.BlockSpec((rows, KV_RANK), prev),
        scratch_shapes=[pltpu.VMEM((2, cp, PAGE_SIZE, KV_RANK), F32), pltpu.VMEM((2, cp, QK_ROPE, PAGE_SIZE), F32),
                        pltpu.SemaphoreType.DMA((2,)), pltpu.SemaphoreType.DMA((2,)),
                        pltpu.VMEM((2, cp * PAGE_SIZE, KV_RANK), BF16), pltpu.VMEM((2, rows, cp * PAGE_SIZE), F32),
                        pltpu.VMEM((rows, 1), F32), pltpu.VMEM((rows, 1), F32), pltpu.VMEM((rows, KV_RANK), F32)])
    return pl.pallas_call(
        functools.partial(_attn_sample_kernel, cp=cp, n_chunks=n_chunks, n_steps=n_steps, n_new=n_new),
        grid_spec=grid_spec,
        out_shape=jax.ShapeDtypeStruct((batch * rows, KV_RANK), BF16),
        compiler_params=_params(("arbitrary",)),
        name="attn_sample",
    )(page_table, q_lat, q_rope, q_lat, q_rope, ckv_new, kr_new, cache_kv, cache_kr_t)


def _mm_kernel(x_ref, w_ref, o_ref):
    o_ref[...] = _dot(x_ref[...], w_ref[...]).astype(o_ref.dtype)


def _mm(x, w, *, tm, out_dtype, name):
    M, K = x.shape
    N = w.shape[1]
    return pl.pallas_call(
        _mm_kernel, grid=(M // tm,),
        in_specs=[pl.BlockSpec((tm, K), lambda i: (i, 0)), _const_spec((K, N))],
        out_specs=pl.BlockSpec((tm, N), lambda i: (i, 0)),
        out_shape=jax.ShapeDtypeStruct((M, N), out_dtype),
        compiler_params=_params(("parallel",)), name=name,
    )(x, w)


def _merge_kernel(ssm_ref, mla_ref, gs_ref, gm_ref, x_ref, wbs_ref, wbm_ref, wo_ref, g_ref, b_ref,
                  wr_ref, br_ref, x1_ref, route_ref, cnt_ref, run_ref, *, tm):
    i = pl.program_id(0)

    @pl.when(i == 0)
    def _():
        run_ref[...] = jnp.zeros(run_ref.shape, F32)

    hm = CHAIN_ROWS
    lane = lax.broadcasted_iota(jnp.int32, (hm, LANES), 1).astype(F32)
    r_i = lax.broadcasted_iota(jnp.int32, (hm, hm), 0)
    c_i = lax.broadcasted_iota(jnp.int32, (hm, hm), 1)
    before = jnp.where(c_i < r_i, 1.0, 0.0).astype(BF16)
    run = run_ref[0:1, :]
    for c in range(tm // hm):
        rows = slice(c * hm, (c + 1) * hm)
        a = _dot(ssm_ref[rows, :], wbs_ref[...])
        m = _dot(mla_ref[rows, :], wbm_ref[...])
        merged = gs_ref[rows, :].astype(F32) * a + gm_ref[rows, :].astype(F32) * m
        y = _dot(merged.astype(BF16), wo_ref[...])
        x1 = _layer_norm(ALPHA * x_ref[rows, :] + y, g_ref[...], b_ref[...])
        x1_ref[rows, :] = x1

        logits = _dot(x1.astype(BF16), wr_ref[...]) + br_ref[...]
        gl = jnp.where(lane < N_GROUPS, logits, NEG_INF)
        gmax = jnp.max(gl, axis=1, keepdims=True)
        g_sel = jnp.min(jnp.where(gl == gmax, lane, float(LANES)), axis=1, keepdims=True)
        onehot = jnp.where(lane == g_sel, 1.0, 0.0)
        rank_all = _dot(before, onehot.astype(BF16)) + run
        rank = jnp.sum(onehot * rank_all, axis=1, keepdims=True)
        route_ref[rows, :] = jnp.where(lane == 0.0, g_sel, jnp.where(lane == 1.0, rank, 0.0))
        run = run + jnp.sum(onehot, axis=0, keepdims=True)
    run_ref[0:1, :] = run

    @pl.when(i == pl.num_programs(0) - 1)
    def _():
        cnt_ref[...] = run_ref[...]


def _merge(ssm, mla, gs, gm, x, wts, *, tm):
    T = x.shape[0]
    row = lambda i: (i, 0)
    ssm_spec = pl.BlockSpec((tm, SSM_WIDTH), row)
    return pl.pallas_call(
        functools.partial(_merge_kernel, tm=tm),
        grid=(T // tm,),
        in_specs=[ssm_spec, pl.BlockSpec((tm, N_HEADS * V_DIM), row),
                  pl.BlockSpec((tm, D_MODEL), row), pl.BlockSpec((tm, D_MODEL), row),
                  pl.BlockSpec((tm, D_MODEL), row),
                  _const_spec(wts["w_br_ssm"].shape), _const_spec(wts["w_br_mla"].shape),
                  _const_spec(wts["w_out"].shape), _const_spec((1, D_MODEL)), _const_spec((1, D_MODEL)),
                  _const_spec((D_MODEL, LANES)), _const_spec((1, LANES))],
        out_specs=[pl.BlockSpec((tm, D_MODEL), row), pl.BlockSpec((tm, LANES), row),
                   _const_spec((8, LANES))],
        out_shape=[jax.ShapeDtypeStruct((T, D_MODEL), F32), jax.ShapeDtypeStruct((T, LANES), F32),
                   jax.ShapeDtypeStruct((8, LANES), F32)],
        scratch_shapes=[pltpu.VMEM((8, LANES), F32)],
        compiler_params=_params(("arbitrary",)),
        name="merge_ln1_route",
    )(ssm, mla, gs, gm, x, wts["w_br_ssm"], wts["w_br_mla"], wts["w_out"], wts["ln1_g"], wts["ln1_b"],
      wts["wrg"], wts["brg"])


def _dispatch_kernel(pad_ref, x_ref, dest_ref, xs_ref, zero_ref, sem, *, tm):
    i = pl.program_id(0)

    def row_copy(r, d):
        return pltpu.make_async_copy(x_ref.at[pl.ds(r, 1)], xs_ref.at[pl.ds(d, 1)], sem)

    def zero_copy(d):
        return pltpu.make_async_copy(zero_ref.at[pl.ds(0, 1)], xs_ref.at[pl.ds(d, 1)], sem)

    @pl.when(i == 0)
    def _():
        zero_ref[...] = jnp.zeros(zero_ref.shape, F32)
        for g in range(N_GROUPS + 1):
            start = pad_ref[g]
            n = pad_ref[N_GROUPS + 1 + g]

            def issue(r, c, start=start):
                zero_copy(start + r).start()
                return c

            def drain(r, c, start=start):
                zero_copy(start + r).wait()
                return c

            lax.fori_loop(0, n, issue, 0)
            lax.fori_loop(0, n, drain, 0)

    for r in range(tm):
        row_copy(r, dest_ref[0, 0, r]).start(priority=r % 2)
    pltpu.make_async_copy(x_ref, xs_ref.at[pl.ds(0, tm)], sem).wait()


def _dispatch(x1, dest, pad_info, *, tm, n_rows):
    T = x1.shape[0]
    nt = T // tm
    grid_spec = pltpu.PrefetchScalarGridSpec(
        num_scalar_prefetch=1, grid=(nt,),
        in_specs=[pl.BlockSpec((tm, D_MODEL), lambda i, pad: (i, 0)),
                  pl.BlockSpec((1, 1, tm), lambda i, pad: (i, 0, 0), memory_space=pltpu.SMEM)],
        out_specs=pl.BlockSpec(memory_space=pl.ANY),
        scratch_shapes=[pltpu.VMEM((8, D_MODEL), F32), pltpu.SemaphoreType.DMA(())])
    return pl.pallas_call(
        functools.partial(_dispatch_kernel, tm=tm),
        grid_spec=grid_spec,
        out_shape=jax.ShapeDtypeStruct((n_rows, D_MODEL), F32),
        compiler_params=_params(("arbitrary",)),
        name="moe_dispatch",
    )(pad_info, x1, dest.reshape(nt, 1, tm))


def _moe_kernel(grp_ref, blk_ref, nv_ref, x_ref, wr_ref, br_ref, wg_ref, wu_ref, wd_ref, o_ref, *, tm):
    del blk_ref
    i = pl.program_id(0)

    @pl.when(i < nv_ref[0])
    def _():
        g = grp_ref[i]
        x = x_ref[...]
        xb = x.astype(BF16)
        logits = _dot(xb, wr_ref[0]) + br_ref[0]
        lane = lax.broadcasted_iota(jnp.int32, (tm, LANES), 1).astype(F32)
        is_grp = (lane >= EXPERTS_PER_GROUP) & (lane < EXPERTS_PER_GROUP + N_GROUPS)
        gl = jnp.where(is_grp, logits, NEG_INF)
        gmax = jnp.max(gl, axis=1, keepdims=True)
        gexp = jnp.exp(gl - gmax)
        p_group = (jnp.sum(jnp.where(lane == (EXPERTS_PER_GROUP + g).astype(F32), gexp, 0.0), axis=1, keepdims=True)
                   / jnp.sum(gexp, axis=1, keepdims=True))
        el = jnp.where(lane < EXPERTS_PER_GROUP, logits, NEG_INF)
        v1 = jnp.max(el, axis=1, keepdims=True)
        i1 = jnp.min(jnp.where(el == v1, lane, float(LANES)), axis=1, keepdims=True)
        el2 = jnp.where(lane == i1, NEG_INF, el)
        v2 = jnp.max(el2, axis=1, keepdims=True)
        i2 = jnp.min(jnp.where(el2 == v2, lane, float(LANES)), axis=1, keepdims=True)
        e2 = jnp.exp(v2 - v1)
        w1 = p_group / (1.0 + e2)
        w2 = w1 * e2
        gates = jnp.where(lane == i1, w1, jnp.where(lane == i2, w2, 0.0))

        hs = []
        for e in range(EXPERTS_PER_GROUP):
            hg = _dot(xb, wg_ref[0, e])
            hu = _dot(xb, wu_ref[0, e])
            ge = jnp.sum(jnp.where(lane == float(e), gates, 0.0), axis=1, keepdims=True)
            hs.append((jax.nn.silu(hg) * hu * ge).astype(BF16))
        o_ref[...] = _dot(jnp.concatenate(hs, axis=1), wd_ref[0])

    @pl.when(i >= nv_ref[0])
    def _():
        o_ref[...] = jnp.zeros(o_ref.shape, F32)


def _moe(xs, tile_grp, tile_blk, n_valid, wts, *, tm):
    n_rows = xs.shape[0]
    nt = n_rows // tm
    grp3 = lambda i, grp, blk, nv: (grp[i], 0, 0)
    grp4 = lambda i, grp, blk, nv: (grp[i], 0, 0, 0)
    rows = lambda i, grp, blk, nv: (blk[i], 0)
    grid_spec = pltpu.PrefetchScalarGridSpec(
        num_scalar_prefetch=3, grid=(nt,),
        in_specs=[pl.BlockSpec((tm, D_MODEL), rows),
                  pl.BlockSpec((1, D_MODEL, LANES), grp3), pl.BlockSpec((1, 1, LANES), grp3),
                  pl.BlockSpec((1, EXPERTS_PER_GROUP, D_MODEL, D_EXPERT), grp4),
                  pl.BlockSpec((1, EXPERTS_PER_GROUP, D_MODEL, D_EXPERT), grp4),
                  pl.BlockSpec((1, EXPERTS_PER_GROUP * D_EXPERT, D_MODEL), grp3)],
        out_specs=pl.BlockSpec((tm, D_MODEL), lambda i, grp, blk, nv: (i, 0)))
    return pl.pallas_call(
        functools.partial(_moe_kernel, tm=tm),
        grid_spec=grid_spec,
        out_shape=jax.ShapeDtypeStruct((n_rows, D_MODEL), F32),
        compiler_params=_params(("arbitrary",)),
        name="moe_experts",
    )(tile_grp, tile_blk, n_valid, xs, wts["wroute"], wts["broute"],
      wts["w_e_gate"], wts["w_e_up"], wts["w_e_down"])


def _final_kernel(x1_ref, dest_ref, dest_next_ref, ys_ref, p_ref, g_ref, b_ref, wpg_ref, bpg_ref, wpp_ref, o_ref,
                  moe_ref, sem, *, tm):
    i = pl.program_id(0)
    slot = lax.rem(i, 2)

    def gather(d_ref, sl):
        for r in range(tm):
            pltpu.make_async_copy(ys_ref.at[pl.ds(d_ref[0, 0, r], 1)], moe_ref.at[sl, pl.ds(r, 1)],
                                  sem.at[sl]).start(priority=r % 2)

    @pl.when(i == 0)
    def _():
        gather(dest_ref, 0)

    @pl.when(i + 1 < pl.num_programs(0))
    def _():
        gather(dest_next_ref, 1 - slot)

    pltpu.make_async_copy(ys_ref.at[pl.ds(0, tm)], moe_ref.at[slot], sem.at[slot]).wait()
    hm = CHAIN_ROWS
    for c in range(tm // hm):
        rows = slice(c * hm, (c + 1) * hm)
        x2 =_layer_norm(ALPHA * x1_ref[rows, :] + moe_ref[slot, rows, :], g_ref[...], b_ref[...])
        gate = jax.nn.sigmoid(_dot(x2.astype(BF16), wpg_ref[...]) + bpg_ref[...])
        o_ref[rows, :] = x2 + gate * _dot(p_ref[rows, :].astype(BF16), wpp_ref[...])


def _final(x1, dest, ys, p, wts, *, tm):
    T = x1.shape[0]
    nt = T // tm
    row = lambda i: (i, 0)
    return pl.pallas_call(
        functools.partial(_final_kernel, tm=tm),
        grid=(nt,),
        in_specs=[pl.BlockSpec((tm, D_MODEL), row),
                  pl.BlockSpec((1, 1, tm), lambda i: (i, 0, 0), memory_space=pltpu.SMEM),
                  pl.BlockSpec((1, 1, tm), lambda i: (jnp.minimum(i + 1, nt - 1), 0, 0), memory_space=pltpu.SMEM),
                  pl.BlockSpec(memory_space=pl.ANY),
                  pl.BlockSpec((tm, PLE_DIM), row),
                  _const_spec((1, D_MODEL)), _const_spec((1, D_MODEL)),
                  _const_spec((D_MODEL, D_MODEL)), _const_spec((1, D_MODEL)), _const_spec((PLE_DIM, D_MODEL))],
        out_specs=pl.BlockSpec((tm, D_MODEL), row),
        out_shape=jax.ShapeDtypeStruct((T, D_MODEL), F32),
        scratch_shapes=[pltpu.VMEM((2, tm, D_MODEL), F32), pltpu.SemaphoreType.DMA((2,))],
        compiler_params=_params(("arbitrary",)),
        name="combine_ln2_ple",
    )(x1, dest.reshape(nt, 1, tm), dest.reshape(nt, 1, tm), ys, p, wts["ln2_g"], wts["ln2_b"], wts["w_ple_gate"], wts["b_ple_gate"],
      wts["w_ple_proj"])


def _prep_weights(w_in, a_re, a_im, log_dt, b_re, b_im, c_re, c_im, d_skip, w_glu, b_glu, w_br_ssm,
                  q_norm_g, w_uq, kv_norm_g, w_uk, w_uv, w_br_mla, w_out, ln1_g, ln1_b,
                  w_gr, b_gr, w_er, b_er, w_e_gate, w_e_up, w_e_down, ln2_g, ln2_b,
                  w_ple_gate, b_ple_gate, w_ple_proj):
    half = QK_ROPE // 2
    rot = lambda w: jnp.concatenate([-w[..., half:], w[..., :half]], axis=-1)
    w = {}

    u_w, cq_w, ckv_w, kr_w, gs_w, gm_w = jnp.split(
        w_in, (512, 768, 1024, 1024 + QK_ROPE, 1024 + QK_ROPE + D_MODEL), axis=1)
    slab = lambda c: jnp.pad(c, ((0, 0), (QK_NOPE, HEAD_PAD - QK_NOPE - QK_ROPE)))
    w["w_in"] = jnp.concatenate([u_w, cq_w, ckv_w, gs_w, gm_w, slab(kr_w), slab(rot(kr_w))], axis=1).astype(BF16)
    w["q_norm_g"] = q_norm_g.reshape(1, Q_RANK)
    w["kv_norm_g"] = kv_norm_g.reshape(1, KV_RANK)

    wq3 = w_uq.reshape(Q_RANK, N_HEADS, QK_NOPE + QK_ROPE)
    nope, rope = wq3[..., :QK_NOPE], wq3[..., QK_NOPE:]
    pad_tail = jnp.zeros((Q_RANK, N_HEADS, HEAD_PAD - QK_NOPE - QK_ROPE), F32)
    w["wq"] = jnp.concatenate([nope, rope, pad_tail], -1).reshape(Q_RANK, N_HEADS * HEAD_PAD).astype(BF16)
    w["wq_rot"] = jnp.concatenate([jnp.zeros_like(nope), rot(rope), pad_tail], -1).reshape(
        Q_RANK, N_HEADS * HEAD_PAD).astype(BF16)
    w["wq_rope"] = rope.reshape(Q_RANK, N_HEADS * QK_ROPE).astype(BF16)
    w["wq_rope_rot"] = rot(rope).reshape(Q_RANK, N_HEADS * QK_ROPE).astype(BF16)
    head_pad = lambda a: jnp.pad(a, ((0, 0), (0, 0), (0, HEAD_PAD - a.shape[-1]))).reshape(
        a.shape[0], N_HEADS * HEAD_PAD).astype(BF16)
    w["wk"] = head_pad(w_uk)
    w["wv"] = head_pad(w_uv)
    w["wuk_abs"] = jnp.pad(w_uk.transpose(1, 2, 0), ((0, 0), (0, HEAD_PAD - QK_NOPE), (0, 0))).astype(BF16)
    w["wuv_blockdiag"] = (jnp.eye(N_HEADS, dtype=F32)[:, None, :, None]
                          * w_uv.transpose(1, 0, 2)[:, :, None, :]).reshape(
                              N_HEADS * KV_RANK, N_HEADS * V_DIM).astype(BF16)

    dt = jnp.exp(log_dt)[:, None]
    mag = jnp.exp(dt * a_re)
    lr = mag * jnp.cos(dt * a_im)
    li = mag * jnp.sin(dt * a_im)
    den = a_re * a_re + a_im * a_im
    fr = ((lr - 1.0) * a_re + li * a_im) / den
    fi = (li * a_re - (lr - 1.0) * a_im) / den
    bbr = fr[..., None] * b_re - fi[..., None] * b_im
    bbi = fr[..., None] * b_im + fi[..., None] * b_re
    w["lam"] = jnp.stack([lr.reshape(-1), li.reshape(-1)])
    n_tiles = SSM_GROUPS // 2
    col_group = jnp.broadcast_to((2 * jnp.arange(n_tiles)[:, None, None, None]
                                  + jnp.arange(2)[None, None, :, None]), (n_tiles, 2, 2, SSM_STATE)).reshape(-1)
    ch_group = jnp.arange(SSM_WIDTH) // SSM_GROUP
    mask = ch_group[:, None] == col_group[None, :]
    bcols = jnp.stack([bbr, bbi]).reshape(2, n_tiles, 2, SSM_STATE, SSM_GROUP).transpose(
        1, 0, 2, 3, 4).reshape(STATE_COLS, SSM_GROUP)
    bmat = jnp.where(mask, jnp.tile(bcols.T, (SSM_GROUPS, 1)), 0.0)
    bmat4 = bmat.reshape(2, MXU_DIM, n_tiles, MXU_DIM)
    w["s5_wb"] = jnp.stack([bmat4[j // (n_tiles // 2), :, j, :] for j in range(n_tiles)]).astype(BF16)
    ccols = jnp.stack([c_re, -c_im]).reshape(2, n_tiles, 2, SSM_GROUP, SSM_STATE).transpose(
        1, 0, 2, 4, 3).reshape(STATE_COLS, SSM_GROUP)
    cmat = jnp.where(mask.T, jnp.tile(ccols, (1, SSM_GROUPS)), 0.0)
    hc = STATE_COLS // 2
    w["s5_wc"] = jnp.stack([cmat[:hc, :MXU_DIM], cmat[hc:, MXU_DIM:]]).astype(BF16)
    w["d_skip"] = d_skip.reshape(1, SSM_WIDTH)
    w["w_glu"] = w_glu.astype(BF16)
    w["b_glu"] = b_glu.reshape(1, SSM_WIDTH)

    w["w_br_ssm"] = w_br_ssm.astype(BF16)
    w["w_br_mla"] = w_br_mla.astype(BF16)
    w["w_out"] = w_out.astype(BF16)
    w["ln1_g"], w["ln1_b"] = ln1_g.reshape(1, D_MODEL), ln1_b.reshape(1, D_MODEL)
    w["ln2_g"], w["ln2_b"] = ln2_g.reshape(1, D_MODEL), ln2_b.reshape(1, D_MODEL)

    wrg = jnp.pad(w_gr, ((0, 0), (0, LANES - N_GROUPS)))
    w["wrg"] = wrg.astype(BF16)
    w["brg"] = jnp.pad(b_gr, (0, LANES - N_GROUPS)).reshape(1, LANES)
    er = w_er.reshape(D_MODEL, N_GROUPS, EXPERTS_PER_GROUP).transpose(1, 0, 2)
    gr = jnp.broadcast_to(w_gr[None], (N_GROUPS, D_MODEL, N_GROUPS))
    wroute = jnp.pad(jnp.concatenate([er, gr], -1), ((0, 0), (0, 0), (0, LANES - EXPERTS_PER_GROUP - N_GROUPS)))
    w["wroute"] = wroute.astype(BF16)
    broute = jnp.concatenate([b_er.reshape(N_GROUPS, EXPERTS_PER_GROUP),
                              jnp.broadcast_to(b_gr[None], (N_GROUPS, N_GROUPS))], -1)
    w["broute"] = jnp.pad(broute, ((0, 0), (0, LANES - EXPERTS_PER_GROUP - N_GROUPS))).reshape(N_GROUPS, 1, LANES)
    w["w_e_gate"] = w_e_gate.astype(BF16).reshape(N_GROUPS, EXPERTS_PER_GROUP, D_MODEL, D_EXPERT)
    w["w_e_up"] = w_e_up.astype(BF16).reshape(N_GROUPS, EXPERTS_PER_GROUP, D_MODEL, D_EXPERT)
    w["w_e_down"] = w_e_down.astype(BF16).reshape(N_GROUPS, EXPERTS_PER_GROUP * D_EXPERT, D_MODEL)
    w["w_ple_gate"] = w_ple_gate.astype(BF16)
    w["b_ple_gate"] = b_ple_gate.reshape(1, D_MODEL)
    w["w_ple_proj"] = w_ple_proj.astype(BF16)
    return w


def _rope_tables(pos, reps):
    half = QK_ROPE // 2
    inv = jnp.exp(-math.log(ROPE_THETA) * jnp.arange(half, dtype=F32) / half)
    ang = pos.astype(F32)[:, None] * inv[None, :]
    cos2 = jnp.tile(jnp.cos(ang), (1, 2))
    sin2 = jnp.tile(jnp.sin(ang), (1, 2))
    slab = lambda t: jnp.tile(jnp.pad(t, ((0, 0), (QK_NOPE, HEAD_PAD - QK_NOPE - QK_ROPE))), (reps, 1))
    heads = lambda t: jnp.tile(t, (reps, N_HEADS))
    return {"cos128": slab(cos2), "sin128": slab(sin2), "cos_heads": heads(cos2), "sin_heads": heads(sin2)}


def _pack_state(s_re, s_im):
    B = s_re.shape[0]
    st = jnp.stack([s_re.reshape(B, SSM_GROUPS // 2, 2, SSM_STATE), s_im.reshape(B, SSM_GROUPS // 2, 2, SSM_STATE)],
                   axis=2)
    return st.reshape(B, STATE_COLS)


def _unpack_state(s):
    B = s.shape[0]
    st = s.reshape(B, SSM_GROUPS // 2, 2, 2, SSM_STATE)
    return (st[:, :, 0].reshape(B, SSM_GROUPS, SSM_STATE), st[:, :, 1].reshape(B, SSM_GROUPS, SSM_STATE))


def _route_plan(route, counts, *, tm, n_tok):
    g_sel = route[:, 0].astype(jnp.int32)
    rank = route[:, 1].astype(jnp.int32)
    cnt = counts[0, :N_GROUPS].astype(jnp.int32)
    padded = ((cnt + tm - 1) // tm) * tm
    ends = jnp.cumsum(padded)
    offs = ends - padded
    dest = offs[g_sel] + rank
    n_tiles = n_tok // tm + N_GROUPS
    n_valid = ends[-1] // tm
    tile_start = jnp.arange(n_tiles, dtype=jnp.int32) * tm
    tile_grp = jnp.minimum(jnp.sum(tile_start[:, None] >= ends[None, :], axis=1), N_GROUPS - 1).astype(jnp.int32)
    last = jnp.maximum(n_valid - 1, 0)
    tile_blk = jnp.minimum(jnp.arange(n_tiles, dtype=jnp.int32), last)
    tile_grp = jnp.where(jnp.arange(n_tiles) < n_valid, tile_grp, tile_grp[last])
    n_rows = n_tiles * tm
    pad_info = jnp.concatenate([offs + cnt, ends[-1:], padded - cnt, n_rows - ends[-1:]]).astype(jnp.int32)
    return dest, tile_grp, tile_blk, n_valid.reshape(1).astype(jnp.int32), pad_info, n_rows


def _token_tail(ssm, mla, gs, gm, x, p, wts, *, tm, tm_final, tm_moe):
    T = x.shape[0]
    x1, route, counts = _merge(ssm, mla, gs, gm, x, wts, tm=tm)
    dest, tile_grp, tile_blk, n_valid, pad_info, n_rows = _route_plan(route, counts, tm=tm_moe, n_tok=T)
    xs = _dispatch(x1, dest, pad_info, tm=tm, n_rows=n_rows)
    ys = _moe(xs, tile_grp, tile_blk, n_valid, wts, tm=tm_moe)
    return _final(x1, dest, ys, p, wts, tm=tm_final)


def kernel(x_prompt, x_sample, p_prompt, p_sample, cache_kv, cache_k_rope, state_ssm_re, state_ssm_im, page_table, w_in, a_re, a_im, log_dt, b_re, b_im, c_re, c_im, d_skip, w_glu, b_glu, w_br_ssm, q_norm_g, w_uq, kv_norm_g, w_uk, w_uv, w_br_mla, w_out, ln1_g, ln1_b, w_gr, b_gr, w_er, b_er, w_e_gate, w_e_up, w_e_down, ln2_g, ln2_b, w_ple_gate, b_ple_gate, w_ple_proj):
    B, S, _ = x_prompt.shape
    Bs, Ss, _ = x_sample.shape
    Tp, Ts = B * S, Bs * Ss
    wts = _prep_weights(w_in[0], a_re[0], a_im[0], log_dt[0], b_re[0], b_im[0], c_re[0], c_im[0], d_skip[0],
                        w_glu[0], b_glu[0], w_br_ssm[0], q_norm_g[0], w_uq[0], kv_norm_g[0], w_uk[0], w_uv[0],
                        w_br_mla[0], w_out[0], ln1_g[0], ln1_b[0], w_gr[0], b_gr[0], w_er[0], b_er[0],
                        w_e_gate[0], w_e_up[0], w_e_down[0], ln2_g[0], ln2_b[0], w_ple_gate[0], b_ple_gate[0],
                        w_ple_proj[0])
    tm = 256
    tm_moe = 256
    tm_tail = 1024
    tm_final = 512

    tabs_p = _rope_tables(jnp.arange(S, dtype=jnp.int32), 1)
    seq_tiles = S // tm
    u_p, ckv_p, kr_p, gs_p, gm_p, q3, k3, v3 = _proj(
        x_prompt.reshape(Tp, D_MODEL), wts, tabs_p, tm=tm, seq_tiles=seq_tiles, absorbed=False)
    zero_state = jnp.zeros((B, STATE_COLS), F32)
    ssm_p, sfin_p = _s5(u_p.reshape(B, S, SSM_WIDTH), zero_state, wts, lt=32, bb=B, name="s5_prompt")
    mla_p = _attn_prompt(q3, k3, v3, batch=B, seq=S, tq=512, tk=512)
    y_p = _token_tail(ssm_p.reshape(Tp, SSM_WIDTH), mla_p, gs_p, gm_p, x_prompt.reshape(Tp, D_MODEL),
                      p_prompt.reshape(Tp, PLE_DIM), wts, tm=tm_tail, tm_final=tm_final, tm_moe=tm_moe)

    tabs_s = _rope_tables(PAST_LEN + jnp.arange(Ss, dtype=jnp.int32), tm // Ss)
    u_s, ckv_s, kr_s, gs_s, gm_s, q_lat, q_rope = _proj(
        x_sample.reshape(Ts, D_MODEL), wts, tabs_s, tm=tm, seq_tiles=None, absorbed=True)
    s0 = _pack_state(state_ssm_re[0], state_ssm_im[0])
    ssm_s, sfin_s = _s5(u_s.reshape(Bs, Ss, SSM_WIDTH), s0, wts, lt=Ss, bb=64, name="s5_sample")
    ssm_s = ssm_s.reshape(Ts, SSM_WIDTH)
    n_pool = cache_kv.shape[1]
    o_lat = _attn_sample(page_table, q_lat.reshape(Ts * N_HEADS, KV_RANK), q_rope.reshape(Ts * N_HEADS, QK_ROPE),
                         ckv_s, kr_s, cache_kv.reshape(n_pool, PAGE_SIZE, KV_RANK),
                         cache_k_rope.reshape(n_pool, PAGE_SIZE, QK_ROPE).transpose(0, 2, 1),
                         batch=Bs, n_new=Ss, cp=64)
    mla_s = _mm(o_lat.reshape(Ts, N_HEADS * KV_RANK), wts["wuv_blockdiag"], tm=tm, out_dtype=BF16, name="uv_sample")
    y_s = _token_tail(ssm_s, mla_s, gs_s, gm_s, x_sample.reshape(Ts, D_MODEL), p_sample.reshape(Ts, PLE_DIM), wts,
                      tm=tm_tail, tm_final=tm_final, tm_moe=tm_moe)

    sre_p, sim_p = _unpack_state(sfin_p)
    sre_s, sim_s = _unpack_state(sfin_s)
    return (y_p.reshape(B, S, D_MODEL), y_s.reshape(Bs, Ss, D_MODEL),
            ckv_p.reshape(1, B, S, KV_RANK), kr_p.reshape(1, B, S, QK_ROPE),
            sre_p[None], sim_p[None],
            ckv_s.reshape(1, Bs, Ss, KV_RANK), kr_s.reshape(1, Bs, Ss, QK_ROPE),
            sre_s[None], sim_s[None])
```
